```python
import jax, jax.numpy as jnp
from jax import lax
import numpy as np

D_MODEL = 1024
BATCH = 8
SEQ = 4096
DEPTH = 2
DEC_BATCH = 32
DEC_SEQ = 16
PAST_LEN = 4096

CHUNK = 64
N_MIXERS = 2
N_ATT_LAYERS = (DEPTH + 1) // 2
N_MLP_LAYERS = DEPTH // 2
N_HEADS = 8
QK_NOPE_DIM = 128
QK_ROPE_DIM = 64
V_HEAD_DIM = 128
Q_LORA_RANK = 384
KV_LORA_RANK = 256
ROPE_THETA = 10000.0
Q_BLOCK = 128
SGU_CHUNK = 128
SGU_WIDTH = D_MODEL
SGU_GROUPS = 8
SGU_GROUP_DIM = SGU_WIDTH // SGU_GROUPS
D_FF = 4 * D_MODEL
EPS = 1e-6

kernel_name = 'hybrid_mla_chunkmlp_streaming_step'


def rmsnorm(x, g):
    xf = x.astype(jnp.float32)
    y = xf * lax.rsqrt(jnp.mean(xf * xf, axis=-1, keepdims=True) + EPS)
    return (y * g.astype(jnp.float32)).astype(x.dtype)


def rope(x, pos):
    half = QK_ROPE_DIM // 2
    inv = 1.0 / (ROPE_THETA ** (jnp.arange(half, dtype=jnp.float32) / half))
    ang = pos.astype(jnp.float32)[:, None] * inv[None, :]
    shape = (1, pos.shape[0]) + (1,) * (x.ndim - 3) + (half,)
    cos = jnp.cos(ang).reshape(shape)
    sin = jnp.sin(ang).reshape(shape)
    xf = x.astype(jnp.float32)
    x1, x2 = xf[..., :half], xf[..., half:]
    return jnp.concatenate([x1 * cos - x2 * sin, x2 * cos + x1 * sin], axis=-1).astype(x.dtype)


def ada_modulation(c, w, b):
    m = jax.nn.silu(c) @ w + b
    return jnp.split(m, 6, axis=-1)


def modulated_norm(x, g, shift, scale):
    return rmsnorm(x, g) * (1 + scale[:, None, :]) + shift[:, None, :]


def mla_project(h, pos, p):
    w_in, g_qa, g_kva, w_q_up, g_qn, g_qr, g_kr = p['w_in'], p['g_qa'], p['g_kva'], p['w_q_up'], p['g_qn'], p['g_qr'], p['g_kr']
    B, T, _ = h.shape
    a = h @ w_in
    cq, ckv, kr = jnp.split(a, [Q_LORA_RANK, Q_LORA_RANK + KV_LORA_RANK], axis=-1)
    cq = rmsnorm(cq, g_qa)
    ckv = rmsnorm(ckv, g_kva)
    q = (cq @ w_q_up).reshape(B, T, N_HEADS, QK_NOPE_DIM + QK_ROPE_DIM)
    qn = rmsnorm(q[..., :QK_NOPE_DIM], g_qn)
    qr = rope(rmsnorm(q[..., QK_NOPE_DIM:], g_qr), pos)
    kr = rope(rmsnorm(kr, g_kr), pos)
    return qn, qr, ckv, kr


def mla_keys(ckv, w_uk, g_kn):
    return rmsnorm(jnp.einsum('bkl,lhd->bkhd', ckv, w_uk), g_kn)


def mla_attend(qn, qr, kn, kr, ckv, q_pos, k_pos):
    scale = (QK_NOPE_DIM + QK_ROPE_DIM) ** -0.5
    s = jnp.einsum('bqhd,bkhd->bhqk', qn, kn) + jnp.einsum('bqhr,bkr->bhqk', qr, kr)
    s = s.astype(jnp.float32) * scale
    visible = (k_pos[None, :] // CHUNK) <= (q_pos[:, None] // CHUNK)
    s = jnp.where(visible[None, None], s, jnp.finfo(jnp.float32).min)
    pr = jax.nn.softmax(s, axis=-1).astype(ckv.dtype)
    return jnp.einsum('bhqk,bkl->bqhl', pr, ckv)


def mla_out(o_lat, w_uv, w_o):
    B, T = o_lat.shape[:2]
    o = jnp.einsum('bqhl,lhd->bqhd', o_lat, w_uv).reshape(B, T, N_HEADS * V_HEAD_DIM)
    return o @ w_o


def mla_prompt_mixer(h, pos, p):
    qn, qr, ckv, kr = mla_project(h, pos, p)
    kn = mla_keys(ckv, p['w_uk'], p['g_kn'])
    B, T = h.shape[:2]
    nb = T // Q_BLOCK
    qn_b = qn.reshape(B, nb, Q_BLOCK, N_HEADS, QK_NOPE_DIM).swapaxes(0, 1)
    qr_b = qr.reshape(B, nb, Q_BLOCK, N_HEADS, QK_ROPE_DIM).swapaxes(0, 1)
    pos_b = pos.reshape(nb, Q_BLOCK)
    o = lax.map(lambda a: mla_attend(a[0], a[1], kn, kr, ckv, a[2], pos), (qn_b, qr_b, pos_b))
    o = o.swapaxes(0, 1).reshape(B, T, N_HEADS, KV_LORA_RANK)
    return mla_out(o, p['w_uv'], p['w_o']), ckv, kr


def mla_sample_mixer(h, pos, past_ckv, past_kpe, p):
    qn, qr, ckv_new, kr_new = mla_project(h, pos, p)
    ckv_all = jnp.concatenate([past_ckv, ckv_new], axis=1)
    kr_all = jnp.concatenate([past_kpe, kr_new], axis=1)
    kn = mla_keys(ckv_all, p['w_uk'], p['g_kn'])
    k_pos = jnp.arange(ckv_all.shape[1])
    o = mla_attend(qn, qr, kn, kr_all, ckv_all, pos, k_pos)
    return mla_out(o, p['w_uv'], p['w_o']), ckv_new, kr_new


def sgu_mix(u, v, w_s, b_s, idx):
    ii = jnp.arange(SGU_CHUNK)
    mask = (ii[None, :] // CHUNK) <= (ii[:, None] // CHUNK)
    w = jnp.where(mask[None], w_s, 0)
    w = w[:, idx[:, None], idx[None, :]]
    bias = b_s[:, idx].T[None, None, :, :, None]
    mixed = jnp.einsum('gij,bnjgc->bnigc', w, v) + bias
    return u * mixed


def chunk_mlp_mixer(h, idx, p):
    B, T, _ = h.shape
    Tc = idx.shape[0]
    n = T // Tc
    z = jax.nn.gelu(h @ p['w_in'])
    u, v = jnp.split(z, 2, axis=-1)
    v = rmsnorm(v, p['g_v'])
    shp = (B, n, Tc, SGU_GROUPS, SGU_GROUP_DIM)
    g = sgu_mix(u.reshape(shp), v.reshape(shp), p['w_s'], p['b_s'], idx).reshape(B, T, SGU_WIDTH)
    return g @ p['w_o'], v


def sq_relu_ffn(h, w1, w2):
    return jnp.square(jax.nn.relu(h @ w1)) @ w2


def setup_inputs(seed: int = 0) -> dict:
    key = jax.random.key(seed)
    ks = iter(jax.random.split(key, 48))

    def nrm(shape, scale):
        return jax.random.normal(next(ks), shape, jnp.float32) * scale

    def gain(shape):
        return 1.0 + nrm(shape, 0.05)

    NA, NB = N_ATT_LAYERS, N_MLP_LAYERS
    return {
        'x_prompt': nrm((BATCH, SEQ, D_MODEL), 1.0),
        'x_sample': nrm((DEC_BATCH, DEC_SEQ, D_MODEL), 1.0),
        'cache_ckv': nrm((NA, DEC_BATCH, PAST_LEN, KV_LORA_RANK), 1.0),
        'cache_kpe': nrm((NA, DEC_BATCH, PAST_LEN, QK_ROPE_DIM), 1.0),
        'c_prompt': nrm((BATCH, D_MODEL), 1.0),
        'c_sample': nrm((DEC_BATCH, D_MODEL), 1.0),
        'ada_w': nrm((DEPTH, D_MODEL, 6 * D_MODEL), 0.5 * D_MODEL ** -0.5),
        'ada_b': nrm((DEPTH, 6 * D_MODEL), 0.02),
        'norm1_g': gain((DEPTH, D_MODEL)),
        'norm2_g': gain((DEPTH, D_MODEL)),
        'ffn_w1': nrm((DEPTH, D_MODEL, D_FF), D_MODEL ** -0.5),
        'ffn_w2': nrm((DEPTH, D_FF, D_MODEL), D_FF ** -0.5),
        'mla_w_in': nrm((NA, D_MODEL, Q_LORA_RANK + KV_LORA_RANK + QK_ROPE_DIM), D_MODEL ** -0.5),
        'mla_g_qa': gain((NA, Q_LORA_RANK)),
        'mla_g_kva': gain((NA, KV_LORA_RANK)),
        'mla_w_q_up': nrm((NA, Q_LORA_RANK, N_HEADS * (QK_NOPE_DIM + QK_ROPE_DIM)), Q_LORA_RANK ** -0.5),
        'mla_w_uk': nrm((NA, KV_LORA_RANK, N_HEADS, QK_NOPE_DIM), KV_LORA_RANK ** -0.5),
        'mla_w_uv': nrm((NA, KV_LORA_RANK, N_HEADS, V_HEAD_DIM), KV_LORA_RANK ** -0.5),
        'mla_g_qn': gain((NA, QK_NOPE_DIM)),
        'mla_g_qr': gain((NA, QK_ROPE_DIM)),
        'mla_g_kn': gain((NA, QK_NOPE_DIM)),
        'mla_g_kr': gain((NA, QK_ROPE_DIM)),
        'mla_w_o': nrm((NA, N_HEADS * V_HEAD_DIM, D_MODEL), (N_HEADS * V_HEAD_DIM) ** -0.5),
        'cm_w_in': nrm((NB, D_MODEL, 2 * SGU_WIDTH), D_MODEL ** -0.5),
        'cm_g_v': gain((NB, SGU_WIDTH)),
        'cm_w_s': nrm((NB, SGU_GROUPS, SGU_CHUNK, SGU_CHUNK), SGU_CHUNK ** -0.5),
        'cm_b_s': gain((NB, SGU_GROUPS, SGU_CHUNK)),
        'cm_w_o': nrm((NB, SGU_WIDTH, D_MODEL), SGU_WIDTH ** -0.5),
    }


def reference(x_prompt, x_sample, cache_ckv, cache_kpe, c_prompt, c_sample,
              ada_w, ada_b, norm1_g, norm2_g, ffn_w1, ffn_w2,
              mla_w_in, mla_g_qa, mla_g_kva, mla_w_q_up, mla_w_uk, mla_w_uv,
              mla_g_qn, mla_g_qr, mla_g_kn, mla_g_kr, mla_w_o,
              cm_w_in, cm_g_v, cm_w_s, cm_b_s, cm_w_o):
    seq_p = x_prompt.shape[1]
    seq_s = x_sample.shape[1]
    past = cache_ckv.shape[2]
    pos_p = jnp.arange(seq_p)
    pos_s = past + jnp.arange(seq_s)
    idx_p = jnp.arange(SGU_CHUNK)
    idx_s = (past + jnp.arange(seq_s)) % SGU_CHUNK

    x_p, x_s = x_prompt, x_sample
    ckv_p_l, kpe_p_l, ckv_s_l, kpe_s_l, v_s_l = [], [], [], [], []
    for i in range(DEPTH):
        sh1_p, sc1_p, gt1_p, sh2_p, sc2_p, gt2_p = ada_modulation(c_prompt, ada_w[i], ada_b[i])
        sh1_s, sc1_s, gt1_s, sh2_s, sc2_s, gt2_s = ada_modulation(c_sample, ada_w[i], ada_b[i])
        h_p = modulated_norm(x_p, norm1_g[i], sh1_p, sc1_p)
        h_s = modulated_norm(x_s, norm1_g[i], sh1_s, sc1_s)
        if i % N_MIXERS == 0:
            a = i // N_MIXERS
            p = {'w_in': mla_w_in[a], 'g_qa': mla_g_qa[a], 'g_kva': mla_g_kva[a], 'w_q_up': mla_w_q_up[a],
                 'w_uk': mla_w_uk[a], 'w_uv': mla_w_uv[a], 'g_qn': mla_g_qn[a], 'g_qr': mla_g_qr[a],
                 'g_kn': mla_g_kn[a], 'g_kr': mla_g_kr[a], 'w_o': mla_w_o[a]}
            m_p, ckv_p, kpe_p = mla_prompt_mixer(h_p, pos_p, p)
            m_s, ckv_s, kpe_s = mla_sample_mixer(h_s, pos_s, cache_ckv[a], cache_kpe[a], p)
            ckv_p_l.append(ckv_p)
            kpe_p_l.append(kpe_p)
            ckv_s_l.append(ckv_s)
            kpe_s_l.append(kpe_s)
        else:
            b = i // N_MIXERS
            p = {'w_in': cm_w_in[b], 'g_v': cm_g_v[b], 'w_s': cm_w_s[b], 'b_s': cm_b_s[b], 'w_o': cm_w_o[b]}
            m_p, _ = chunk_mlp_mixer(h_p, idx_p, p)
            m_s, v_s = chunk_mlp_mixer(h_s, idx_s, p)
            v_s_l.append(v_s)
        x_p = x_p + gt1_p[:, None, :] * m_p
        x_s = x_s + gt1_s[:, None, :] * m_s
        f_p = sq_relu_ffn(modulated_norm(x_p, norm2_g[i], sh2_p, sc2_p), ffn_w1[i], ffn_w2[i])
        f_s = sq_relu_ffn(modulated_norm(x_s, norm2_g[i], sh2_s, sc2_s), ffn_w1[i], ffn_w2[i])
        x_p = x_p + gt2_p[:, None, :] * f_p
        x_s = x_s + gt2_s[:, None, :] * f_s

    return (x_p, x_s, jnp.stack(ckv_p_l), jnp.stack(kpe_p_l), jnp.stack(ckv_s_l), jnp.stack(kpe_s_l), jnp.stack(v_s_l))
```

```python
import functools

import numpy as np
import jax
import jax.numpy as jnp
from jax import lax
from jax.experimental import pallas as pl
from jax.experimental.pallas import tpu as pltpu

F32 = jnp.float32
BF16 = jnp.bfloat16

CHUNK = 64
N_HEADS = 8
QK_NOPE_DIM = 128
QK_ROPE_DIM = 64
V_HEAD_DIM = 128
Q_LORA_RANK = 384
KV_LORA_RANK = 256
ROPE_THETA = 10000.0
SGU_CHUNK = 128
SGU_GROUPS = 8
EPS = 1e-6

LANES = 128
BF16_ROWS = 16
HEAD_PAD = 2 * LANES
VMEM_LIMIT = 56 * 1024 * 1024

NEG_INF = float(np.finfo(np.float32).min)


def _cparams(n_axes):
    return pltpu.CompilerParams(
        dimension_semantics=("arbitrary",) * n_axes,
        vmem_limit_bytes=VMEM_LIMIT,
    )


def _const_spec(shape):
    nd = len(shape)
    return pl.BlockSpec(shape, lambda *_: (0,) * nd, pipeline_mode=pl.Buffered(1))


def _dot(a, b):
    return jnp.dot(a, b, preferred_element_type=F32)


def _dot_nt(a, b):
    return lax.dot_general(a, b, (((1,), (1,)), ((), ())), preferred_element_type=F32)


def _rms(x, n):
    ms = jnp.sum(x * x, axis=-1, keepdims=True) / n
    return x * lax.rsqrt(ms + EPS)


def _rope128(x, c, s):
    half = QK_ROPE_DIM // 2
    rot = pltpu.roll(x, LANES - half, axis=1) + pltpu.roll(x, half, axis=1)
    return x * c + rot * s


def _mod_norm(x, g, shift, scale):
    return _rms(x, x.shape[-1]) * g * (1.0 + scale) + shift


def _sq_relu_ffn(hb, w1_ref, w2_ref, ff_chunk):
    d_ff = w1_ref.shape[1]
    acc = None
    for c in range(d_ff // ff_chunk):
        a = _dot(hb, w1_ref[:, c * ff_chunk:(c + 1) * ff_chunk])
        a = jnp.maximum(a, 0.0)
        a = (a * a).astype(BF16)
        part = _dot(a, w2_ref[c * ff_chunk:(c + 1) * ff_chunk, :])
        acc = part if acc is None else acc + part
    return acc


def _ada_kernel(c_ref, w_ref, b_ref, o_ref):
    s = jax.nn.silu(c_ref[...]).astype(BF16)
    o_ref[...] = _dot(s, w_ref[...].astype(BF16)) + b_ref[...]


def _ada_modulation(c_all, ada_w, ada_b):
    depth, d, n6 = ada_w.shape
    rows = c_all.shape[0]
    tn = 1536
    return pl.pallas_call(
        _ada_kernel,
        grid=(depth, n6 // tn),
        in_specs=[
            pl.BlockSpec((rows, d), lambda l, j: (0, 0)),
            pl.BlockSpec((None, d, tn), lambda l, j: (l, 0, j)),
            pl.BlockSpec((None, 1, tn), lambda l, j: (l, 0, j)),
        ],
        out_specs=pl.BlockSpec((None, rows, tn), lambda l, j: (l, 0, j)),
        out_shape=jax.ShapeDtypeStruct((depth, rows, n6), F32),
        compiler_params=_cparams(2),
        name="ada_modulation",
    )(c_all, ada_w, ada_b.reshape(depth, 1, n6))


def _mla_proj_kernel(x_ref, g1_ref, sh_ref, sc_ref, w_in_ref, g_qa_ref, g_kva_ref, g_kr_ref,
                     w_qup_ref, g_qn_ref, g_qr_ref, w_uk_ref, g_kn_ref, rc_ref, rs_ref,
                     q_ref, k_ref, ckv_ref, ckvb_ref, kr_ref):
    x = x_ref[...]
    h = _mod_norm(x, g1_ref[...], sh_ref[...], sc_ref[...]).astype(BF16)
    a = _dot(h, w_in_ref[...])
    cq = _rms(a[:, :Q_LORA_RANK], Q_LORA_RANK) * g_qa_ref[...]
    c0 = Q_LORA_RANK
    ckv = _rms(a[:, c0:c0 + KV_LORA_RANK], KV_LORA_RANK) * g_kva_ref[...]
    c1 = c0 + KV_LORA_RANK
    rc = rc_ref[...]
    rs = rs_ref[...]
    kr = _rope128(_rms(a[:, c1:c1 + LANES], QK_ROPE_DIM) * g_kr_ref[...], rc, rs)
    ckv_ref[...] = ckv
    ckv_b = ckv.astype(BF16)
    ckvb_ref[...] = ckv_b
    kr_ref[...] = kr[:, :QK_ROPE_DIM]
    kr_b = kr.astype(BF16)

    q = _dot(cq.astype(BF16), w_qup_ref[...])
    kn_all = _dot(ckv_b, w_uk_ref[...])
    g_qn = g_qn_ref[...]
    g_qr = g_qr_ref[...]
    g_kn = g_kn_ref[...]
    for hd in range(N_HEADS):
        o = hd * HEAD_PAD
        qn = _rms(q[:, o:o + QK_NOPE_DIM], QK_NOPE_DIM) * g_qn
        qr = _rope128(_rms(q[:, o + QK_NOPE_DIM:o + HEAD_PAD], QK_ROPE_DIM) * g_qr, rc, rs)
        q_ref[hd] = jnp.concatenate([qn.astype(BF16), qr.astype(BF16)], axis=-1)
        kn = _rms(kn_all[:, hd * QK_NOPE_DIM:(hd + 1) * QK_NOPE_DIM], QK_NOPE_DIM) * g_kn
        k_ref[hd] = jnp.concatenate([kn.astype(BF16), kr_b], axis=-1)


def _mla_project(x, mods, mod_spec, p, rope_c, rope_s, rope_spec, tm):
    n, d = x.shape
    row = lambda w: pl.BlockSpec((tm, w), lambda i: (i, 0))
    hrow = pl.BlockSpec((N_HEADS, tm, HEAD_PAD), lambda i: (0, i, 0))
    consts = [p["w_in"], p["g_qa"], p["g_kva"], p["g_kr"], p["w_q_up"], p["g_qn"], p["g_qr"],
              p["w_uk"], p["g_kn"]]
    return pl.pallas_call(
        _mla_proj_kernel,
        grid=(n // tm,),
        in_specs=[row(d), _const_spec(p["g1"].shape), mod_spec, mod_spec]
        + [_const_spec(c.shape) for c in consts] + [rope_spec, rope_spec],
        out_specs=[hrow, hrow, row(KV_LORA_RANK), row(KV_LORA_RANK), row(QK_ROPE_DIM)],
        out_shape=[
            jax.ShapeDtypeStruct((N_HEADS, n, HEAD_PAD), BF16),
            jax.ShapeDtypeStruct((N_HEADS, n, HEAD_PAD), BF16),
            jax.ShapeDtypeStruct((n, KV_LORA_RANK), F32),
            jax.ShapeDtypeStruct((n, KV_LORA_RANK), BF16),
            jax.ShapeDtypeStruct((n, QK_ROPE_DIM), F32),
        ],
        compiler_params=_cparams(1),
        name="mla_project",
    )(x, p["g1"], mods[0], mods[1], *consts, rope_c, rope_s)


def _prompt_attn_kernel(q_ref, k_ref, v_ref, o_ref, m_ref, l_ref, acc_ref, *, tile, scale):
    qi = pl.program_id(2)
    q = q_ref[...]
    m_ref[...] = jnp.full(m_ref.shape, NEG_INF, F32)
    l_ref[...] = jnp.zeros(l_ref.shape, F32)
    acc_ref[...] = jnp.zeros(acc_ref.shape, F32)

    def step(kt, masked):
        start = pl.multiple_of(kt * tile, tile)
        k = k_ref[pl.ds(start, tile), :]
        v = v_ref[pl.ds(start, tile), :]
        s = _dot_nt(q, k) * scale
        if masked:
            r = lax.broadcasted_iota(jnp.int32, s.shape, 0) // CHUNK
            c = lax.broadcasted_iota(jnp.int32, s.shape, 1) // CHUNK
            s = jnp.where(c <= r, s, NEG_INF)
        m_prev = m_ref[...]
        m_new = jnp.maximum(m_prev, jnp.max(s, axis=-1, keepdims=True))
        alpha = jnp.exp(m_prev - m_new)
        pr = jnp.exp(s - m_new)
        l_ref[...] = alpha * l_ref[...] + jnp.sum(pr, axis=-1, keepdims=True)
        acc_ref[...] = alpha * acc_ref[...] + _dot(pr.astype(BF16), v)
        m_ref[...] = m_new

    def body(kt, carry):
        step(kt, False)
        return carry

    lax.fori_loop(0, qi, body, 0)
    step(qi, True)
    o_ref[...] = (acc_ref[...] / l_ref[...]).astype(o_ref.dtype)


def _prompt_attention(q_cat, k_cat, ckv_b, batch, seq, tile):
    n = batch * seq
    nq = seq // tile
    scale = float((QK_NOPE_DIM + QK_ROPE_DIM) ** -0.5)
    return pl.pallas_call(
        functools.partial(_prompt_attn_kernel, tile=tile, scale=scale),
        grid=(batch, N_HEADS, nq),
        in_specs=[
            pl.BlockSpec((None, tile, HEAD_PAD), lambda b, h, i: (h, b * nq + i, 0)),
            pl.BlockSpec((None, seq, HEAD_PAD), lambda b, h, i: (h, b, 0)),
            pl.BlockSpec((seq, KV_LORA_RANK), lambda b, h, i: (b, 0)),
        ],
        out_specs=pl.BlockSpec((tile, KV_LORA_RANK), lambda b, h, i: (b * nq + i, h)),
        out_shape=jax.ShapeDtypeStruct((n, N_HEADS * KV_LORA_RANK), BF16),
        scratch_shapes=[
            pltpu.VMEM((tile, 1), F32),
            pltpu.VMEM((tile, 1), F32),
            pltpu.VMEM((tile, KV_LORA_RANK), F32),
        ],
        compiler_params=_cparams(3),
        name="prompt_attention",
    )(q_cat, k_cat, ckv_b)


def _sample_attn_kernel(q_ref, kn_ref, ckvn_ref, cckv_ref, ckpe_ref, w_uk_ref, g_kn_ref, o_ref,
                        m_ref, l_ref, acc_ref, *, tile, scale, past, n_new):
    m_ref[...] = jnp.full(m_ref.shape, NEG_INF, F32)
    l_ref[...] = jnp.zeros(l_ref.shape, F32)
    acc_ref[...] = jnp.zeros(acc_ref.shape, F32)
    g_kn = g_kn_ref[...]
    qr_all = jnp.concatenate([q_ref[hd][:, QK_NOPE_DIM:] for hd in range(N_HEADS)], axis=0)

    def update(s, v):
        m_prev = m_ref[...]
        m_new = jnp.maximum(m_prev, jnp.max(s, axis=-1, keepdims=True))
        alpha = jnp.exp(m_prev - m_new)
        pr = jnp.exp(s - m_new)
        l_ref[...] = alpha * l_ref[...] + jnp.sum(pr, axis=-1, keepdims=True)
        acc_ref[...] = alpha * acc_ref[...] + _dot(pr.astype(BF16), v)
        m_ref[...] = m_new

    def body(kt, carry):
        start = pl.multiple_of(kt * tile, tile)
        v = cckv_ref[pl.ds(start, tile), :].astype(BF16)
        kpe = ckpe_ref[pl.ds(start, tile), :].astype(BF16)
        kpe = jnp.concatenate([kpe, jnp.zeros_like(kpe)], axis=-1)
        kn_all = _dot(v, w_uk_ref[...])
        s_rope = _dot_nt(qr_all, kpe)
        parts = []
        for hd in range(N_HEADS):
            kn = _rms(kn_all[:, hd * QK_NOPE_DIM:(hd + 1) * QK_NOPE_DIM], QK_NOPE_DIM) * g_kn
            parts.append(_dot_nt(q_ref[hd][:, :QK_NOPE_DIM], kn.astype(BF16)))
        s = (jnp.concatenate(parts, axis=0) + s_rope) * scale
        update(s, v)
        return carry

    lax.fori_loop(0, past // tile, body, 0)

    s_new = jnp.concatenate([_dot_nt(q_ref[hd], kn_ref[hd]) for hd in range(N_HEADS)], axis=0) * scale
    q_pos = past + lax.broadcasted_iota(jnp.int32, s_new.shape, 0) % n_new
    k_pos = past + lax.broadcasted_iota(jnp.int32, s_new.shape, 1)
    s_new = jnp.where(k_pos // CHUNK <= q_pos // CHUNK, s_new, NEG_INF)
    update(s_new, ckvn_ref[...])

    o = (acc_ref[...] / l_ref[...]).astype(o_ref.dtype)
    for hd in range(N_HEADS):
        o_ref[:, hd * KV_LORA_RANK:(hd + 1) * KV_LORA_RANK] = o[hd * n_new:(hd + 1) * n_new, :]


def _sample_attention(q_cat, k_cat, ckv_b, cache_ckv, cache_kpe, a, w_uk, g_kn, n_new, tile):
    _, dec_batch, past, _ = cache_ckv.shape
    n = dec_batch * n_new
    scale = float((QK_NOPE_DIM + QK_ROPE_DIM) ** -0.5)
    rows = N_HEADS * n_new
    hrow = pl.BlockSpec((N_HEADS, n_new, HEAD_PAD), lambda b: (0, b, 0))
    return pl.pallas_call(
        functools.partial(_sample_attn_kernel, tile=tile, scale=scale, past=past, n_new=n_new),
        grid=(dec_batch,),
        in_specs=[
            hrow, hrow,
            pl.BlockSpec((n_new, KV_LORA_RANK), lambda b: (b, 0)),
            pl.BlockSpec((None, None, past, KV_LORA_RANK), lambda b: (a, b, 0, 0)),
            pl.BlockSpec((None, None, past, QK_ROPE_DIM), lambda b: (a, b, 0, 0)),
            _const_spec(w_uk.shape), _const_spec(g_kn.shape),
        ],
        out_specs=pl.BlockSpec((n_new, N_HEADS * KV_LORA_RANK), lambda b: (b, 0)),
        out_shape=jax.ShapeDtypeStruct((n, N_HEADS * KV_LORA_RANK), BF16),
        scratch_shapes=[
            pltpu.VMEM((rows, 1), F32),
            pltpu.VMEM((rows, 1), F32),
            pltpu.VMEM((rows, KV_LORA_RANK), F32),
        ],
        compiler_params=_cparams(1),
        name="sample_attention",
    )(q_cat, k_cat, ckv_b, cache_ckv, cache_kpe, w_uk, g_kn)


def _ffn_tail(x1, g2_ref, sh2_ref, sc2_ref, gt2_ref, w1_ref, w2_ref, ff_chunk):
    h2 = _mod_norm(x1, g2_ref[...], sh2_ref[...], sc2_ref[...]).astype(BF16)
    return x1 + gt2_ref[...] * _sq_relu_ffn(h2, w1_ref, w2_ref, ff_chunk)


def _mla_tail_kernel(x_ref, ol_ref, w_uv_ref, w_o_ref, gt1_ref, g2_ref, sh2_ref, sc2_ref, gt2_ref,
                     w1_ref, w2_ref, o_ref, *, ff_chunk):
    heads = []
    for hd in range(N_HEADS):
        ol = ol_ref[:, hd * KV_LORA_RANK:(hd + 1) * KV_LORA_RANK]
        heads.append(_dot(ol, w_uv_ref[hd]).astype(BF16))
    m = _dot(jnp.concatenate(heads, axis=-1), w_o_ref[...])
    x1 = x_ref[...] + gt1_ref[...] * m
    o_ref[...] = _ffn_tail(x1, g2_ref, sh2_ref, sc2_ref, gt2_ref, w1_ref, w2_ref, ff_chunk)


def _mla_tail(x, o_lat, mods, mod_spec, p, tm, ff_chunk):
    n, d = x.shape
    row = lambda w: pl.BlockSpec((tm, w), lambda i: (i, 0))
    consts_a = [p["w_uv"], p["w_o"]]
    consts_b = [p["w1"], p["w2"]]
    return pl.pallas_call(
        functools.partial(_mla_tail_kernel, ff_chunk=ff_chunk),
        grid=(n // tm,),
        in_specs=[row(d), row(o_lat.shape[1])] + [_const_spec(c.shape) for c in consts_a]
        + [mod_spec, _const_spec(p["g2"].shape), mod_spec, mod_spec, mod_spec]
        + [_const_spec(c.shape) for c in consts_b],
        out_specs=row(d),
        out_shape=jax.ShapeDtypeStruct((n, d), F32),
        compiler_params=_cparams(1),
        name="mla_tail_ffn",
    )(x, o_lat, *consts_a, mods[2], p["g2"], mods[3], mods[4], mods[5], *consts_b)


def _sgu_layer_kernel(x_ref, g1_ref, sh1_ref, sc1_ref, gt1_ref, w_in_ref, g_v_ref, w_s_ref, b_s_ref,
                      w_o_ref, g2_ref, sh2_ref, sc2_ref, gt2_ref, w1_ref, w2_ref, *out_refs,
                      ff_chunk, period, offset, emit_v):
    x = x_ref[...]
    tm = x.shape[0]
    width = w_o_ref.shape[0]
    gdim = width // SGU_GROUPS
    h = _mod_norm(x, g1_ref[...], sh1_ref[...], sc1_ref[...]).astype(BF16)
    z = jax.nn.gelu(_dot(h, w_in_ref[...]))
    u = z[:, :width]
    v = _rms(z[:, width:], width) * g_v_ref[...]
    if emit_v:
        out_refs[1][...] = v
    vb = v.astype(BF16)

    ri = lax.broadcasted_iota(jnp.int32, (SGU_CHUNK, SGU_CHUNK), 0)
    ci = lax.broadcasted_iota(jnp.int32, (SGU_CHUNK, SGU_CHUNK), 1)
    vis = (ri // period == ci // period) & ((ci % period + offset) // CHUNK <= (ri % period + offset) // CHUNK)
    w_mix = [jnp.where(vis, w_s_ref[g], 0.0).astype(BF16) for g in range(SGU_GROUPS)]

    rows = []
    for c in range(tm // SGU_CHUNK):
        r0 = c * SGU_CHUNK
        cols = []
        for g in range(SGU_GROUPS):
            l0 = g * gdim
            mixed = _dot(w_mix[g], vb[r0:r0 + SGU_CHUNK, l0:l0 + gdim]) + b_s_ref[g]
            cols.append((u[r0:r0 + SGU_CHUNK, l0:l0 + gdim] * mixed).astype(BF16))
        rows.append(jnp.concatenate(cols, axis=-1))
    gated = jnp.concatenate(rows, axis=0)
    x1 = x + gt1_ref[...] * _dot(gated, w_o_ref[...])
    out_refs[0][...] = _ffn_tail(x1, g2_ref, sh2_ref, sc2_ref, gt2_ref, w1_ref, w2_ref, ff_chunk)


def _sgu_layer(x, mods, mod_spec, p, tm, ff_chunk, period, offset, emit_v):
    n, d = x.shape
    width = p["w_o"].shape[0]
    row = lambda w: pl.BlockSpec((tm, w), lambda i: (i, 0))
    cs = lambda a: _const_spec(a.shape)
    out_specs = [row(d)]
    out_shape = [jax.ShapeDtypeStruct((n, d), F32)]
    if emit_v:
        out_specs.append(row(width))
        out_shape.append(jax.ShapeDtypeStruct((n, width), F32))
    return pl.pallas_call(
        functools.partial(_sgu_layer_kernel, ff_chunk=ff_chunk, period=period, offset=offset, emit_v=emit_v),
        grid=(n // tm,),
        in_specs=[row(d), cs(p["g1"]), mod_spec, mod_spec, mod_spec, cs(p["w_in"]), cs(p["g_v"]),
                  cs(p["w_s"]), cs(p["b_s"]), cs(p["w_o"]), cs(p["g2"]), mod_spec, mod_spec, mod_spec,
                  cs(p["w1"]), cs(p["w2"])],
        out_specs=out_specs,
        out_shape=out_shape,
        compiler_params=_cparams(1),
        name="sgu_layer_ffn",
    )(x, p["g1"], mods[0], mods[1], mods[2], p["w_in"], p["g_v"], p["w_s"], p["b_s"], p["w_o"],
      p["g2"], mods[3], mods[4], mods[5], p["w1"], p["w2"])


def _rope_tables(pos):
    half = QK_ROPE_DIM // 2
    inv = 1.0 / (ROPE_THETA ** (jnp.arange(half, dtype=F32) / half))
    ang = pos.astype(F32)[:, None] * inv[None, :]
    cos, sin = jnp.cos(ang), jnp.sin(ang)
    z = jnp.zeros((pos.shape[0], LANES - QK_ROPE_DIM), F32)
    return jnp.concatenate([cos, cos, z], axis=-1), jnp.concatenate([-sin, sin, z], axis=-1)


def _pad_lanes(g, n):
    return jnp.pad(g, (0, n - g.shape[0])).reshape(1, n)


def _prep_mla(a, norm1_g, norm2_g, ffn_w1, ffn_w2, layer, mla_w_in, mla_g_qa, mla_g_kva, mla_w_q_up,
              mla_w_uk, mla_w_uv, mla_g_qn, mla_g_qr, mla_g_kn, mla_g_kr, mla_w_o):
    d = mla_w_in.shape[1]
    w_in = jnp.pad(mla_w_in[a], ((0, 0), (0, LANES - QK_ROPE_DIM))).astype(BF16)
    per_head = QK_NOPE_DIM + QK_ROPE_DIM
    w_q_up = mla_w_q_up[a].reshape(Q_LORA_RANK, N_HEADS, per_head)
    w_q_up = jnp.pad(w_q_up, ((0, 0), (0, 0), (0, HEAD_PAD - per_head)))
    return {
        "g1": norm1_g[layer].reshape(1, d), "g2": norm2_g[layer].reshape(1, d),
        "w1": ffn_w1[layer].astype(BF16), "w2": ffn_w2[layer].astype(BF16),
        "w_in": w_in,
        "g_qa": mla_g_qa[a].reshape(1, -1), "g_kva": mla_g_kva[a].reshape(1, -1),
        "g_kr": _pad_lanes(mla_g_kr[a], LANES),
        "w_q_up": w_q_up.reshape(Q_LORA_RANK, N_HEADS * HEAD_PAD).astype(BF16),
        "g_qn": mla_g_qn[a].reshape(1, -1), "g_qr": _pad_lanes(mla_g_qr[a], LANES),
        "w_uk": mla_w_uk[a].reshape(KV_LORA_RANK, N_HEADS * QK_NOPE_DIM).astype(BF16),
        "g_kn": mla_g_kn[a].reshape(1, -1),
        "w_uv": jnp.swapaxes(mla_w_uv[a], 0, 1).astype(BF16),
        "w_o": mla_w_o[a].astype(BF16),
    }


def _prep_sgu(b, norm1_g, norm2_g, ffn_w1, ffn_w2, layer, cm_w_in, cm_g_v, cm_w_s, cm_b_s, cm_w_o, idx):
    d = cm_w_in.shape[1]
    period = len(idx)
    lo = int(idx[0])
    assert SGU_CHUNK % period == 0 and np.array_equal(idx, lo + np.arange(period)), idx
    reps = SGU_CHUNK // period
    w_s = jnp.tile(cm_w_s[b][:, lo:lo + period, lo:lo + period], (1, reps, reps))
    b_s = jnp.tile(cm_b_s[b][:, lo:lo + period], (1, reps))
    gdim = cm_w_o.shape[1] // SGU_GROUPS
    return {
        "g1": norm1_g[layer].reshape(1, d), "g2": norm2_g[layer].reshape(1, d),
        "w1": ffn_w1[layer].astype(BF16), "w2": ffn_w2[layer].astype(BF16),
        "w_in": cm_w_in[b].astype(BF16), "g_v": cm_g_v[b].reshape(1, -1),
        "w_s": w_s, "b_s": jnp.broadcast_to(b_s[:, :, None], (SGU_GROUPS, SGU_CHUNK, gdim)),
        "w_o": cm_w_o[b].astype(BF16),
    }, period, lo


def kernel(x_prompt, x_sample, cache_ckv, cache_kpe, c_prompt, c_sample, ada_w, ada_b, norm1_g, norm2_g,
           ffn_w1, ffn_w2, mla_w_in, mla_g_qa, mla_g_kva, mla_w_q_up, mla_w_uk, mla_w_uv, mla_g_qn,
           mla_g_qr, mla_g_kn, mla_g_kr, mla_w_o, cm_w_in, cm_g_v, cm_w_s, cm_b_s, cm_w_o):
    batch, seq, d = x_prompt.shape
    dec_batch, dec_seq, _ = x_sample.shape
    past = cache_ckv.shape[2]
    depth = ada_w.shape[0]
    n_p, n_s = batch * seq, dec_batch * dec_seq
    tm_p, tm_s, attn_tile, ff_chunk = 512, n_s, 512, 1024
    tiles_per_seq = seq // tm_p

    pos_p = np.arange(seq)
    pos_s = past + np.arange(dec_seq)
    rc_p, rs_p = _rope_tables(jnp.asarray(pos_p))
    rc_s, rs_s = _rope_tables(jnp.asarray(np.tile(pos_s, dec_batch)))
    rope_spec_p = pl.BlockSpec((tm_p, LANES), lambda i: (i % tiles_per_seq, 0))
    rope_spec_s = pl.BlockSpec((tm_s, LANES), lambda i: (i, 0))

    n_c = batch + dec_batch
    c_all = jnp.concatenate([c_prompt, c_sample], axis=0)
    c_all = jnp.pad(c_all, ((0, -n_c % BF16_ROWS), (0, 0)))
    mod = _ada_modulation(c_all, ada_w, ada_b)[:, :n_c]
    mod_spec_p = pl.BlockSpec((None, 1, d), lambda i: (i // tiles_per_seq, 0, 0))
    mod_spec_s = pl.BlockSpec((tm_s, d), lambda i: (i, 0))

    def split_mods(layer):
        m6 = mod[layer].reshape(batch + dec_batch, 6, d)
        m_p = [m6[:batch, j].reshape(batch, 1, d) for j in range(6)]
        m_s = [jnp.repeat(m6[batch:, j], dec_seq, axis=0) for j in range(6)]
        return m_p, m_s

    x_p = x_prompt.reshape(n_p, d)
    x_s = x_sample.reshape(n_s, d)
    ckv_p_l, kpe_p_l, ckv_s_l, kpe_s_l, v_s_l = [], [], [], [], []
    for layer in range(depth):
        m_p, m_s = split_mods(layer)
        if layer % 2 == 0:
            a = layer // 2
            p = _prep_mla(a, norm1_g, norm2_g, ffn_w1, ffn_w2, layer, mla_w_in, mla_g_qa, mla_g_kva,
                          mla_w_q_up, mla_w_uk, mla_w_uv, mla_g_qn, mla_g_qr, mla_g_kn, mla_g_kr, mla_w_o)
            q_p, k_p, ckv_p, ckvb_p, kr_p = _mla_project(x_p, m_p, mod_spec_p, p, rc_p, rs_p, rope_spec_p, tm_p)
            q_s, k_s, ckv_s, ckvb_s, kr_s = _mla_project(x_s, m_s, mod_spec_s, p, rc_s, rs_s, rope_spec_s, tm_s)
            ol_p = _prompt_attention(q_p, k_p, ckvb_p, batch, seq, attn_tile)
            ol_s = _sample_attention(q_s, k_s, ckvb_s, cache_ckv, cache_kpe, a, p["w_uk"], p["g_kn"],
                                     dec_seq, attn_tile)
            x_p = _mla_tail(x_p, ol_p, m_p, mod_spec_p, p, tm_p, ff_chunk)
            x_s = _mla_tail(x_s, ol_s, m_s, mod_spec_s, p, tm_s, ff_chunk)
            ckv_p_l.append(ckv_p.reshape(batch, seq, -1))
            kpe_p_l.append(kr_p.reshape(batch, seq, -1))
            ckv_s_l.append(ckv_s.reshape(dec_batch, dec_seq, -1))
            kpe_s_l.append(kr_s.reshape(dec_batch, dec_seq, -1))
        else:
            b = layer // 2
            args = (b, norm1_g, norm2_g, ffn_w1, ffn_w2, layer, cm_w_in, cm_g_v, cm_w_s, cm_b_s, cm_w_o)
            pp, period_p, off_p = _prep_sgu(*args, np.arange(SGU_CHUNK))
            ps, period_s, off_s = _prep_sgu(*args, pos_s % SGU_CHUNK)
            (x_p,) = _sgu_layer(x_p, m_p, mod_spec_p, pp, tm_p, ff_chunk, period_p, off_p, False)
            x_s, v_s = _sgu_layer(x_s, m_s, mod_spec_s, ps, tm_s, ff_chunk, period_s, off_s, True)
            v_s_l.append(v_s.reshape(dec_batch, dec_seq, -1))

    return (x_p.reshape(batch, seq, d), x_s.reshape(dec_batch, dec_seq, d),
            jnp.stack(ckv_p_l), jnp.stack(kpe_p_l), jnp.stack(ckv_s_l), jnp.stack(kpe_s_l),
            jnp.stack(v_s_l))
```

```python
import functools

import numpy as np
import jax
import jax.numpy as jnp
from jax import lax
from jax.experimental import pallas as pl
from jax.experimental.pallas import tpu as pltpu

F32 = jnp.float32
BF16 = jnp.bfloat16

CHUNK = 64
N_HEADS = 8
QK_NOPE_DIM = 128
QK_ROPE_DIM = 64
V_HEAD_DIM = 128
Q_LORA_RANK = 384
KV_LORA_RANK = 256
ROPE_THETA = 10000.0
SGU_CHUNK = 128
SGU_GROUPS = 8
EPS = 1e-6

LANES = 128
BF16_ROWS = 16
MXU_DIM = 256
HEAD_PAD = 2 * LANES
VMEM_LIMIT = 56 * 1024 * 1024

NEG_INF = float(np.finfo(np.float32).min)
Q_SCALE = float((QK_NOPE_DIM + QK_ROPE_DIM) ** -0.5 * np.log2(np.e))


def _cparams(n_axes):
    return pltpu.CompilerParams(
        dimension_semantics=("arbitrary",) * n_axes,
        vmem_limit_bytes=VMEM_LIMIT,
    )


def _const_spec(shape):
    nd = len(shape)
    return pl.BlockSpec(shape, lambda *_: (0,) * nd, pipeline_mode=pl.Buffered(1))


def _dot(a, b):
    return jnp.dot(a, b, preferred_element_type=F32)


def _dot_nt(a, b):
    return lax.dot_general(a, b, (((1,), (1,)), ((), ())), preferred_element_type=F32)


def _rms(x, n):
    ms = jnp.sum(x * x, axis=-1, keepdims=True) / n
    return x * lax.rsqrt(ms + EPS)


def _rope128(x, c, s):
    half = QK_ROPE_DIM // 2
    rot = pltpu.roll(x, LANES - half, axis=1) + pltpu.roll(x, half, axis=1)
    return x * c + rot * s


def _mod_norm(x, g, shift, scale):
    return _rms(x, x.shape[-1]) * g * (1.0 + scale) + shift


def _sq_relu_ffn(hb, w1_ref, w2_ref, ff_chunk):
    d_ff = w1_ref.shape[1]
    acc = None
    for c in range(d_ff // ff_chunk):
        a = _dot(hb, w1_ref[:, c * ff_chunk:(c + 1) * ff_chunk])
        a = jnp.maximum(a, 0.0)
        a = (a * a).astype(BF16)
        part = _dot(a, w2_ref[c * ff_chunk:(c + 1) * ff_chunk, :])
        acc = part if acc is None else acc + part
    return acc


def _ada_kernel(c_ref, w_ref, b_ref, o_ref):
    s = jax.nn.silu(c_ref[...]).astype(BF16)
    o_ref[...] = _dot(s, w_ref[...].astype(BF16)) + b_ref[...]


def _ada_modulation(c_all, ada_w, ada_b):
    depth, d, n6 = ada_w.shape
    rows = c_all.shape[0]
    tn = 1536
    return pl.pallas_call(
        _ada_kernel,
        grid=(depth, n6 // tn),
        in_specs=[
            pl.BlockSpec((rows, d), lambda l, j: (0, 0)),
            pl.BlockSpec((None, d, tn), lambda l, j: (l, 0, j)),
            pl.BlockSpec((None, 1, tn), lambda l, j: (l, 0, j)),
        ],
        out_specs=pl.BlockSpec((None, rows, tn), lambda l, j: (l, 0, j)),
        out_shape=jax.ShapeDtypeStruct((depth, rows, n6), F32),
        compiler_params=_cparams(2),
        name="ada_modulation",
    )(c_all, ada_w, ada_b.reshape(depth, 1, n6))


def _mla_proj_kernel(x_ref, g1_ref, sh_ref, sc_ref, w_in_ref, g_qa_ref, g_kva_ref, g_kr_ref,
                     w_qup_ref, g_qn_ref, g_qr_ref, w_uk_ref, g_kn_ref, rc_ref, rs_ref,
                     q_ref, k_ref, ckv_ref, ckvb_ref, ckvt_ref, kr_ref):
    x = x_ref[...]
    h = _mod_norm(x, g1_ref[...], sh_ref[...], sc_ref[...]).astype(BF16)
    a = _dot(h, w_in_ref[...])
    cq = _rms(a[:, :Q_LORA_RANK], Q_LORA_RANK) * g_qa_ref[...]
    c0 = Q_LORA_RANK
    ckv = _rms(a[:, c0:c0 + KV_LORA_RANK], KV_LORA_RANK) * g_kva_ref[...]
    c1 = c0 + KV_LORA_RANK
    rc = rc_ref[...]
    rs = rs_ref[...]
    kr = _rope128(_rms(a[:, c1:c1 + LANES], QK_ROPE_DIM) * g_kr_ref[...], rc, rs)
    ckv_ref[...] = ckv
    ckv_b = ckv.astype(BF16)
    ckvb_ref[...] = ckv_b
    ckvt_ref[...] = ckv.T.astype(BF16)
    kr_ref[...] = kr[:, :QK_ROPE_DIM]
    kr_b = kr.astype(BF16)

    q = _dot(cq.astype(BF16), w_qup_ref[...])
    kn_all = _dot(ckv_b, w_uk_ref[...])
    g_qn = g_qn_ref[...] * Q_SCALE
    g_qr = g_qr_ref[...] * Q_SCALE
    g_kn = g_kn_ref[...]
    for hd in range(N_HEADS):
        o = hd * HEAD_PAD
        qn = _rms(q[:, o:o + QK_NOPE_DIM], QK_NOPE_DIM) * g_qn
        qr = _rope128(_rms(q[:, o + QK_NOPE_DIM:o + HEAD_PAD], QK_ROPE_DIM) * g_qr, rc, rs)
        q_ref[hd] = jnp.concatenate([qn.astype(BF16), qr.astype(BF16)], axis=-1)
        kn = _rms(kn_all[:, hd * QK_NOPE_DIM:(hd + 1) * QK_NOPE_DIM], QK_NOPE_DIM) * g_kn
        k_ref[hd] = jnp.concatenate([kn.astype(BF16), kr_b], axis=-1)


def _mla_project(x, mods, mod_spec, p, rope_c, rope_s, rope_spec, tm):
    n, d = x.shape
    row = lambda w: pl.BlockSpec((tm, w), lambda i: (i, 0))
    hrow = pl.BlockSpec((N_HEADS, tm, HEAD_PAD), lambda i: (0, i, 0))
    consts = [p["w_in"], p["g_qa"], p["g_kva"], p["g_kr"], p["w_q_up"], p["g_qn"], p["g_qr"],
              p["w_uk"], p["g_kn"]]
    return pl.pallas_call(
        _mla_proj_kernel,
        grid=(n // tm,),
        in_specs=[row(d), _const_spec(p["g1"].shape), mod_spec, mod_spec]
        + [_const_spec(c.shape) for c in consts] + [rope_spec, rope_spec],
        out_specs=[hrow, hrow, row(KV_LORA_RANK), row(KV_LORA_RANK),
                   pl.BlockSpec((None, KV_LORA_RANK, tm), lambda i: (i, 0, 0)), row(QK_ROPE_DIM)],
        out_shape=[
            jax.ShapeDtypeStruct((N_HEADS, n, HEAD_PAD), BF16),
            jax.ShapeDtypeStruct((N_HEADS, n, HEAD_PAD), BF16),
            jax.ShapeDtypeStruct((n, KV_LORA_RANK), F32),
            jax.ShapeDtypeStruct((n, KV_LORA_RANK), BF16),
            jax.ShapeDtypeStruct((n // tm, KV_LORA_RANK, tm), BF16),
            jax.ShapeDtypeStruct((n, QK_ROPE_DIM), F32),
        ],
        compiler_params=_cparams(1),
        name="mla_project",
    )(x, p["g1"], mods[0], mods[1], *consts, rope_c, rope_s)


def _prompt_attn_kernel(q_ref, k_ref, vt_ref, o_ref, m_ref, l_ref, acc_ref, *, tile, heads):
    qi = pl.program_id(2)
    m_ref[...] = jnp.full(m_ref.shape, NEG_INF, F32)
    l_ref[...] = jnp.zeros(l_ref.shape, F32)
    acc_ref[...] = jnp.zeros(acc_ref.shape, F32)

    def scores(kt):
        start = pl.multiple_of(kt * tile, tile)
        return tuple(_dot_nt(k_ref[g, pl.ds(start, tile), :], q_ref[g]) for g in range(heads))

    def step(kt, sts, masked):
        vt = vt_ref[kt]
        for g in range(heads):
            for w in range(tile // MXU_DIM):
                alphas, probs = [], []
                for j in range(w * MXU_DIM // LANES, (w + 1) * MXU_DIM // LANES):
                    cols = slice(j * LANES, (j + 1) * LANES)
                    col = sts[g][:, cols]
                    if masked:
                        kc = lax.broadcasted_iota(jnp.int32, col.shape, 0) // CHUNK
                        qc = (lax.broadcasted_iota(jnp.int32, col.shape, 1) + j * LANES) // CHUNK
                        col = jnp.where(kc <= qc, col, NEG_INF)
                    m_prev = m_ref[g, :, cols]
                    m_new = jnp.maximum(m_prev, jnp.max(col, axis=0, keepdims=True))
                    alpha = jnp.exp2(m_prev - m_new)
                    pr = jnp.exp2(col - m_new)
                    l_ref[g, :, cols] = alpha * l_ref[g, :, cols] + jnp.sum(pr, axis=0, keepdims=True)
                    m_ref[g, :, cols] = m_new
                    probs.append(pr.astype(BF16))
                    alphas.append(alpha)
                wide = slice(w * MXU_DIM, (w + 1) * MXU_DIM)
                acc_ref[g, :, wide] = (acc_ref[g, :, wide] * jnp.concatenate(alphas, axis=-1)
                                       + _dot(vt, jnp.concatenate(probs, axis=-1)))

    def body(kt, sts):
        nxt = scores(kt + 1)
        step(kt, sts, False)
        return nxt

    sts = lax.fori_loop(0, qi, body, scores(0))
    step(qi, sts, True)
    for g in range(heads):
        o_ref[g * KV_LORA_RANK:(g + 1) * KV_LORA_RANK, :] = (acc_ref[g] / l_ref[g]).astype(o_ref.dtype)


def _prompt_attention(q_cat, k_cat, ckv_t, batch, seq, tile, heads):
    n = batch * seq
    nq = seq // tile
    assert ckv_t.shape == (n // tile, KV_LORA_RANK, tile), ckv_t.shape
    return pl.pallas_call(
        functools.partial(_prompt_attn_kernel, tile=tile, heads=heads),
        grid=(batch, N_HEADS // heads, nq),
        in_specs=[
            pl.BlockSpec((heads, tile, HEAD_PAD), lambda b, h, i: (h, b * nq + i, 0)),
            pl.BlockSpec((heads, seq, HEAD_PAD), lambda b, h, i: (h, b, 0)),
            pl.BlockSpec((nq, KV_LORA_RANK, tile), lambda b, h, i: (b, 0, 0)),
        ],
        out_specs=pl.BlockSpec((heads * KV_LORA_RANK, tile), lambda b, h, i: (h, b * nq + i)),
        out_shape=jax.ShapeDtypeStruct((N_HEADS * KV_LORA_RANK, n), BF16),
        scratch_shapes=[
            pltpu.VMEM((heads, 1, tile), F32),
            pltpu.VMEM((heads, 1, tile), F32),
            pltpu.VMEM((heads, KV_LORA_RANK, tile), F32),
        ],
        compiler_params=_cparams(3),
        name="prompt_attention",
    )(q_cat, k_cat, ckv_t)


def _sample_attn_kernel(q_ref, kn_ref, ckvn_ref, cckv_ref, ckpe_ref, w_uk_ref, g_kn_ref, o_ref,
                        m_ref, l_ref, acc_ref, *, tile, past, n_new):
    m_ref[...] = jnp.full(m_ref.shape, NEG_INF, F32)
    l_ref[...] = jnp.zeros(l_ref.shape, F32)
    acc_ref[...] = jnp.zeros(acc_ref.shape, F32)
    g_kn = g_kn_ref[...]
    qr_all = jnp.concatenate([q_ref[hd][:, QK_NOPE_DIM:] for hd in range(N_HEADS)], axis=0)

    def update(s, v):
        m_prev = m_ref[...]
        m_new = jnp.maximum(m_prev, jnp.max(s, axis=-1, keepdims=True))
        alpha = jnp.exp2(m_prev - m_new)
        pr = jnp.exp2(s - m_new)
        l_ref[...] = alpha * l_ref[...] + jnp.sum(pr, axis=-1, keepdims=True)
        acc_ref[...] = alpha * acc_ref[...] + _dot(pr.astype(BF16), v)
        m_ref[...] = m_new

    def body(kt, carry):
        start = pl.multiple_of(kt * tile, tile)
        v = cckv_ref[pl.ds(start, tile), :].astype(BF16)
        kpe = ckpe_ref[pl.ds(start, tile), :].astype(BF16)
        kpe = jnp.concatenate([kpe, jnp.zeros_like(kpe)], axis=-1)
        kn_all = _dot(v, w_uk_ref[...])
        s_rope = _dot_nt(qr_all, kpe)
        parts = []
        for hd in range(N_HEADS):
            kn = _rms(kn_all[:, hd * QK_NOPE_DIM:(hd + 1) * QK_NOPE_DIM], QK_NOPE_DIM) * g_kn
            parts.append(_dot_nt(q_ref[hd][:, :QK_NOPE_DIM], kn.astype(BF16)))
        s = jnp.concatenate(parts, axis=0) + s_rope
        update(s, v)
        return carry

    lax.fori_loop(0, past // tile, body, 0)

    s_new = jnp.concatenate([_dot_nt(q_ref[hd], kn_ref[hd]) for hd in range(N_HEADS)], axis=0)
    q_pos = past + lax.broadcasted_iota(jnp.int32, s_new.shape, 0) % n_new
    k_pos = past + lax.broadcasted_iota(jnp.int32, s_new.shape, 1)
    s_new = jnp.where(k_pos // CHUNK <= q_pos // CHUNK, s_new, NEG_INF)
    update(s_new, ckvn_ref[...])

    o = (acc_ref[...] / l_ref[...]).astype(o_ref.dtype)
    for hd in range(N_HEADS):
        o_ref[:, hd * KV_LORA_RANK:(hd + 1) * KV_LORA_RANK] = o[hd * n_new:(hd + 1) * n_new, :]


def _sample_attention(q_cat, k_cat, ckv_b, cache_ckv, cache_kpe, a, w_uk, g_kn, n_new, tile):
    _, dec_batch, past, _ = cache_ckv.shape
    n = dec_batch * n_new
    rows = N_HEADS * n_new
    hrow = pl.BlockSpec((N_HEADS, n_new, HEAD_PAD), lambda b: (0, b, 0))
    return pl.pallas_call(
        functools.partial(_sample_attn_kernel, tile=tile, past=past, n_new=n_new),
        grid=(dec_batch,),
        in_specs=[
            hrow, hrow,
            pl.BlockSpec((n_new, KV_LORA_RANK), lambda b: (b, 0)),
            pl.BlockSpec((None, None, past, KV_LORA_RANK), lambda b: (a, b, 0, 0)),
            pl.BlockSpec((None, None, past, QK_ROPE_DIM), lambda b: (a, b, 0, 0)),
            _const_spec(w_uk.shape), _const_spec(g_kn.shape),
        ],
        out_specs=pl.BlockSpec((n_new, N_HEADS * KV_LORA_RANK), lambda b: (b, 0)),
        out_shape=jax.ShapeDtypeStruct((n, N_HEADS * KV_LORA_RANK), BF16),
        scratch_shapes=[
            pltpu.VMEM((rows, 1), F32),
            pltpu.VMEM((rows, 1), F32),
            pltpu.VMEM((rows, KV_LORA_RANK), F32),
        ],
        compiler_params=_cparams(1),
        name="sample_attention",
    )(q_cat, k_cat, ckv_b, cache_ckv, cache_kpe, w_uk, g_kn)


def _ffn_tail(x1, g2_ref, sh2_ref, sc2_ref, gt2_ref, w1_ref, w2_ref, ff_chunk):
    h2 = _mod_norm(x1, g2_ref[...], sh2_ref[...], sc2_ref[...]).astype(BF16)
    return x1 + gt2_ref[...] * _sq_relu_ffn(h2, w1_ref, w2_ref, ff_chunk)


def _mla_tail_kernel(x_ref, ol_ref, w_uv_ref, w_o_ref, gt1_ref, g2_ref, sh2_ref, sc2_ref, gt2_ref,
                     w1_ref, w2_ref, o_ref, *, ff_chunk, latent_major):
    heads = []
    for hd in range(N_HEADS):
        lat = slice(hd * KV_LORA_RANK, (hd + 1) * KV_LORA_RANK)
        if latent_major:
            up = lax.dot_general(ol_ref[lat, :], w_uv_ref[hd], (((0,), (0,)), ((), ())),
                                 preferred_element_type=F32)
        else:
            up = _dot(ol_ref[:, lat], w_uv_ref[hd])
        heads.append(up.astype(BF16))
    m = _dot(jnp.concatenate(heads, axis=-1), w_o_ref[...])
    x1 = x_ref[...] + gt1_ref[...] * m
    o_ref[...] = _ffn_tail(x1, g2_ref, sh2_ref, sc2_ref, gt2_ref, w1_ref, w2_ref, ff_chunk)


def _mla_tail(x, o_lat, mods, mod_spec, p, tm, ff_chunk, latent_major):
    n, d = x.shape
    row = lambda w: pl.BlockSpec((tm, w), lambda i: (i, 0))
    consts_a = [p["w_uv"], p["w_o"]]
    consts_b = [p["w1"], p["w2"]]
    if latent_major:
        ol_spec = pl.BlockSpec((o_lat.shape[0], tm), lambda i: (0, i))
    else:
        ol_spec = row(o_lat.shape[1])
    return pl.pallas_call(
        functools.partial(_mla_tail_kernel, ff_chunk=ff_chunk, latent_major=latent_major),
        grid=(n // tm,),
        in_specs=[row(d), ol_spec] + [_const_spec(c.shape) for c in consts_a]
        + [mod_spec, _const_spec(p["g2"].shape), mod_spec, mod_spec, mod_spec]
        + [_const_spec(c.shape) for c in consts_b],
        out_specs=row(d),
        out_shape=jax.ShapeDtypeStruct((n, d), F32),
        compiler_params=_cparams(1),
        name="mla_tail_ffn",
    )(x, o_lat, *consts_a, mods[2], p["g2"], mods[3], mods[4], mods[5], *consts_b)


def _sgu_layer_kernel(x_ref, g1_ref, sh1_ref, sc1_ref, gt1_ref, w_in_ref, g_v_ref, w_s_ref, b_s_ref,
                      w_o_ref, g2_ref, sh2_ref, sc2_ref, gt2_ref, w1_ref, w2_ref, *out_refs,
                      ff_chunk, period, offset, emit_v):
    x = x_ref[...]
    tm = x.shape[0]
    width = w_o_ref.shape[0]
    gdim = width // SGU_GROUPS
    h = _mod_norm(x, g1_ref[...], sh1_ref[...], sc1_ref[...]).astype(BF16)
    z = jax.nn.gelu(_dot(h, w_in_ref[...]))
    u = z[:, :width]
    v = _rms(z[:, width:], width) * g_v_ref[...]
    if emit_v:
        out_refs[1][...] = v
    vb = v.astype(BF16)

    ri = lax.broadcasted_iota(jnp.int32, (SGU_CHUNK, SGU_CHUNK), 0)
    ci = lax.broadcasted_iota(jnp.int32, (SGU_CHUNK, SGU_CHUNK), 1)
    vis = (ri // period == ci // period) & ((ci % period + offset) // CHUNK <= (ri % period + offset) // CHUNK)
    w_mix = [jnp.where(vis, w_s_ref[g], 0.0).astype(BF16) for g in range(SGU_GROUPS)]

    rows = []
    for c in range(tm // SGU_CHUNK):
        r0 = c * SGU_CHUNK
        cols = []
        for g in range(SGU_GROUPS):
            l0 = g * gdim
            mixed = _dot(w_mix[g], vb[r0:r0 + SGU_CHUNK, l0:l0 + gdim]) + b_s_ref[g]
            cols.append((u[r0:r0 + SGU_CHUNK, l0:l0 + gdim] * mixed).astype(BF16))
        rows.append(jnp.concatenate(cols, axis=-1))
    gated = jnp.concatenate(rows, axis=0)
    x1 = x + gt1_ref[...] * _dot(gated, w_o_ref[...])
    out_refs[0][...] = _ffn_tail(x1, g2_ref, sh2_ref, sc2_ref, gt2_ref, w1_ref, w2_ref, ff_chunk)


def _sgu_layer(x, mods, mod_spec, p, tm, ff_chunk, period, offset, emit_v):
    n, d = x.shape
    width = p["w_o"].shape[0]
    row = lambda w: pl.BlockSpec((tm, w), lambda i: (i, 0))
    cs = lambda a: _const_spec(a.shape)
    out_specs = [row(d)]
    out_shape = [jax.ShapeDtypeStruct((n, d), F32)]
    if emit_v:
        out_specs.append(row(width))
        out_shape.append(jax.ShapeDtypeStruct((n, width), F32))
    return pl.pallas_call(
        functools.partial(_sgu_layer_kernel, ff_chunk=ff_chunk, period=period, offset=offset, emit_v=emit_v),
        grid=(n // tm,),
        in_specs=[row(d), cs(p["g1"]), mod_spec, mod_spec, mod_spec, cs(p["w_in"]), cs(p["g_v"]),
                  cs(p["w_s"]), cs(p["b_s"]), cs(p["w_o"]), cs(p["g2"]), mod_spec, mod_spec, mod_spec,
                  cs(p["w1"]), cs(p["w2"])],
        out_specs=out_specs,
        out_shape=out_shape,
        compiler_params=_cparams(1),
        name="sgu_layer_ffn",
    )(x, p["g1"], mods[0], mods[1], mods[2], p["w_in"], p["g_v"], p["w_s"], p["b_s"], p["w_o"],
      p["g2"], mods[3], mods[4], mods[5], p["w1"], p["w2"])


def _rope_tables(pos):
    half = QK_ROPE_DIM // 2
    inv = 1.0 / (ROPE_THETA ** (jnp.arange(half, dtype=F32) / half))
    ang = pos.astype(F32)[:, None] * inv[None, :]
    cos, sin = jnp.cos(ang), jnp.sin(ang)
    z = jnp.zeros((pos.shape[0], LANES - QK_ROPE_DIM), F32)
    return jnp.concatenate([cos, cos, z], axis=-1), jnp.concatenate([-sin, sin, z], axis=-1)


def _pad_lanes(g, n):
    return jnp.pad(g, (0, n - g.shape[0])).reshape(1, n)


def _prep_mla(a, norm1_g, norm2_g, ffn_w1, ffn_w2, layer, mla_w_in, mla_g_qa, mla_g_kva, mla_w_q_up,
              mla_w_uk, mla_w_uv, mla_g_qn, mla_g_qr, mla_g_kn, mla_g_kr, mla_w_o):
    d = mla_w_in.shape[1]
    w_in = jnp.pad(mla_w_in[a], ((0, 0), (0, LANES - QK_ROPE_DIM))).astype(BF16)
    per_head = QK_NOPE_DIM + QK_ROPE_DIM
    w_q_up = mla_w_q_up[a].reshape(Q_LORA_RANK, N_HEADS, per_head)
    w_q_up = jnp.pad(w_q_up, ((0, 0), (0, 0), (0, HEAD_PAD - per_head)))
    return {
        "g1": norm1_g[layer].reshape(1, d), "g2": norm2_g[layer].reshape(1, d),
        "w1": ffn_w1[layer].astype(BF16), "w2": ffn_w2[layer].astype(BF16),
        "w_in": w_in,
        "g_qa": mla_g_qa[a].reshape(1, -1), "g_kva": mla_g_kva[a].reshape(1, -1),
        "g_kr": _pad_lanes(mla_g_kr[a], LANES),
        "w_q_up": w_q_up.reshape(Q_LORA_RANK, N_HEADS * HEAD_PAD).astype(BF16),
        "g_qn": mla_g_qn[a].reshape(1, -1), "g_qr": _pad_lanes(mla_g_qr[a], LANES),
        "w_uk": mla_w_uk[a].reshape(KV_LORA_RANK, N_HEADS * QK_NOPE_DIM).astype(BF16),
        "g_kn": mla_g_kn[a].reshape(1, -1),
        "w_uv": jnp.swapaxes(mla_w_uv[a], 0, 1).astype(BF16),
        "w_o": mla_w_o[a].astype(BF16),
    }


def _prep_sgu(b, norm1_g, norm2_g, ffn_w1, ffn_w2, layer, cm_w_in, cm_g_v, cm_w_s, cm_b_s, cm_w_o, idx):
    d = cm_w_in.shape[1]
    period = len(idx)
    lo = int(idx[0])
    assert SGU_CHUNK % period == 0 and np.array_equal(idx, lo + np.arange(period)), idx
    reps = SGU_CHUNK // period
    w_s = jnp.tile(cm_w_s[b][:, lo:lo + period, lo:lo + period], (1, reps, reps))
    b_s = jnp.tile(cm_b_s[b][:, lo:lo + period], (1, reps))
    gdim = cm_w_o.shape[1] // SGU_GROUPS
    return {
        "g1": norm1_g[layer].reshape(1, d), "g2": norm2_g[layer].reshape(1, d),
        "w1": ffn_w1[layer].astype(BF16), "w2": ffn_w2[layer].astype(BF16),
        "w_in": cm_w_in[b].astype(BF16), "g_v": cm_g_v[b].reshape(1, -1),
        "w_s": w_s, "b_s": jnp.broadcast_to(b_s[:, :, None], (SGU_GROUPS, SGU_CHUNK, gdim)),
        "w_o": cm_w_o[b].astype(BF16),
    }, period, lo


def kernel(x_prompt, x_sample, cache_ckv, cache_kpe, c_prompt, c_sample, ada_w, ada_b, norm1_g, norm2_g,
           ffn_w1, ffn_w2, mla_w_in, mla_g_qa, mla_g_kva, mla_w_q_up, mla_w_uk, mla_w_uv, mla_g_qn,
           mla_g_qr, mla_g_kn, mla_g_kr, mla_w_o, cm_w_in, cm_g_v, cm_w_s, cm_b_s, cm_w_o):
    batch, seq, d = x_prompt.shape
    dec_batch, dec_seq, _ = x_sample.shape
    past = cache_ckv.shape[2]
    depth = ada_w.shape[0]
    n_p, n_s = batch * seq, dec_batch * dec_seq
    tm_p, tm_s, attn_tile, attn_heads, ff_chunk = 512, n_s, 512, 2, 1024
    tiles_per_seq = seq // tm_p

    pos_p = np.arange(seq)
    pos_s = past + np.arange(dec_seq)
    rc_p, rs_p = _rope_tables(jnp.asarray(pos_p))
    rc_s, rs_s = _rope_tables(jnp.asarray(np.tile(pos_s, dec_batch)))
    rope_spec_p = pl.BlockSpec((tm_p, LANES), lambda i: (i % tiles_per_seq, 0))
    rope_spec_s = pl.BlockSpec((tm_s, LANES), lambda i: (i, 0))

    n_c = batch + dec_batch
    c_all = jnp.concatenate([c_prompt, c_sample], axis=0)
    c_all = jnp.pad(c_all, ((0, -n_c % BF16_ROWS), (0, 0)))
    mod = _ada_modulation(c_all, ada_w, ada_b)[:, :n_c]
    mod_spec_p = pl.BlockSpec((None, 1, d), lambda i: (i // tiles_per_seq, 0, 0))
    mod_spec_s = pl.BlockSpec((tm_s, d), lambda i: (i, 0))

    def split_mods(layer):
        m6 = mod[layer].reshape(batch + dec_batch, 6, d)
        m_p = [m6[:batch, j].reshape(batch, 1, d) for j in range(6)]
        m_s = [jnp.repeat(m6[batch:, j], dec_seq, axis=0) for j in range(6)]
        return m_p, m_s

    x_p = x_prompt.reshape(n_p, d)
    x_s = x_sample.reshape(n_s, d)
    ckv_p_l, kpe_p_l, ckv_s_l, kpe_s_l, v_s_l = [], [], [], [], []
    for layer in range(depth):
        m_p, m_s = split_mods(layer)
        if layer % 2 == 0:
            a = layer // 2
            p = _prep_mla(a, norm1_g, norm2_g, ffn_w1, ffn_w2, layer, mla_w_in, mla_g_qa, mla_g_kva,
                          mla_w_q_up, mla_w_uk, mla_w_uv, mla_g_qn, mla_g_qr, mla_g_kn, mla_g_kr, mla_w_o)
            q_p, k_p, ckv_p, _, ckvt_p, kr_p = _mla_project(x_p, m_p, mod_spec_p, p, rc_p, rs_p, rope_spec_p, tm_p)
            q_s, k_s, ckv_s, ckvb_s, _, kr_s = _mla_project(x_s, m_s, mod_spec_s, p, rc_s, rs_s, rope_spec_s, tm_s)
            ol_p = _prompt_attention(q_p, k_p, ckvt_p, batch, seq, attn_tile, attn_heads)
            ol_s = _sample_attention(q_s, k_s, ckvb_s, cache_ckv, cache_kpe, a, p["w_uk"], p["g_kn"],
                                     dec_seq, attn_tile)
            x_p = _mla_tail(x_p, ol_p, m_p, mod_spec_p, p, tm_p, ff_chunk, True)
            x_s = _mla_tail(x_s, ol_s, m_s, mod_spec_s, p, tm_s, ff_chunk, False)
            ckv_p_l.append(ckv_p.reshape(batch, seq, -1))
            kpe_p_l.append(kr_p.reshape(batch, seq, -1))
            ckv_s_l.append(ckv_s.reshape(dec_batch, dec_seq, -1))
            kpe_s_l.append(kr_s.reshape(dec_batch, dec_seq, -1))
        else:
            b = layer // 2
            args = (b, norm1_g, norm2_g, ffn_w1, ffn_w2, layer, cm_w_in, cm_g_v, cm_w_s, cm_b_s, cm_w_o)
            pp, period_p, off_p = _prep_sgu(*args, np.arange(SGU_CHUNK))
            ps, period_s, off_s = _prep_sgu(*args, pos_s % SGU_CHUNK)
            (x_p,) = _sgu_layer(x_p, m_p, mod_spec_p, pp, tm_p, ff_chunk, period_p, off_p, False)
            x_s, v_s = _sgu_layer(x_s, m_s, mod_spec_s, ps, tm_s, ff_chunk, period_s, off_s, True)
            v_s_l.append(v_s.reshape(dec_batch, dec_seq, -1))

    return (x_p.reshape(batch, seq, d), x_s.reshape(dec_batch, dec_seq, d),
            jnp.stack(ckv_p_l), jnp.stack(kpe_p_l), jnp.stack(ckv_s_l), jnp.stack(kpe_s_l),
            jnp.stack(v_s_l))
```

```python
import functools

import numpy as np
import jax
import jax.numpy as jnp
from jax import lax
from jax.experimental import pallas as pl
from jax.experimental.pallas import tpu as pltpu

F32 = jnp.float32
BF16 = jnp.bfloat16

CHUNK = 64
N_HEADS = 8
QK_NOPE_DIM = 128
QK_ROPE_DIM = 64
V_HEAD_DIM = 128
Q_LORA_RANK = 384
KV_LORA_RANK = 256
ROPE_THETA = 10000.0
SGU_CHUNK = 128
SGU_GROUPS = 8
EPS = 1e-6

LANES = 128
BF16_ROWS = 16
MXU_DIM = 256
HEAD_PAD = 2 * LANES
Q_HEAD_COLS = 3 * LANES
VMEM_LIMIT = 56 * 1024 * 1024

NEG_INF = float(np.finfo(np.float32).min)
Q_SCALE = float((QK_NOPE_DIM + QK_ROPE_DIM) ** -0.5 * np.log2(np.e))


def _cparams(n_axes):
    return pltpu.CompilerParams(
        dimension_semantics=("arbitrary",) * n_axes,
        vmem_limit_bytes=VMEM_LIMIT,
    )


def _const_spec(shape):
    nd = len(shape)
    return pl.BlockSpec(shape, lambda *_: (0,) * nd, pipeline_mode=pl.Buffered(1))


def _dot(a, b):
    return jnp.dot(a, b, preferred_element_type=F32)


def _dot_nt(a, b):
    return lax.dot_general(a, b, (((1,), (1,)), ((), ())), preferred_element_type=F32)


def _rms(x, n):
    ms = jnp.sum(x * x, axis=-1, keepdims=True) / n
    return x * lax.rsqrt(ms + EPS)


def _rope128(x, c, s):
    half = QK_ROPE_DIM // 2
    rot = pltpu.roll(x, LANES - half, axis=1) + pltpu.roll(x, half, axis=1)
    return x * c + rot * s


def _mod_norm(x, g, shift, scale):
    return _rms(x, x.shape[-1]) * g * (1.0 + scale) + shift


def _sq_relu_ffn(hb, w1_ref, w2_ref, ff_chunk):
    d_ff = w1_ref.shape[1]
    acc = None
    for c in range(d_ff // ff_chunk):
        a = _dot(hb, w1_ref[:, c * ff_chunk:(c + 1) * ff_chunk])
        a = jnp.maximum(a, 0.0)
        a = (a * a).astype(BF16)
        part = _dot(a, w2_ref[c * ff_chunk:(c + 1) * ff_chunk, :])
        acc = part if acc is None else acc + part
    return acc


def _ada_kernel(c_ref, w_ref, b_ref, o_ref):
    s = jax.nn.silu(c_ref[...]).astype(BF16)
    o_ref[...] = _dot(s, w_ref[...].astype(BF16)) + b_ref[...]


def _ada_modulation(c_all, ada_w, ada_b):
    depth, d, n6 = ada_w.shape
    rows = c_all.shape[0]
    tn = 1536
    return pl.pallas_call(
        _ada_kernel,
        grid=(depth, n6 // tn),
        in_specs=[
            pl.BlockSpec((rows, d), lambda l, j: (0, 0)),
            pl.BlockSpec((None, d, tn), lambda l, j: (l, 0, j)),
            pl.BlockSpec((None, 1, tn), lambda l, j: (l, 0, j)),
        ],
        out_specs=pl.BlockSpec((None, rows, tn), lambda l, j: (l, 0, j)),
        out_shape=jax.ShapeDtypeStruct((depth, rows, n6), F32),
        compiler_params=_cparams(2),
        name="ada_modulation",
    )(c_all, ada_w, ada_b.reshape(depth, 1, n6))


def _mla_proj_kernel(x_ref, g1_ref, sh_ref, sc_ref, w_in_ref, g_qa_ref, g_kva_ref, g_kr_ref,
                     w_qup_ref, g_qn_ref, g_qr_ref, w_uk_ref, g_kn_ref, rc_ref, rs_ref,
                     q_ref, k_ref, ckv_ref, ckvb_ref, ckvt_ref, kr_ref):
    x = x_ref[...]
    h = _mod_norm(x, g1_ref[...], sh_ref[...], sc_ref[...]).astype(BF16)
    a = _dot(h, w_in_ref[...])
    cq = _rms(a[:, :Q_LORA_RANK], Q_LORA_RANK) * g_qa_ref[...]
    c0 = Q_LORA_RANK
    ckv = _rms(a[:, c0:c0 + KV_LORA_RANK], KV_LORA_RANK) * g_kva_ref[...]
    c1 = c0 + KV_LORA_RANK
    rc = rc_ref[...]
    rs = rs_ref[...]
    kr = _rope128(_rms(a[:, c1:c1 + LANES], QK_ROPE_DIM) * g_kr_ref[...], rc, rs)
    ckv_ref[...] = ckv
    ckv_b = ckv.astype(BF16)
    ckvb_ref[...] = ckv_b
    ckvt_ref[...] = ckv.T.astype(BF16)
    kr_ref[...] = kr[:, :QK_ROPE_DIM]
    kr_b = kr.astype(BF16)

    q = _dot(cq.astype(BF16), w_qup_ref[...])
    kn_all = _dot(ckv_b, w_uk_ref[...])
    g_qn = g_qn_ref[...] * Q_SCALE
    rot_c = rc * (g_qr_ref[0:1, :] * Q_SCALE)
    rot_s = rs * (g_qr_ref[1:2, :] * Q_SCALE)
    g_kn = g_kn_ref[...]
    for hd in range(N_HEADS):
        o = hd * Q_HEAD_COLS
        qn = _rms(q[:, o:o + QK_NOPE_DIM], QK_NOPE_DIM) * g_qn
        x = q[:, o + QK_NOPE_DIM:o + 2 * LANES]
        x_sw = q[:, o + 2 * LANES:o + 3 * LANES]
        inv = lax.rsqrt(jnp.sum(x * x, axis=-1, keepdims=True) / QK_ROPE_DIM + EPS)
        qr = (x * rot_c + x_sw * rot_s) * inv
        q_ref[hd] = jnp.concatenate([qn.astype(BF16), qr.astype(BF16)], axis=-1)
        kn = _rms(kn_all[:, hd * QK_NOPE_DIM:(hd + 1) * QK_NOPE_DIM], QK_NOPE_DIM) * g_kn
        k_ref[hd] = jnp.concatenate([kn.astype(BF16), kr_b], axis=-1)


def _mla_project(x, mods, mod_spec, p, rope_c, rope_s, rope_spec, tm):
    n, d = x.shape
    row = lambda w: pl.BlockSpec((tm, w), lambda i: (i, 0))
    hrow = pl.BlockSpec((N_HEADS, tm, HEAD_PAD), lambda i: (0, i, 0))
    consts = [p["w_in"], p["g_qa"], p["g_kva"], p["g_kr"], p["w_q_up"], p["g_qn"], p["g_qr"],
              p["w_uk"], p["g_kn"]]
    return pl.pallas_call(
        _mla_proj_kernel,
        grid=(n // tm,),
        in_specs=[row(d), _const_spec(p["g1"].shape), mod_spec, mod_spec]
        + [_const_spec(c.shape) for c in consts] + [rope_spec, rope_spec],
        out_specs=[hrow, hrow, row(KV_LORA_RANK), row(KV_LORA_RANK),
                   pl.BlockSpec((None, KV_LORA_RANK, tm), lambda i: (i, 0, 0)), row(QK_ROPE_DIM)],
        out_shape=[
            jax.ShapeDtypeStruct((N_HEADS, n, HEAD_PAD), BF16),
            jax.ShapeDtypeStruct((N_HEADS, n, HEAD_PAD), BF16),
            jax.ShapeDtypeStruct((n, KV_LORA_RANK), F32),
            jax.ShapeDtypeStruct((n, KV_LORA_RANK), BF16),
            jax.ShapeDtypeStruct((n // tm, KV_LORA_RANK, tm), BF16),
            jax.ShapeDtypeStruct((n, QK_ROPE_DIM), F32),
        ],
        compiler_params=_cparams(1),
        name="mla_project",
    )(x, p["g1"], mods[0], mods[1], *consts, rope_c, rope_s)


def _prompt_attn_kernel(q_ref, k_ref, vt_ref, o_ref, m_ref, l_ref, acc_ref, st_ref, *, tile, heads):
    qi = pl.program_id(2)
    m_ref[...] = jnp.full(m_ref.shape, NEG_INF, F32)
    l_ref[...] = jnp.zeros(l_ref.shape, F32)
    acc_ref[...] = jnp.zeros(acc_ref.shape, F32)

    def scores(kt, slot):
        start = pl.multiple_of(kt * tile, tile)
        for g in range(heads):
            st_ref[slot, g] = _dot_nt(k_ref[g, pl.ds(start, tile), :], q_ref[g])

    def step(kt, slot, masked):
        vt = vt_ref[kt]
        for g in range(heads):
            for w in range(tile // MXU_DIM):
                nk = (w + 1) * MXU_DIM if masked else tile
                alphas, probs = [], []
                for j in range(w * MXU_DIM // LANES, (w + 1) * MXU_DIM // LANES):
                    cols = slice(j * LANES, (j + 1) * LANES)
                    col = st_ref[slot, g, :nk, cols]
                    if masked:
                        kc = lax.broadcasted_iota(jnp.int32, col.shape, 0) // CHUNK
                        qc = (lax.broadcasted_iota(jnp.int32, col.shape, 1) + j * LANES) // CHUNK
                        col = jnp.where(kc <= qc, col, NEG_INF)
                    m_prev = m_ref[g, :, cols]
                    m_new = jnp.maximum(m_prev, jnp.max(col, axis=0, keepdims=True))
                    alpha = jnp.exp2(m_prev - m_new)
                    pr = jnp.exp2(col - m_new)
                    l_ref[g, :, cols] = alpha * l_ref[g, :, cols] + jnp.sum(pr, axis=0, keepdims=True)
                    m_ref[g, :, cols] = m_new
                    probs.append(pr.astype(BF16))
                    alphas.append(alpha)
                wide = slice(w * MXU_DIM, (w + 1) * MXU_DIM)
                acc_ref[g, :, wide] = (acc_ref[g, :, wide] * jnp.concatenate(alphas, axis=-1)
                                       + _dot(vt[:, :nk], jnp.concatenate(probs, axis=-1)))

    def body(i, carry):
        kt = 2 * i
        scores(kt + 1, 1)
        step(kt, 0, False)
        scores(kt + 2, 0)
        step(kt + 1, 1, False)
        return carry

    scores(0, 0)
    lax.fori_loop(0, qi // 2, body, 0)

    @pl.when(qi % 2 == 0)
    def _():
        step(qi, 0, True)

    @pl.when(qi % 2 == 1)
    def _():
        scores(qi, 1)
        step(qi - 1, 0, False)
        step(qi, 1, True)

    for g in range(heads):
        o_ref[g * KV_LORA_RANK:(g + 1) * KV_LORA_RANK, :] = (acc_ref[g] / l_ref[g]).astype(o_ref.dtype)


def _prompt_attention(q_cat, k_cat, ckv_t, batch, seq, tile, heads):
    n = batch * seq
    nq = seq // tile
    assert ckv_t.shape == (n // tile, KV_LORA_RANK, tile), ckv_t.shape
    return pl.pallas_call(
        functools.partial(_prompt_attn_kernel, tile=tile, heads=heads),
        grid=(batch, N_HEADS // heads, nq),
        in_specs=[
            pl.BlockSpec((heads, tile, HEAD_PAD), lambda b, h, i: (h, b * nq + i, 0)),
            pl.BlockSpec((heads, seq, HEAD_PAD), lambda b, h, i: (h, b, 0)),
            pl.BlockSpec((nq, KV_LORA_RANK, tile), lambda b, h, i: (b, 0, 0)),
        ],
        out_specs=pl.BlockSpec((heads * KV_LORA_RANK, tile), lambda b, h, i: (h, b * nq + i)),
        out_shape=jax.ShapeDtypeStruct((N_HEADS * KV_LORA_RANK, n), BF16),
        scratch_shapes=[
            pltpu.VMEM((heads, 1, tile), F32),
            pltpu.VMEM((heads, 1, tile), F32),
            pltpu.VMEM((heads, KV_LORA_RANK, tile), F32),
            pltpu.VMEM((2, heads, tile, tile), F32),
        ],
        compiler_params=_cparams(3),
        name="prompt_attention",
    )(q_cat, k_cat, ckv_t)


def _sample_attn_kernel(q_ref, kn_ref, ckvn_ref, cckv_ref, ckpe_ref, w_uk_ref, g_kn_ref, o_ref,
                        m_ref, l_ref, acc_ref, *, tile, past, n_new):
    m_ref[...] = jnp.full(m_ref.shape, NEG_INF, F32)
    l_ref[...] = jnp.zeros(l_ref.shape, F32)
    acc_ref[...] = jnp.zeros(acc_ref.shape, F32)
    g_kn = g_kn_ref[...]
    qr_all = jnp.concatenate([q_ref[hd][:, QK_NOPE_DIM:] for hd in range(N_HEADS)], axis=0)

    def update(s, v):
        m_prev = m_ref[...]
        m_new = jnp.maximum(m_prev, jnp.max(s, axis=-1, keepdims=True))
        alpha = jnp.exp2(m_prev - m_new)
        pr = jnp.exp2(s - m_new)
        l_ref[...] = alpha * l_ref[...] + jnp.sum(pr, axis=-1, keepdims=True)
        acc_ref[...] = alpha * acc_ref[...] + _dot(pr.astype(BF16), v)
        m_ref[...] = m_new

    def body(kt, carry):
        start = pl.multiple_of(kt * tile, tile)
        v = cckv_ref[pl.ds(start, tile), :].astype(BF16)
        kpe = ckpe_ref[pl.ds(start, tile), :].astype(BF16)
        kpe = jnp.concatenate([kpe, jnp.zeros_like(kpe)], axis=-1)
        kn_all = _dot(v, w_uk_ref[...])
        s_rope = _dot_nt(qr_all, kpe)
        parts = []
        for hd in range(N_HEADS):
            kn = _rms(kn_all[:, hd * QK_NOPE_DIM:(hd + 1) * QK_NOPE_DIM], QK_NOPE_DIM) * g_kn
            parts.append(_dot_nt(q_ref[hd][:, :QK_NOPE_DIM], kn.astype(BF16)))
        s = jnp.concatenate(parts, axis=0) + s_rope
        update(s, v)
        return carry

    lax.fori_loop(0, past // tile, body, 0)

    s_new = jnp.concatenate([_dot_nt(q_ref[hd], kn_ref[hd]) for hd in range(N_HEADS)], axis=0)
    q_pos = past + lax.broadcasted_iota(jnp.int32, s_new.shape, 0) % n_new
    k_pos = past + lax.broadcasted_iota(jnp.int32, s_new.shape, 1)
    s_new = jnp.where(k_pos // CHUNK <= q_pos // CHUNK, s_new, NEG_INF)
    update(s_new, ckvn_ref[...])

    o = (acc_ref[...] / l_ref[...]).astype(o_ref.dtype)
    for hd in range(N_HEADS):
        o_ref[:, hd * KV_LORA_RANK:(hd + 1) * KV_LORA_RANK] = o[hd * n_new:(hd + 1) * n_new, :]


def _sample_attention(q_cat, k_cat, ckv_b, cache_ckv, cache_kpe, a, w_uk, g_kn, n_new, tile):
    _, dec_batch, past, _ = cache_ckv.shape
    n = dec_batch * n_new
    rows = N_HEADS * n_new
    hrow = pl.BlockSpec((N_HEADS, n_new, HEAD_PAD), lambda b: (0, b, 0))
    return pl.pallas_call(
        functools.partial(_sample_attn_kernel, tile=tile, past=past, n_new=n_new),
        grid=(dec_batch,),
        in_specs=[
            hrow, hrow,
            pl.BlockSpec((n_new, KV_LORA_RANK), lambda b: (b, 0)),
            pl.BlockSpec((None, None, past, KV_LORA_RANK), lambda b: (a, b, 0, 0)),
            pl.BlockSpec((None, None, past, QK_ROPE_DIM), lambda b: (a, b, 0, 0)),
            _const_spec(w_uk.shape), _const_spec(g_kn.shape),
        ],
        out_specs=pl.BlockSpec((n_new, N_HEADS * KV_LORA_RANK), lambda b: (b, 0)),
        out_shape=jax.ShapeDtypeStruct((n, N_HEADS * KV_LORA_RANK), BF16),
        scratch_shapes=[
            pltpu.VMEM((rows, 1), F32),
            pltpu.VMEM((rows, 1), F32),
            pltpu.VMEM((rows, KV_LORA_RANK), F32),
        ],
        compiler_params=_cparams(1),
        name="sample_attention",
    )(q_cat, k_cat, ckv_b, cache_ckv, cache_kpe, w_uk, g_kn)


def _ffn_tail(x1, g2_ref, sh2_ref, sc2_ref, gt2_ref, w1_ref, w2_ref, ff_chunk):
    h2 = _mod_norm(x1, g2_ref[...], sh2_ref[...], sc2_ref[...]).astype(BF16)
    return x1 + gt2_ref[...] * _sq_relu_ffn(h2, w1_ref, w2_ref, ff_chunk)


def _mla_tail_kernel(x_ref, ol_ref, w_uv_ref, w_o_ref, gt1_ref, g2_ref, sh2_ref, sc2_ref, gt2_ref,
                     w1_ref, w2_ref, o_ref, *, ff_chunk, latent_major):
    heads = []
    for hd in range(N_HEADS):
        lat = slice(hd * KV_LORA_RANK, (hd + 1) * KV_LORA_RANK)
        if latent_major:
            up = lax.dot_general(ol_ref[lat, :], w_uv_ref[hd], (((0,), (0,)), ((), ())),
                                 preferred_element_type=F32)
        else:
            up = _dot(ol_ref[:, lat], w_uv_ref[hd])
        heads.append(up.astype(BF16))
    m = _dot(jnp.concatenate(heads, axis=-1), w_o_ref[...])
    x1 = x_ref[...] + gt1_ref[...] * m
    o_ref[...] = _ffn_tail(x1, g2_ref, sh2_ref, sc2_ref, gt2_ref, w1_ref, w2_ref, ff_chunk)


def _mla_tail(x, o_lat, mods, mod_spec, p, tm, ff_chunk, latent_major):
    n, d = x.shape
    row = lambda w: pl.BlockSpec((tm, w), lambda i: (i, 0))
    consts_a = [p["w_uv"], p["w_o"]]
    consts_b = [p["w1"], p["w2"]]
    if latent_major:
        ol_spec = pl.BlockSpec((o_lat.shape[0], tm), lambda i: (0, i))
    else:
        ol_spec = row(o_lat.shape[1])
    return pl.pallas_call(
        functools.partial(_mla_tail_kernel, ff_chunk=ff_chunk, latent_major=latent_major),
        grid=(n // tm,),
        in_specs=[row(d), ol_spec] + [_const_spec(c.shape) for c in consts_a]
        + [mod_spec, _const_spec(p["g2"].shape), mod_spec, mod_spec, mod_spec]
        + [_const_spec(c.shape) for c in consts_b],
        out_specs=row(d),
        out_shape=jax.ShapeDtypeStruct((n, d), F32),
        compiler_params=_cparams(1),
        name="mla_tail_ffn",
    )(x, o_lat, *consts_a, mods[2], p["g2"], mods[3], mods[4], mods[5], *consts_b)


def _sgu_layer_kernel(x_ref, g1_ref, sh1_ref, sc1_ref, gt1_ref, w_in_ref, g_v_ref, w_s_ref, b_s_ref,
                      w_o_ref, g2_ref, sh2_ref, sc2_ref, gt2_ref, w1_ref, w2_ref, *out_refs,
                      ff_chunk, period, offset, emit_v):
    x = x_ref[...]
    tm = x.shape[0]
    width = w_o_ref.shape[0]
    gdim = width // SGU_GROUPS
    h = _mod_norm(x, g1_ref[...], sh1_ref[...], sc1_ref[...]).astype(BF16)
    z = jax.nn.gelu(_dot(h, w_in_ref[...]))
    u = z[:, :width]
    v = _rms(z[:, width:], width) * g_v_ref[...]
    if emit_v:
        out_refs[1][...] = v
    vb = v.astype(BF16)

    ri = lax.broadcasted_iota(jnp.int32, (SGU_CHUNK, SGU_CHUNK), 0)
    ci = lax.broadcasted_iota(jnp.int32, (SGU_CHUNK, SGU_CHUNK), 1)
    vis = (ri // period == ci // period) & ((ci % period + offset) // CHUNK <= (ri % period + offset) // CHUNK)
    w_mix = [jnp.where(vis, w_s_ref[g], 0.0).astype(BF16) for g in range(SGU_GROUPS)]

    rows = []
    for c in range(tm // SGU_CHUNK):
        r0 = c * SGU_CHUNK
        cols = []
        for g in range(SGU_GROUPS):
            l0 = g * gdim
            mixed = _dot(w_mix[g], vb[r0:r0 + SGU_CHUNK, l0:l0 + gdim]) + b_s_ref[g]
            cols.append((u[r0:r0 + SGU_CHUNK, l0:l0 + gdim] * mixed).astype(BF16))
        rows.append(jnp.concatenate(cols, axis=-1))
    gated = jnp.concatenate(rows, axis=0)
    x1 = x + gt1_ref[...] * _dot(gated, w_o_ref[...])
    out_refs[0][...] = _ffn_tail(x1, g2_ref, sh2_ref, sc2_ref, gt2_ref, w1_ref, w2_ref, ff_chunk)


def _sgu_layer(x, mods, mod_spec, p, tm, ff_chunk, period, offset, emit_v):
    n, d = x.shape
    width = p["w_o"].shape[0]
    row = lambda w: pl.BlockSpec((tm, w), lambda i: (i, 0))
    cs = lambda a: _const_spec(a.shape)
    out_specs = [row(d)]
    out_shape = [jax.ShapeDtypeStruct((n, d), F32)]
    if emit_v:
        out_specs.append(row(width))
        out_shape.append(jax.ShapeDtypeStruct((n, width), F32))
    return pl.pallas_call(
        functools.partial(_sgu_layer_kernel, ff_chunk=ff_chunk, period=period, offset=offset, emit_v=emit_v),
        grid=(n // tm,),
        in_specs=[row(d), cs(p["g1"]), mod_spec, mod_spec, mod_spec, cs(p["w_in"]), cs(p["g_v"]),
                  cs(p["w_s"]), cs(p["b_s"]), cs(p["w_o"]), cs(p["g2"]), mod_spec, mod_spec, mod_spec,
                  cs(p["w1"]), cs(p["w2"])],
        out_specs=out_specs,
        out_shape=out_shape,
        compiler_params=_cparams(1),
        name="sgu_layer_ffn",
    )(x, p["g1"], mods[0], mods[1], mods[2], p["w_in"], p["g_v"], p["w_s"], p["b_s"], p["w_o"],
      p["g2"], mods[3], mods[4], mods[5], p["w1"], p["w2"])


def _rope_tables(pos):
    half = QK_ROPE_DIM // 2
    inv = 1.0 / (ROPE_THETA ** (jnp.arange(half, dtype=F32) / half))
    ang = pos.astype(F32)[:, None] * inv[None, :]
    cos, sin = jnp.cos(ang), jnp.sin(ang)
    z = jnp.zeros((pos.shape[0], LANES - QK_ROPE_DIM), F32)
    return jnp.concatenate([cos, cos, z], axis=-1), jnp.concatenate([-sin, sin, z], axis=-1)


def _pad_lanes(g, n):
    return jnp.pad(g, (0, n - g.shape[0])).reshape(1, n)


def _prep_mla(a, norm1_g, norm2_g, ffn_w1, ffn_w2, layer, mla_w_in, mla_g_qa, mla_g_kva, mla_w_q_up,
              mla_w_uk, mla_w_uv, mla_g_qn, mla_g_qr, mla_g_kn, mla_g_kr, mla_w_o):
    d = mla_w_in.shape[1]
    w_in = jnp.pad(mla_w_in[a], ((0, 0), (0, LANES - QK_ROPE_DIM))).astype(BF16)
    half = QK_ROPE_DIM // 2
    w_q_up = mla_w_q_up[a].reshape(Q_LORA_RANK, N_HEADS, QK_NOPE_DIM + QK_ROPE_DIM)
    w_r1 = w_q_up[:, :, QK_NOPE_DIM:QK_NOPE_DIM + half]
    w_r2 = w_q_up[:, :, QK_NOPE_DIM + half:]
    w_z = jnp.zeros((Q_LORA_RANK, N_HEADS, LANES - QK_ROPE_DIM), w_q_up.dtype)
    w_q_up = jnp.concatenate([w_q_up[:, :, :QK_NOPE_DIM], w_r1, w_r2, w_z, w_r2, w_r1, w_z], axis=-1)
    g_qr = mla_g_qr[a]
    g_qr = jnp.concatenate([_pad_lanes(g_qr, LANES),
                            _pad_lanes(jnp.concatenate([g_qr[half:], g_qr[:half]]), LANES)], axis=0)
    return {
        "g1": norm1_g[layer].reshape(1, d), "g2": norm2_g[layer].reshape(1, d),
        "w1": ffn_w1[layer].astype(BF16), "w2": ffn_w2[layer].astype(BF16),
        "w_in": w_in,
        "g_qa": mla_g_qa[a].reshape(1, -1), "g_kva": mla_g_kva[a].reshape(1, -1),
        "g_kr": _pad_lanes(mla_g_kr[a], LANES),
        "w_q_up": w_q_up.reshape(Q_LORA_RANK, N_HEADS * Q_HEAD_COLS).astype(BF16),
        "g_qn": mla_g_qn[a].reshape(1, -1), "g_qr": g_qr,
        "w_uk": mla_w_uk[a].reshape(KV_LORA_RANK, N_HEADS * QK_NOPE_DIM).astype(BF16),
        "g_kn": mla_g_kn[a].reshape(1, -1),
        "w_uv": jnp.swapaxes(mla_w_uv[a], 0, 1).astype(BF16),
        "w_o": mla_w_o[a].astype(BF16),
    }


def _prep_sgu(b, norm1_g, norm2_g, ffn_w1, ffn_w2, layer, cm_w_in, cm_g_v, cm_w_s, cm_b_s, cm_w_o, idx):
    d = cm_w_in.shape[1]
    period = len(idx)
    lo = int(idx[0])
    assert SGU_CHUNK % period == 0 and np.array_equal(idx, lo + np.arange(period)), idx
    reps = SGU_CHUNK // period
    w_s = jnp.tile(cm_w_s[b][:, lo:lo + period, lo:lo + period], (1, reps, reps))
    b_s = jnp.tile(cm_b_s[b][:, lo:lo + period], (1, reps))
    gdim = cm_w_o.shape[1] // SGU_GROUPS
    return {
        "g1": norm1_g[layer].reshape(1, d), "g2": norm2_g[layer].reshape(1, d),
        "w1": ffn_w1[layer].astype(BF16), "w2": ffn_w2[layer].astype(BF16),
        "w_in": cm_w_in[b].astype(BF16), "g_v": cm_g_v[b].reshape(1, -1),
        "w_s": w_s, "b_s": jnp.broadcast_to(b_s[:, :, None], (SGU_GROUPS, SGU_CHUNK, gdim)),
        "w_o": cm_w_o[b].astype(BF16),
    }, period, lo


def kernel(x_prompt, x_sample, cache_ckv, cache_kpe, c_prompt, c_sample, ada_w, ada_b, norm1_g, norm2_g,
           ffn_w1, ffn_w2, mla_w_in, mla_g_qa, mla_g_kva, mla_w_q_up, mla_w_uk, mla_w_uv, mla_g_qn,
           mla_g_qr, mla_g_kn, mla_g_kr, mla_w_o, cm_w_in, cm_g_v, cm_w_s, cm_b_s, cm_w_o):
    batch, seq, d = x_prompt.shape
    dec_batch, dec_seq, _ = x_sample.shape
    past = cache_ckv.shape[2]
    depth = ada_w.shape[0]
    n_p, n_s = batch * seq, dec_batch * dec_seq
    tm_p, tm_s, attn_tile, attn_heads, ff_chunk = 512, n_s, 512, 2, 1024
    tiles_per_seq = seq // tm_p

    pos_p = np.arange(seq)
    pos_s = past + np.arange(dec_seq)
    rc_p, rs_p = _rope_tables(jnp.asarray(pos_p))
    rc_s, rs_s = _rope_tables(jnp.asarray(np.tile(pos_s, dec_batch)))
    rope_spec_p = pl.BlockSpec((tm_p, LANES), lambda i: (i % tiles_per_seq, 0))
    rope_spec_s = pl.BlockSpec((tm_s, LANES), lambda i: (i, 0))

    n_c = batch + dec_batch
    c_all = jnp.concatenate([c_prompt, c_sample], axis=0)
    c_all = jnp.pad(c_all, ((0, -n_c % BF16_ROWS), (0, 0)))
    mod = _ada_modulation(c_all, ada_w, ada_b)[:, :n_c]
    mod_spec_p = pl.BlockSpec((None, 1, d), lambda i: (i // tiles_per_seq, 0, 0))
    mod_spec_s = pl.BlockSpec((tm_s, d), lambda i: (i, 0))

    def split_mods(layer):
        m6 = mod[layer].reshape(batch + dec_batch, 6, d)
        m_p = [m6[:batch, j].reshape(batch, 1, d) for j in range(6)]
        m_s = [jnp.repeat(m6[batch:, j], dec_seq, axis=0) for j in range(6)]
        return m_p, m_s

    x_p = x_prompt.reshape(n_p, d)
    x_s = x_sample.reshape(n_s, d)
    ckv_p_l, kpe_p_l, ckv_s_l, kpe_s_l, v_s_l = [], [], [], [], []
    for layer in range(depth):
        m_p, m_s = split_mods(layer)
        if layer % 2 == 0:
            a = layer // 2
            p = _prep_mla(a, norm1_g, norm2_g, ffn_w1, ffn_w2, layer, mla_w_in, mla_g_qa, mla_g_kva,
                          mla_w_q_up, mla_w_uk, mla_w_uv, mla_g_qn, mla_g_qr, mla_g_kn, mla_g_kr, mla_w_o)
            q_p, k_p, ckv_p, _, ckvt_p, kr_p = _mla_project(x_p, m_p, mod_spec_p, p, rc_p, rs_p, rope_spec_p, tm_p)
            q_s, k_s, ckv_s, ckvb_s, _, kr_s = _mla_project(x_s, m_s, mod_spec_s, p, rc_s, rs_s, rope_spec_s, tm_s)
            ol_p = _prompt_attention(q_p, k_p, ckvt_p, batch, seq, attn_tile, attn_heads)
            ol_s = _sample_attention(q_s, k_s, ckvb_s, cache_ckv, cache_kpe, a, p["w_uk"], p["g_kn"],
                                     dec_seq, attn_tile)
            x_p = _mla_tail(x_p, ol_p, m_p, mod_spec_p, p, tm_p, ff_chunk, True)
            x_s = _mla_tail(x_s, ol_s, m_s, mod_spec_s, p, tm_s, ff_chunk, False)
            ckv_p_l.append(ckv_p.reshape(batch, seq, -1))
            kpe_p_l.append(kr_p.reshape(batch, seq, -1))
            ckv_s_l.append(ckv_s.reshape(dec_batch, dec_seq, -1))
            kpe_s_l.append(kr_s.reshape(dec_batch, dec_seq, -1))
        else:
            b = layer // 2
            args = (b, norm1_g, norm2_g, ffn_w1, ffn_w2, layer, cm_w_in, cm_g_v, cm_w_s, cm_b_s, cm_w_o)
            pp, period_p, off_p = _prep_sgu(*args, np.arange(SGU_CHUNK))
            ps, period_s, off_s = _prep_sgu(*args, pos_s % SGU_CHUNK)
            (x_p,) = _sgu_layer(x_p, m_p, mod_spec_p, pp, tm_p, ff_chunk, period_p, off_p, False)
            x_s, v_s = _sgu_layer(x_s, m_s, mod_spec_s, ps, tm_s, ff_chunk, period_s, off_s, True)
            v_s_l.append(v_s.reshape(dec_batch, dec_seq, -1))

    return (x_p.reshape(batch, seq, d), x_s.reshape(dec_batch, dec_seq, d),
            jnp.stack(ckv_p_l), jnp.stack(kpe_p_l), jnp.stack(ckv_s_l), jnp.stack(kpe_s_l),
            jnp.stack(v_s_l))
```

```python
import functools

import numpy as np
import jax
import jax.numpy as jnp
from jax import lax
from jax.experimental import pallas as pl
from jax.experimental.pallas import tpu as pltpu

F32 = jnp.float32
BF16 = jnp.bfloat16

CHUNK = 64
N_HEADS = 8
QK_NOPE_DIM = 128
QK_ROPE_DIM = 64
V_HEAD_DIM = 128
Q_LORA_RANK = 384
KV_LORA_RANK = 256
ROPE_THETA = 10000.0
SGU_CHUNK = 128
SGU_GROUPS = 8
EPS = 1e-6

LANES = 128
BF16_ROWS = 16
MXU_DIM = 256
HEAD_PAD = 2 * LANES
Q_HEAD_COLS = 3 * LANES
VMEM_LIMIT = 56 * 1024 * 1024

NEG_INF = float(np.finfo(np.float32).min)
Q_SCALE = float((QK_NOPE_DIM + QK_ROPE_DIM) ** -0.5 * np.log2(np.e))


def _cparams(n_axes):
    return pltpu.CompilerParams(
        dimension_semantics=("arbitrary",) * n_axes,
        vmem_limit_bytes=VMEM_LIMIT,
    )


def _const_spec(shape):
    nd = len(shape)
    return pl.BlockSpec(shape, lambda *_: (0,) * nd, pipeline_mode=pl.Buffered(1))


def _dot(a, b):
    return jnp.dot(a, b, preferred_element_type=F32)


def _dot_nt(a, b):
    return lax.dot_general(a, b, (((1,), (1,)), ((), ())), preferred_element_type=F32)


def _rms(x, n):
    ms = jnp.sum(x * x, axis=-1, keepdims=True) / n
    return x * lax.rsqrt(ms + EPS)


def _rope128(x, c, s):
    half = QK_ROPE_DIM // 2
    rot = pltpu.roll(x, LANES - half, axis=1) + pltpu.roll(x, half, axis=1)
    return x * c + rot * s


def _mod_norm(x, g, shift, scale):
    return _rms(x, x.shape[-1]) * g * (1.0 + scale) + shift


def _sq_relu_ffn(hb, w1_ref, w2_ref, ff_chunk):
    d_ff = w1_ref.shape[1]
    acc = None
    for c in range(d_ff // ff_chunk):
        a = _dot(hb, w1_ref[:, c * ff_chunk:(c + 1) * ff_chunk])
        a = jnp.maximum(a, 0.0)
        a = (a * a).astype(BF16)
        part = _dot(a, w2_ref[c * ff_chunk:(c + 1) * ff_chunk, :])
        acc = part if acc is None else acc + part
    return acc


def _ada_kernel(c_ref, w_ref, b_ref, o_ref):
    s = jax.nn.silu(c_ref[...]).astype(BF16)
    o_ref[...] = _dot(s, w_ref[...].astype(BF16)) + b_ref[...]


def _ada_modulation(c_all, ada_w, ada_b):
    depth, d, n6 = ada_w.shape
    rows = c_all.shape[0]
    tn = 1536
    return pl.pallas_call(
        _ada_kernel,
        grid=(depth, n6 // tn),
        in_specs=[
            pl.BlockSpec((rows, d), lambda l, j: (0, 0)),
            pl.BlockSpec((None, d, tn), lambda l, j: (l, 0, j)),
            pl.BlockSpec((None, 1, tn), lambda l, j: (l, 0, j)),
        ],
        out_specs=pl.BlockSpec((None, rows, tn), lambda l, j: (l, 0, j)),
        out_shape=jax.ShapeDtypeStruct((depth, rows, n6), F32),
        compiler_params=_cparams(2),
        name="ada_modulation",
    )(c_all, ada_w, ada_b.reshape(depth, 1, n6))


def _mla_proj_kernel(x_ref, g1_ref, sh_ref, sc_ref, w_in_ref, g_qa_ref, g_kva_ref, g_kr_ref,
                     w_qup_ref, g_qn_ref, g_qr_ref, w_uk_ref, g_kn_ref, rc_ref, rs_ref,
                     q_ref, k_ref, ckv_ref, ckvb_ref, ckvt_ref, kr_ref):
    x = x_ref[...]
    h = _mod_norm(x, g1_ref[...], sh_ref[...], sc_ref[...]).astype(BF16)
    a = _dot(h, w_in_ref[...])
    cq = _rms(a[:, :Q_LORA_RANK], Q_LORA_RANK) * g_qa_ref[...]
    c0 = Q_LORA_RANK
    ckv = _rms(a[:, c0:c0 + KV_LORA_RANK], KV_LORA_RANK) * g_kva_ref[...]
    c1 = c0 + KV_LORA_RANK
    rc = rc_ref[...]
    rs = rs_ref[...]
    kr = _rope128(_rms(a[:, c1:c1 + LANES], QK_ROPE_DIM) * g_kr_ref[...], rc, rs)
    ckv_ref[...] = ckv
    ckv_b = ckv.astype(BF16)
    ckvb_ref[...] = ckv_b
    ckvt_ref[...] = ckv.T.astype(BF16)
    kr_ref[...] = kr[:, :QK_ROPE_DIM]
    kr_b = kr.astype(BF16)

    q = _dot(cq.astype(BF16), w_qup_ref[...])
    kn_all = _dot(ckv_b, w_uk_ref[...])
    g_qn = g_qn_ref[...] * Q_SCALE
    rot_c = rc * (g_qr_ref[0:1, :] * Q_SCALE)
    rot_s = rs * (g_qr_ref[1:2, :] * Q_SCALE)
    g_kn = g_kn_ref[...]
    for hd in range(N_HEADS):
        o = hd * Q_HEAD_COLS
        qn = _rms(q[:, o:o + QK_NOPE_DIM], QK_NOPE_DIM) * g_qn
        x = q[:, o + QK_NOPE_DIM:o + 2 * LANES]
        x_sw = q[:, o + 2 * LANES:o + 3 * LANES]
        inv = lax.rsqrt(jnp.sum(x * x, axis=-1, keepdims=True) / QK_ROPE_DIM + EPS)
        qr = (x * rot_c + x_sw * rot_s) * inv
        q_ref[hd] = jnp.concatenate([qn.astype(BF16), qr.astype(BF16)], axis=-1)
        kn = _rms(kn_all[:, hd * QK_NOPE_DIM:(hd + 1) * QK_NOPE_DIM], QK_NOPE_DIM) * g_kn
        k_ref[hd] = jnp.concatenate([kn.astype(BF16), kr_b], axis=-1)


def _mla_project(x, mods, mod_spec, p, rope_c, rope_s, rope_spec, tm):
    n, d = x.shape
    row = lambda w: pl.BlockSpec((tm, w), lambda i: (i, 0))
    hrow = pl.BlockSpec((N_HEADS, tm, HEAD_PAD), lambda i: (0, i, 0))
    consts = [p["w_in"], p["g_qa"], p["g_kva"], p["g_kr"], p["w_q_up"], p["g_qn"], p["g_qr"],
              p["w_uk"], p["g_kn"]]
    return pl.pallas_call(
        _mla_proj_kernel,
        grid=(n // tm,),
        in_specs=[row(d), _const_spec(p["g1"].shape), mod_spec, mod_spec]
        + [_const_spec(c.shape) for c in consts] + [rope_spec, rope_spec],
        out_specs=[hrow, hrow, row(KV_LORA_RANK), row(KV_LORA_RANK),
                   pl.BlockSpec((None, KV_LORA_RANK, tm), lambda i: (i, 0, 0)), row(QK_ROPE_DIM)],
        out_shape=[
            jax.ShapeDtypeStruct((N_HEADS, n, HEAD_PAD), BF16),
            jax.ShapeDtypeStruct((N_HEADS, n, HEAD_PAD), BF16),
            jax.ShapeDtypeStruct((n, KV_LORA_RANK), F32),
            jax.ShapeDtypeStruct((n, KV_LORA_RANK), BF16),
            jax.ShapeDtypeStruct((n // tm, KV_LORA_RANK, tm), BF16),
            jax.ShapeDtypeStruct((n, QK_ROPE_DIM), F32),
        ],
        compiler_params=_cparams(1),
        name="mla_project",
    )(x, p["g1"], mods[0], mods[1], *consts, rope_c, rope_s)


def _prompt_attn_kernel(q_ref, k_ref, vt_ref, o_ref, m_ref, l_ref, acc_ref, st_ref, *, tile, heads):
    qi = pl.program_id(2)
    m_ref[...] = jnp.full(m_ref.shape, NEG_INF, F32)
    l_ref[...] = jnp.zeros(l_ref.shape, F32)
    acc_ref[...] = jnp.zeros(acc_ref.shape, F32)

    def scores(kt, slot):
        start = pl.multiple_of(kt * tile, tile)
        for g in range(heads):
            st_ref[slot, g] = _dot_nt(k_ref[g, pl.ds(start, tile), :], q_ref[g])

    def step(kt, slot, masked):
        vt = vt_ref[kt]
        for g in range(heads):
            for w in range(tile // MXU_DIM):
                nk = (w + 1) * MXU_DIM if masked else tile
                alphas, probs = [], []
                for j in range(w * MXU_DIM // LANES, (w + 1) * MXU_DIM // LANES):
                    cols = slice(j * LANES, (j + 1) * LANES)
                    col = st_ref[slot, g, :nk, cols]
                    if masked:
                        kc = lax.broadcasted_iota(jnp.int32, col.shape, 0) // CHUNK
                        qc = (lax.broadcasted_iota(jnp.int32, col.shape, 1) + j * LANES) // CHUNK
                        col = jnp.where(kc <= qc, col, NEG_INF)
                    m_prev = m_ref[g, :, cols]
                    m_new = jnp.maximum(m_prev, jnp.max(col, axis=0, keepdims=True))
                    alpha = jnp.exp2(m_prev - m_new)
                    pr = jnp.exp2(col - m_new)
                    l_ref[g, :, cols] = alpha * l_ref[g, :, cols] + jnp.sum(pr, axis=0, keepdims=True)
                    m_ref[g, :, cols] = m_new
                    probs.append(pr.astype(BF16))
                    alphas.append(alpha)
                wide = slice(w * MXU_DIM, (w + 1) * MXU_DIM)
                acc_ref[g, :, wide] = (acc_ref[g, :, wide] * jnp.concatenate(alphas, axis=-1)
                                       + _dot(vt[:, :nk], jnp.concatenate(probs, axis=-1)))

    def body(i, carry):
        kt = 2 * i
        scores(kt + 1, 1)
        step(kt, 0, False)
        scores(kt + 2, 0)
        step(kt + 1, 1, False)
        return carry

    scores(0, 0)
    lax.fori_loop(0, qi // 2, body, 0)

    @pl.when(qi % 2 == 0)
    def _():
        step(qi, 0, True)

    @pl.when(qi % 2 == 1)
    def _():
        scores(qi, 1)
        step(qi - 1, 0, False)
        step(qi, 1, True)

    for g in range(heads):
        o_ref[g * KV_LORA_RANK:(g + 1) * KV_LORA_RANK, :] = (acc_ref[g] / l_ref[g]).astype(o_ref.dtype)


def _prompt_attention(q_cat, k_cat, ckv_t, batch, seq, tile, heads):
    n = batch * seq
    nq = seq // tile
    assert ckv_t.shape == (n // tile, KV_LORA_RANK, tile), ckv_t.shape
    return pl.pallas_call(
        functools.partial(_prompt_attn_kernel, tile=tile, heads=heads),
        grid=(batch, N_HEADS // heads, nq),
        in_specs=[
            pl.BlockSpec((heads, tile, HEAD_PAD), lambda b, h, i: (h, b * nq + i, 0)),
            pl.BlockSpec((heads, seq, HEAD_PAD), lambda b, h, i: (h, b, 0)),
            pl.BlockSpec((nq, KV_LORA_RANK, tile), lambda b, h, i: (b, 0, 0)),
        ],
        out_specs=pl.BlockSpec((heads * KV_LORA_RANK, tile), lambda b, h, i: (h, b * nq + i)),
        out_shape=jax.ShapeDtypeStruct((N_HEADS * KV_LORA_RANK, n), BF16),
        scratch_shapes=[
            pltpu.VMEM((heads, 1, tile), F32),
            pltpu.VMEM((heads, 1, tile), F32),
            pltpu.VMEM((heads, KV_LORA_RANK, tile), F32),
            pltpu.VMEM((2, heads, tile, tile), F32),
        ],
        compiler_params=_cparams(3),
        name="prompt_attention",
    )(q_cat, k_cat, ckv_t)


def _sample_attn_kernel(q_ref, kn_ref, ckvn_ref, cckv_ref, ckpe_ref, w_uk_ref, g_kn_ref, o_ref,
                        m_ref, l_ref, acc_ref, *, tile, past, n_new):
    m_ref[...] = jnp.full(m_ref.shape, NEG_INF, F32)
    l_ref[...] = jnp.zeros(l_ref.shape, F32)
    acc_ref[...] = jnp.zeros(acc_ref.shape, F32)
    g_kn = g_kn_ref[...]
    qr_all = jnp.concatenate([q_ref[hd][:, QK_NOPE_DIM:] for hd in range(N_HEADS)], axis=0)
    zero = jnp.zeros((n_new, QK_NOPE_DIM), BF16)
    q_bd = jnp.concatenate([
        jnp.concatenate([(q_ref[hd][:, :QK_NOPE_DIM] * g_kn).astype(BF16) if c == hd else zero
                         for c in range(N_HEADS)], axis=1)
        for hd in range(N_HEADS)], axis=0)

    def update(s, v):
        m_prev = m_ref[...]
        m_new = jnp.maximum(m_prev, jnp.max(s, axis=-1, keepdims=True))
        alpha = jnp.exp2(m_prev - m_new)
        pr = jnp.exp2(s - m_new)
        l_ref[...] = alpha * l_ref[...] + jnp.sum(pr, axis=-1, keepdims=True)
        acc_ref[...] = alpha * acc_ref[...] + _dot(pr.astype(BF16), v)
        m_ref[...] = m_new

    def up_project(kt):
        v = cckv_ref[kt * tile:(kt + 1) * tile, :].astype(BF16)
        return v, _dot_nt(w_uk_ref[...], v)

    def attend(kt, v, kn_t):
        kpe = ckpe_ref[kt * tile:(kt + 1) * tile, :].astype(BF16)
        kpe = jnp.concatenate([kpe, jnp.zeros_like(kpe)], axis=-1)
        s_rope = _dot_nt(qr_all, kpe)
        inv = []
        for hd in range(N_HEADS):
            blk = kn_t[hd * QK_NOPE_DIM:(hd + 1) * QK_NOPE_DIM, :]
            ms = jnp.sum(blk * blk, axis=0, keepdims=True) / QK_NOPE_DIM
            inv.append(jnp.broadcast_to(lax.rsqrt(ms + EPS), (n_new, tile)))
        s = _dot(q_bd, kn_t.astype(BF16)) * jnp.concatenate(inv, axis=0) + s_rope
        update(s, v)

    n_tiles = past // tile
    nxt = up_project(0)
    for kt in range(n_tiles):
        cur = nxt
        if kt + 1 < n_tiles:
            nxt = up_project(kt + 1)
        attend(kt, *cur)

    s_new = jnp.concatenate([_dot_nt(q_ref[hd], kn_ref[hd]) for hd in range(N_HEADS)], axis=0)
    q_pos = past + lax.broadcasted_iota(jnp.int32, s_new.shape, 0) % n_new
    k_pos = past + lax.broadcasted_iota(jnp.int32, s_new.shape, 1)
    s_new = jnp.where(k_pos // CHUNK <= q_pos // CHUNK, s_new, NEG_INF)
    update(s_new, ckvn_ref[...])

    o = (acc_ref[...] / l_ref[...]).astype(o_ref.dtype)
    for hd in range(N_HEADS):
        o_ref[:, hd * KV_LORA_RANK:(hd + 1) * KV_LORA_RANK] = o[hd * n_new:(hd + 1) * n_new, :]


def _sample_attention(q_cat, k_cat, ckv_b, cache_ckv, cache_kpe, a, w_uk, g_kn, n_new, tile):
    _, dec_batch, past, _ = cache_ckv.shape
    n = dec_batch * n_new
    rows = N_HEADS * n_new
    hrow = pl.BlockSpec((N_HEADS, n_new, HEAD_PAD), lambda b: (0, b, 0))
    return pl.pallas_call(
        functools.partial(_sample_attn_kernel, tile=tile, past=past, n_new=n_new),
        grid=(dec_batch,),
        in_specs=[
            hrow, hrow,
            pl.BlockSpec((n_new, KV_LORA_RANK), lambda b: (b, 0)),
            pl.BlockSpec((None, None, past, KV_LORA_RANK), lambda b: (a, b, 0, 0)),
            pl.BlockSpec((None, None, past, QK_ROPE_DIM), lambda b: (a, b, 0, 0)),
            _const_spec(w_uk.shape), _const_spec(g_kn.shape),
        ],
        out_specs=pl.BlockSpec((n_new, N_HEADS * KV_LORA_RANK), lambda b: (b, 0)),
        out_shape=jax.ShapeDtypeStruct((n, N_HEADS * KV_LORA_RANK), BF16),
        scratch_shapes=[
            pltpu.VMEM((rows, 1), F32),
            pltpu.VMEM((rows, 1), F32),
            pltpu.VMEM((rows, KV_LORA_RANK), F32),
        ],
        compiler_params=_cparams(1),
        name="sample_attention",
    )(q_cat, k_cat, ckv_b, cache_ckv, cache_kpe, w_uk, g_kn)


def _ffn_tail(x1, g2_ref, sh2_ref, sc2_ref, gt2_ref, w1_ref, w2_ref, ff_chunk):
    h2 = _mod_norm(x1, g2_ref[...], sh2_ref[...], sc2_ref[...]).astype(BF16)
    return x1 + gt2_ref[...] * _sq_relu_ffn(h2, w1_ref, w2_ref, ff_chunk)


def _mla_tail_kernel(x_ref, ol_ref, w_uv_ref, w_o_ref, gt1_ref, g2_ref, sh2_ref, sc2_ref, gt2_ref,
                     w1_ref, w2_ref, o_ref, *, ff_chunk, latent_major):
    heads = []
    for hd in range(N_HEADS):
        lat = slice(hd * KV_LORA_RANK, (hd + 1) * KV_LORA_RANK)
        if latent_major:
            up = lax.dot_general(ol_ref[lat, :], w_uv_ref[hd], (((0,), (0,)), ((), ())),
                                 preferred_element_type=F32)
        else:
            up = _dot(ol_ref[:, lat], w_uv_ref[hd])
        heads.append(up.astype(BF16))
    m = _dot(jnp.concatenate(heads, axis=-1), w_o_ref[...])
    x1 = x_ref[...] + gt1_ref[...] * m
    o_ref[...] = _ffn_tail(x1, g2_ref, sh2_ref, sc2_ref, gt2_ref, w1_ref, w2_ref, ff_chunk)


def _mla_tail(x, o_lat, mods, mod_spec, p, tm, ff_chunk, latent_major):
    n, d = x.shape
    row = lambda w: pl.BlockSpec((tm, w), lambda i: (i, 0))
    consts_a = [p["w_uv"], p["w_o"]]
    consts_b = [p["w1"], p["w2"]]
    if latent_major:
        ol_spec = pl.BlockSpec((o_lat.shape[0], tm), lambda i: (0, i))
    else:
        ol_spec = row(o_lat.shape[1])
    return pl.pallas_call(
        functools.partial(_mla_tail_kernel, ff_chunk=ff_chunk, latent_major=latent_major),
        grid=(n // tm,),
        in_specs=[row(d), ol_spec] + [_const_spec(c.shape) for c in consts_a]
        + [mod_spec, _const_spec(p["g2"].shape), mod_spec, mod_spec, mod_spec]
        + [_const_spec(c.shape) for c in consts_b],
        out_specs=row(d),
        out_shape=jax.ShapeDtypeStruct((n, d), F32),
        compiler_params=_cparams(1),
        name="mla_tail_ffn",
    )(x, o_lat, *consts_a, mods[2], p["g2"], mods[3], mods[4], mods[5], *consts_b)


def _sgu_layer_kernel(x_ref, g1_ref, sh1_ref, sc1_ref, gt1_ref, w_in_ref, g_v_ref, w_s_ref, b_s_ref,
                      w_o_ref, g2_ref, sh2_ref, sc2_ref, gt2_ref, w1_ref, w2_ref, *out_refs,
                      ff_chunk, period, offset, emit_v):
    x = x_ref[...]
    tm = x.shape[0]
    width = w_o_ref.shape[0]
    gdim = width // SGU_GROUPS
    h = _mod_norm(x, g1_ref[...], sh1_ref[...], sc1_ref[...]).astype(BF16)
    z = jax.nn.gelu(_dot(h, w_in_ref[...]))
    u = z[:, :width]
    v = _rms(z[:, width:], width) * g_v_ref[...]
    if emit_v:
        out_refs[1][...] = v
    vb = v.astype(BF16)

    ri = lax.broadcasted_iota(jnp.int32, (SGU_CHUNK, SGU_CHUNK), 0)
    ci = lax.broadcasted_iota(jnp.int32, (SGU_CHUNK, SGU_CHUNK), 1)
    vis = (ri // period == ci // period) & ((ci % period + offset) // CHUNK <= (ri % period + offset) // CHUNK)
    w_mix = [jnp.where(vis, w_s_ref[g], 0.0).astype(BF16) for g in range(SGU_GROUPS)]

    rows = []
    for c in range(tm // SGU_CHUNK):
        r0 = c * SGU_CHUNK
        cols = []
        for g in range(SGU_GROUPS):
            l0 = g * gdim
            mixed = _dot(w_mix[g], vb[r0:r0 + SGU_CHUNK, l0:l0 + gdim]) + b_s_ref[g]
            cols.append((u[r0:r0 + SGU_CHUNK, l0:l0 + gdim] * mixed).astype(BF16))
        rows.append(jnp.concatenate(cols, axis=-1))
    gated = jnp.concatenate(rows, axis=0)
    x1 = x + gt1_ref[...] * _dot(gated, w_o_ref[...])
    out_refs[0][...] = _ffn_tail(x1, g2_ref, sh2_ref, sc2_ref, gt2_ref, w1_ref, w2_ref, ff_chunk)


def _sgu_layer(x, mods, mod_spec, p, tm, ff_chunk, period, offset, emit_v):
    n, d = x.shape
    width = p["w_o"].shape[0]
    row = lambda w: pl.BlockSpec((tm, w), lambda i: (i, 0))
    cs = lambda a: _const_spec(a.shape)
    out_specs = [row(d)]
    out_shape = [jax.ShapeDtypeStruct((n, d), F32)]
    if emit_v:
        out_specs.append(row(width))
        out_shape.append(jax.ShapeDtypeStruct((n, width), F32))
    return pl.pallas_call(
        functools.partial(_sgu_layer_kernel, ff_chunk=ff_chunk, period=period, offset=offset, emit_v=emit_v),
        grid=(n // tm,),
        in_specs=[row(d), cs(p["g1"]), mod_spec, mod_spec, mod_spec, cs(p["w_in"]), cs(p["g_v"]),
                  cs(p["w_s"]), cs(p["b_s"]), cs(p["w_o"]), cs(p["g2"]), mod_spec, mod_spec, mod_spec,
                  cs(p["w1"]), cs(p["w2"])],
        out_specs=out_specs,
        out_shape=out_shape,
        compiler_params=_cparams(1),
        name="sgu_layer_ffn",
    )(x, p["g1"], mods[0], mods[1], mods[2], p["w_in"], p["g_v"], p["w_s"], p["b_s"], p["w_o"],
      p["g2"], mods[3], mods[4], mods[5], p["w1"], p["w2"])


def _rope_tables(pos):
    half = QK_ROPE_DIM // 2
    inv = 1.0 / (ROPE_THETA ** (jnp.arange(half, dtype=F32) / half))
    ang = pos.astype(F32)[:, None] * inv[None, :]
    cos, sin = jnp.cos(ang), jnp.sin(ang)
    z = jnp.zeros((pos.shape[0], LANES - QK_ROPE_DIM), F32)
    return jnp.concatenate([cos, cos, z], axis=-1), jnp.concatenate([-sin, sin, z], axis=-1)


def _pad_lanes(g, n):
    return jnp.pad(g, (0, n - g.shape[0])).reshape(1, n)


def _prep_mla(a, norm1_g, norm2_g, ffn_w1, ffn_w2, layer, mla_w_in, mla_g_qa, mla_g_kva, mla_w_q_up,
              mla_w_uk, mla_w_uv, mla_g_qn, mla_g_qr, mla_g_kn, mla_g_kr, mla_w_o):
    d = mla_w_in.shape[1]
    w_in = jnp.pad(mla_w_in[a], ((0, 0), (0, LANES - QK_ROPE_DIM))).astype(BF16)
    half = QK_ROPE_DIM // 2
    w_q_up = mla_w_q_up[a].reshape(Q_LORA_RANK, N_HEADS, QK_NOPE_DIM + QK_ROPE_DIM)
    w_r1 = w_q_up[:, :, QK_NOPE_DIM:QK_NOPE_DIM + half]
    w_r2 = w_q_up[:, :, QK_NOPE_DIM + half:]
    w_z = jnp.zeros((Q_LORA_RANK, N_HEADS, LANES - QK_ROPE_DIM), w_q_up.dtype)
    w_q_up = jnp.concatenate([w_q_up[:, :, :QK_NOPE_DIM], w_r1, w_r2, w_z, w_r2, w_r1, w_z], axis=-1)
    g_qr = mla_g_qr[a]
    g_qr = jnp.concatenate([_pad_lanes(g_qr, LANES),
                            _pad_lanes(jnp.concatenate([g_qr[half:], g_qr[:half]]), LANES)], axis=0)
    return {
        "g1": norm1_g[layer].reshape(1, d), "g2": norm2_g[layer].reshape(1, d),
        "w1": ffn_w1[layer].astype(BF16), "w2": ffn_w2[layer].astype(BF16),
        "w_in": w_in,
        "g_qa": mla_g_qa[a].reshape(1, -1), "g_kva": mla_g_kva[a].reshape(1, -1),
        "g_kr": _pad_lanes(mla_g_kr[a], LANES),
        "w_q_up": w_q_up.reshape(Q_LORA_RANK, N_HEADS * Q_HEAD_COLS).astype(BF16),
        "g_qn": mla_g_qn[a].reshape(1, -1), "g_qr": g_qr,
        "w_uk": mla_w_uk[a].reshape(KV_LORA_RANK, N_HEADS * QK_NOPE_DIM).astype(BF16),
        "g_kn": mla_g_kn[a].reshape(1, -1),
        "w_uv": jnp.swapaxes(mla_w_uv[a], 0, 1).astype(BF16),
        "w_o": mla_w_o[a].astype(BF16),
    }


def _prep_sgu(b, norm1_g, norm2_g, ffn_w1, ffn_w2, layer, cm_w_in, cm_g_v, cm_w_s, cm_b_s, cm_w_o, idx):
    d = cm_w_in.shape[1]
    period = len(idx)
    lo = int(idx[0])
    assert SGU_CHUNK % period == 0 and np.array_equal(idx, lo + np.arange(period)), idx
    reps = SGU_CHUNK // period
    w_s = jnp.tile(cm_w_s[b][:, lo:lo + period, lo:lo + period], (1, reps, reps))
    b_s = jnp.tile(cm_b_s[b][:, lo:lo + period], (1, reps))
    gdim = cm_w_o.shape[1] // SGU_GROUPS
    return {
        "g1": norm1_g[layer].reshape(1, d), "g2": norm2_g[layer].reshape(1, d),
        "w1": ffn_w1[layer].astype(BF16), "w2": ffn_w2[layer].astype(BF16),
        "w_in": cm_w_in[b].astype(BF16), "g_v": cm_g_v[b].reshape(1, -1),
        "w_s": w_s, "b_s": jnp.broadcast_to(b_s[:, :, None], (SGU_GROUPS, SGU_CHUNK, gdim)),
        "w_o": cm_w_o[b].astype(BF16),
    }, period, lo


def kernel(x_prompt, x_sample, cache_ckv, cache_kpe, c_prompt, c_sample, ada_w, ada_b, norm1_g, norm2_g,
           ffn_w1, ffn_w2, mla_w_in, mla_g_qa, mla_g_kva, mla_w_q_up, mla_w_uk, mla_w_uv, mla_g_qn,
           mla_g_qr, mla_g_kn, mla_g_kr, mla_w_o, cm_w_in, cm_g_v, cm_w_s, cm_b_s, cm_w_o):
    batch, seq, d = x_prompt.shape
    dec_batch, dec_seq, _ = x_sample.shape
    past = cache_ckv.shape[2]
    depth = ada_w.shape[0]
    n_p, n_s = batch * seq, dec_batch * dec_seq
    tm_p, tm_s, attn_tile, attn_heads, ff_chunk = 512, n_s, 512, 4, 1024
    tiles_per_seq = seq // tm_p

    pos_p = np.arange(seq)
    pos_s = past + np.arange(dec_seq)
    rc_p, rs_p = _rope_tables(jnp.asarray(pos_p))
    rc_s, rs_s = _rope_tables(jnp.asarray(np.tile(pos_s, dec_batch)))
    rope_spec_p = pl.BlockSpec((tm_p, LANES), lambda i: (i % tiles_per_seq, 0))
    rope_spec_s = pl.BlockSpec((tm_s, LANES), lambda i: (i, 0))

    n_c = batch + dec_batch
    c_all = jnp.concatenate([c_prompt, c_sample], axis=0)
    c_all = jnp.pad(c_all, ((0, -n_c % BF16_ROWS), (0, 0)))
    mod = _ada_modulation(c_all, ada_w, ada_b)[:, :n_c]
    mod_spec_p = pl.BlockSpec((None, 1, d), lambda i: (i // tiles_per_seq, 0, 0))
    mod_spec_s = pl.BlockSpec((tm_s, d), lambda i: (i, 0))

    def split_mods(layer):
        m6 = mod[layer].reshape(batch + dec_batch, 6, d)
        m_p = [m6[:batch, j].reshape(batch, 1, d) for j in range(6)]
        m_s = [jnp.repeat(m6[batch:, j], dec_seq, axis=0) for j in range(6)]
        return m_p, m_s

    x_p = x_prompt.reshape(n_p, d)
    x_s = x_sample.reshape(n_s, d)
    ckv_p_l, kpe_p_l, ckv_s_l, kpe_s_l, v_s_l = [], [], [], [], []
    for layer in range(depth):
        m_p, m_s = split_mods(layer)
        if layer % 2 == 0:
            a = layer // 2
            p = _prep_mla(a, norm1_g, norm2_g, ffn_w1, ffn_w2, layer, mla_w_in, mla_g_qa, mla_g_kva,
                          mla_w_q_up, mla_w_uk, mla_w_uv, mla_g_qn, mla_g_qr, mla_g_kn, mla_g_kr, mla_w_o)
            q_p, k_p, ckv_p, _, ckvt_p, kr_p = _mla_project(x_p, m_p, mod_spec_p, p, rc_p, rs_p, rope_spec_p, tm_p)
            q_s, k_s, ckv_s, ckvb_s, _, kr_s = _mla_project(x_s, m_s, mod_spec_s, p, rc_s, rs_s, rope_spec_s, tm_s)
            ol_p = _prompt_attention(q_p, k_p, ckvt_p, batch, seq, attn_tile, attn_heads)
            ol_s = _sample_attention(q_s, k_s, ckvb_s, cache_ckv, cache_kpe, a, p["w_uk"].T, p["g_kn"],
                                     dec_seq, attn_tile)
            x_p = _mla_tail(x_p, ol_p, m_p, mod_spec_p, p, tm_p, ff_chunk, True)
            x_s = _mla_tail(x_s, ol_s, m_s, mod_spec_s, p, tm_s, ff_chunk, False)
            ckv_p_l.append(ckv_p.reshape(batch, seq, -1))
            kpe_p_l.append(kr_p.reshape(batch, seq, -1))
            ckv_s_l.append(ckv_s.reshape(dec_batch, dec_seq, -1))
            kpe_s_l.append(kr_s.reshape(dec_batch, dec_seq, -1))
        else:
            b = layer // 2
            args = (b, norm1_g, norm2_g, ffn_w1, ffn_w2, layer, cm_w_in, cm_g_v, cm_w_s, cm_b_s, cm_w_o)
            pp, period_p, off_p = _prep_sgu(*args, np.arange(SGU_CHUNK))
            ps, period_s, off_s = _prep_sgu(*args, pos_s % SGU_CHUNK)
            (x_p,) = _sgu_layer(x_p, m_p, mod_spec_p, pp, tm_p, ff_chunk, period_p, off_p, False)
            x_s, v_s = _sgu_layer(x_s, m_s, mod_spec_s, ps, tm_s, ff_chunk, period_s, off_s, True)
            v_s_l.append(v_s.reshape(dec_batch, dec_seq, -1))

    return (x_p.reshape(batch, seq, d), x_s.reshape(dec_batch, dec_seq, d),
            jnp.stack(ckv_p_l), jnp.stack(kpe_p_l), jnp.stack(ckv_s_l), jnp.stack(kpe_s_l),
            jnp.stack(v_s_l))
```

```python
import functools

import numpy as np
import jax
import jax.numpy as jnp
from jax import lax
from jax.experimental import pallas as pl
from jax.experimental.pallas import tpu as pltpu

F32 = jnp.float32
BF16 = jnp.bfloat16

CHUNK = 64
N_HEADS = 8
QK_NOPE_DIM = 128
QK_ROPE_DIM = 64
V_HEAD_DIM = 128
Q_LORA_RANK = 384
KV_LORA_RANK = 256
ROPE_THETA = 10000.0
SGU_CHUNK = 128
SGU_GROUPS = 8
EPS = 1e-6

LANES = 128
BF16_ROWS = 16
MXU_DIM = 256
HEAD_PAD = 2 * LANES
Q_HEAD_COLS = 3 * LANES
VMEM_LIMIT = 56 * 1024 * 1024

NEG_INF = float(np.finfo(np.float32).min)
Q_SCALE = float((QK_NOPE_DIM + QK_ROPE_DIM) ** -0.5 * np.log2(np.e))


def _cparams(n_axes):
    return pltpu.CompilerParams(
        dimension_semantics=("arbitrary",) * n_axes,
        vmem_limit_bytes=VMEM_LIMIT,
    )


def _const_spec(shape):
    nd = len(shape)
    return pl.BlockSpec(shape, lambda *_: (0,) * nd, pipeline_mode=pl.Buffered(1))


def _layer_spec(stacked, layer):
    return pl.BlockSpec((None,) + stacked.shape[1:], lambda *_: (layer, 0, 0), pipeline_mode=pl.Buffered(1))


def _dot(a, b):
    return jnp.dot(a, b, preferred_element_type=F32)


def _dot_nt(a, b):
    return lax.dot_general(a, b, (((1,), (1,)), ((), ())), preferred_element_type=F32)


def _rms(x, n):
    ms = jnp.sum(x * x, axis=-1, keepdims=True) / n
    return x * lax.rsqrt(ms + EPS)


def _rope128(x, c, s):
    half = QK_ROPE_DIM // 2
    rot = pltpu.roll(x, LANES - half, axis=1) + pltpu.roll(x, half, axis=1)
    return x * c + rot * s


def _mod_norm(x, g, shift, scale):
    return _rms(x, x.shape[-1]) * g * (1.0 + scale) + shift


def _sq_relu_ffn(hb, w1_ref, w2_ref, ff_chunk):
    d_ff = w1_ref.shape[1]
    acc = None
    for c in range(d_ff // ff_chunk):
        a = _dot(hb, w1_ref[:, c * ff_chunk:(c + 1) * ff_chunk])
        a = jnp.maximum(a, 0.0)
        a = (a * a).astype(BF16)
        part = _dot(a, w2_ref[c * ff_chunk:(c + 1) * ff_chunk, :])
        acc = part if acc is None else acc + part
    return acc


def _ada_kernel(c_ref, w_ref, b_ref, o_ref):
    s = jax.nn.silu(c_ref[...]).astype(BF16)
    o_ref[...] = _dot(s, w_ref[...].astype(BF16)) + b_ref[...]


def _ada_modulation(c_all, ada_w, ada_b):
    depth, d, n6 = ada_w.shape
    rows = c_all.shape[0]
    tn = 1536
    return pl.pallas_call(
        _ada_kernel,
        grid=(depth, n6 // tn),
        in_specs=[
            pl.BlockSpec((rows, d), lambda l, j: (0, 0)),
            pl.BlockSpec((None, d, tn), lambda l, j: (l, 0, j)),
            pl.BlockSpec((None, 1, tn), lambda l, j: (l, 0, j)),
        ],
        out_specs=pl.BlockSpec((None, rows, tn), lambda l, j: (l, 0, j)),
        out_shape=jax.ShapeDtypeStruct((depth, rows, n6), F32),
        compiler_params=_cparams(2),
        name="ada_modulation",
    )(c_all, ada_w, ada_b.reshape(depth, 1, n6))


def _mla_proj_kernel(x_ref, g1_ref, sh_ref, sc_ref, w_in_ref, g_qa_ref, g_kva_ref, g_kr_ref,
                     w_qup_ref, g_qn_ref, g_qr_ref, w_uk_ref, g_kn_ref, rc_ref, rs_ref,
                     q_ref, k_ref, ckv_ref, ckvb_ref, ckvt_ref, kr_ref):
    x = x_ref[...]
    h = _mod_norm(x, g1_ref[...], sh_ref[...], sc_ref[...]).astype(BF16)
    a = _dot(h, w_in_ref[...])
    cq = _rms(a[:, :Q_LORA_RANK], Q_LORA_RANK) * g_qa_ref[...]
    c0 = Q_LORA_RANK
    ckv = _rms(a[:, c0:c0 + KV_LORA_RANK], KV_LORA_RANK) * g_kva_ref[...]
    c1 = c0 + KV_LORA_RANK
    rc = rc_ref[...]
    rs = rs_ref[...]
    kr = _rope128(_rms(a[:, c1:c1 + LANES], QK_ROPE_DIM) * g_kr_ref[...], rc, rs)
    ckv_ref[...] = ckv
    ckv_b = ckv.astype(BF16)
    ckvb_ref[...] = ckv_b
    ckvt_ref[...] = ckv.T.astype(BF16)
    kr_ref[...] = kr[:, :QK_ROPE_DIM]
    kr_b = kr.astype(BF16)

    q = _dot(cq.astype(BF16), w_qup_ref[...])
    kn_all = _dot(ckv_b, w_uk_ref[...])
    g_qn = g_qn_ref[...] * Q_SCALE
    rot_c = rc * (g_qr_ref[0:1, :] * Q_SCALE)
    rot_s = rs * (g_qr_ref[1:2, :] * Q_SCALE)
    g_kn = g_kn_ref[...]
    for hd in range(N_HEADS):
        o = hd * Q_HEAD_COLS
        qn = _rms(q[:, o:o + QK_NOPE_DIM], QK_NOPE_DIM) * g_qn
        x = q[:, o + QK_NOPE_DIM:o + 2 * LANES]
        x_sw = q[:, o + 2 * LANES:o + 3 * LANES]
        inv = lax.rsqrt(jnp.sum(x * x, axis=-1, keepdims=True) / QK_ROPE_DIM + EPS)
        qr = (x * rot_c + x_sw * rot_s) * inv
        q_ref[hd] = jnp.concatenate([qn.astype(BF16), qr.astype(BF16)], axis=-1)
        kn = _rms(kn_all[:, hd * QK_NOPE_DIM:(hd + 1) * QK_NOPE_DIM], QK_NOPE_DIM) * g_kn
        k_ref[hd] = jnp.concatenate([kn.astype(BF16), kr_b], axis=-1)


def _mla_project(x, mods, mod_spec, p, rope_c, rope_s, rope_spec, tm):
    n, d = x.shape
    row = lambda w: pl.BlockSpec((tm, w), lambda i: (i, 0))
    hrow = pl.BlockSpec((N_HEADS, tm, HEAD_PAD), lambda i: (0, i, 0))
    consts = [p["w_in"], p["g_qa"], p["g_kva"], p["g_kr"], p["w_q_up"], p["g_qn"], p["g_qr"],
              p["w_uk"], p["g_kn"]]
    return pl.pallas_call(
        _mla_proj_kernel,
        grid=(n // tm,),
        in_specs=[row(d), _const_spec(p["g1"].shape), mod_spec, mod_spec]
        + [_const_spec(c.shape) for c in consts] + [rope_spec, rope_spec],
        out_specs=[hrow, hrow, row(KV_LORA_RANK), row(KV_LORA_RANK),
                   pl.BlockSpec((None, KV_LORA_RANK, tm), lambda i: (i, 0, 0)), row(QK_ROPE_DIM)],
        out_shape=[
            jax.ShapeDtypeStruct((N_HEADS, n, HEAD_PAD), BF16),
            jax.ShapeDtypeStruct((N_HEADS, n, HEAD_PAD), BF16),
            jax.ShapeDtypeStruct((n, KV_LORA_RANK), F32),
            jax.ShapeDtypeStruct((n, KV_LORA_RANK), BF16),
            jax.ShapeDtypeStruct((n // tm, KV_LORA_RANK, tm), BF16),
            jax.ShapeDtypeStruct((n, QK_ROPE_DIM), F32),
        ],
        compiler_params=_cparams(1),
        name="mla_project",
    )(x, p["g1"], mods[0], mods[1], *consts, rope_c, rope_s)


def _prompt_attn_kernel(q_ref, k_ref, vt_ref, o_ref, m_ref, l_ref, acc_ref, st_ref, *, tile, heads):
    qi = pl.program_id(2)
    m_ref[...] = jnp.full(m_ref.shape, NEG_INF, F32)
    l_ref[...] = jnp.zeros(l_ref.shape, F32)
    acc_ref[...] = jnp.zeros(acc_ref.shape, F32)

    def scores(kt, slot):
        start = pl.multiple_of(kt * tile, tile)
        for g in range(heads):
            st_ref[slot, g] = _dot_nt(k_ref[g, pl.ds(start, tile), :], q_ref[g])

    def step(kt, slot, masked):
        vt = vt_ref[kt]
        for g in range(heads):
            for w in range(tile // MXU_DIM):
                nk = (w + 1) * MXU_DIM if masked else tile
                alphas, probs = [], []
                for j in range(w * MXU_DIM // LANES, (w + 1) * MXU_DIM // LANES):
                    cols = slice(j * LANES, (j + 1) * LANES)
                    col = st_ref[slot, g, :nk, cols]
                    if masked:
                        kc = lax.broadcasted_iota(jnp.int32, col.shape, 0) // CHUNK
                        qc = (lax.broadcasted_iota(jnp.int32, col.shape, 1) + j * LANES) // CHUNK
                        col = jnp.where(kc <= qc, col, NEG_INF)
                    m_prev = m_ref[g, :, cols]
                    m_new = jnp.maximum(m_prev, jnp.max(col, axis=0, keepdims=True))
                    alpha = jnp.exp2(m_prev - m_new)
                    pr = jnp.exp2(col - m_new)
                    l_ref[g, :, cols] = alpha * l_ref[g, :, cols] + jnp.sum(pr, axis=0, keepdims=True)
                    m_ref[g, :, cols] = m_new
                    probs.append(pr.astype(BF16))
                    alphas.append(alpha)
                wide = slice(w * MXU_DIM, (w + 1) * MXU_DIM)
                acc_ref[g, :, wide] = (acc_ref[g, :, wide] * jnp.concatenate(alphas, axis=-1)
                                       + _dot(vt[:, :nk], jnp.concatenate(probs, axis=-1)))

    def body(i, carry):
        kt = 2 * i
        scores(kt + 1, 1)
        step(kt, 0, False)
        scores(kt + 2, 0)
        step(kt + 1, 1, False)
        return carry

    scores(0, 0)
    lax.fori_loop(0, qi // 2, body, 0)

    @pl.when(qi % 2 == 0)
    def _():
        step(qi, 0, True)

    @pl.when(qi % 2 == 1)
    def _():
        scores(qi, 1)
        step(qi - 1, 0, False)
        step(qi, 1, True)

    for g in range(heads):
        o_ref[g * KV_LORA_RANK:(g + 1) * KV_LORA_RANK, :] = (acc_ref[g] / l_ref[g]).astype(o_ref.dtype)


def _prompt_attention(q_cat, k_cat, ckv_t, batch, seq, tile, heads):
    n = batch * seq
    nq = seq // tile
    assert ckv_t.shape == (n // tile, KV_LORA_RANK, tile), ckv_t.shape
    return pl.pallas_call(
        functools.partial(_prompt_attn_kernel, tile=tile, heads=heads),
        grid=(batch, N_HEADS // heads, nq),
        in_specs=[
            pl.BlockSpec((heads, tile, HEAD_PAD), lambda b, h, i: (h, b * nq + i, 0)),
            pl.BlockSpec((heads, seq, HEAD_PAD), lambda b, h, i: (h, b, 0)),
            pl.BlockSpec((nq, KV_LORA_RANK, tile), lambda b, h, i: (b, 0, 0)),
        ],
        out_specs=pl.BlockSpec((heads * KV_LORA_RANK, tile), lambda b, h, i: (h, b * nq + i)),
        out_shape=jax.ShapeDtypeStruct((N_HEADS * KV_LORA_RANK, n), BF16),
        scratch_shapes=[
            pltpu.VMEM((heads, 1, tile), F32),
            pltpu.VMEM((heads, 1, tile), F32),
            pltpu.VMEM((heads, KV_LORA_RANK, tile), F32),
            pltpu.VMEM((2, heads, tile, tile), F32),
        ],
        compiler_params=_cparams(3),
        name="prompt_attention",
    )(q_cat, k_cat, ckv_t)


def _sample_attn_kernel(q_ref, kn_ref, ckvn_ref, cckv_ref, ckpe_ref, w_uk_ref, g_kn_ref, o_ref,
                        m_ref, l_ref, acc_ref, *, tile, past, n_new):
    m_ref[...] = jnp.full(m_ref.shape, NEG_INF, F32)
    l_ref[...] = jnp.zeros(l_ref.shape, F32)
    acc_ref[...] = jnp.zeros(acc_ref.shape, F32)
    g_kn = g_kn_ref[...]
    qr_all = jnp.concatenate([q_ref[hd][:, QK_NOPE_DIM:] for hd in range(N_HEADS)], axis=0)
    zero = jnp.zeros((n_new, QK_NOPE_DIM), BF16)
    q_bd = jnp.concatenate([
        jnp.concatenate([(q_ref[hd][:, :QK_NOPE_DIM] * g_kn).astype(BF16) if c == hd else zero
                         for c in range(N_HEADS)], axis=1)
        for hd in range(N_HEADS)], axis=0)

    def update(s, v):
        m_prev = m_ref[...]
        m_new = jnp.maximum(m_prev, jnp.max(s, axis=-1, keepdims=True))
        alpha = jnp.exp2(m_prev - m_new)
        pr = jnp.exp2(s - m_new)
        l_ref[...] = alpha * l_ref[...] + jnp.sum(pr, axis=-1, keepdims=True)
        acc_ref[...] = alpha * acc_ref[...] + _dot(pr.astype(BF16), v)
        m_ref[...] = m_new

    def up_project(kt):
        v = cckv_ref[kt * tile:(kt + 1) * tile, :].astype(BF16)
        return v, _dot_nt(w_uk_ref[...], v)

    def attend(kt, v, kn_t):
        kpe_t = ckpe_ref[:, kt * tile:(kt + 1) * tile].astype(BF16)
        kpe_t = jnp.concatenate([kpe_t, jnp.zeros_like(kpe_t)], axis=0)
        s_rope = _dot(qr_all, kpe_t)
        inv = []
        for hd in range(N_HEADS):
            blk = kn_t[hd * QK_NOPE_DIM:(hd + 1) * QK_NOPE_DIM, :]
            ms = jnp.sum(blk * blk, axis=0, keepdims=True) / QK_NOPE_DIM
            inv.append(jnp.broadcast_to(lax.rsqrt(ms + EPS), (n_new, tile)))
        s = _dot(q_bd, kn_t.astype(BF16)) * jnp.concatenate(inv, axis=0) + s_rope
        update(s, v)

    n_tiles = past // tile
    nxt = up_project(0)
    for kt in range(n_tiles):
        cur = nxt
        if kt + 1 < n_tiles:
            nxt = up_project(kt + 1)
        attend(kt, *cur)

    s_new = jnp.concatenate([_dot_nt(q_ref[hd], kn_ref[hd]) for hd in range(N_HEADS)], axis=0)
    q_pos = past + lax.broadcasted_iota(jnp.int32, s_new.shape, 0) % n_new
    k_pos = past + lax.broadcasted_iota(jnp.int32, s_new.shape, 1)
    s_new = jnp.where(k_pos // CHUNK <= q_pos // CHUNK, s_new, NEG_INF)
    update(s_new, ckvn_ref[...])

    o = (acc_ref[...] / l_ref[...]).astype(o_ref.dtype)
    for hd in range(N_HEADS):
        o_ref[:, hd * KV_LORA_RANK:(hd + 1) * KV_LORA_RANK] = o[hd * n_new:(hd + 1) * n_new, :]


def _sample_attention(q_cat, k_cat, ckv_b, cache_ckv, cache_kpe, a, w_uk, g_kn, n_new, tile):
    _, dec_batch, past, _ = cache_ckv.shape
    n = dec_batch * n_new
    rows = N_HEADS * n_new
    hrow = pl.BlockSpec((N_HEADS, n_new, HEAD_PAD), lambda b: (0, b, 0))
    cache_kpe_t = jnp.swapaxes(cache_kpe, 2, 3)
    return pl.pallas_call(
        functools.partial(_sample_attn_kernel, tile=tile, past=past, n_new=n_new),
        grid=(dec_batch,),
        in_specs=[
            hrow, hrow,
            pl.BlockSpec((n_new, KV_LORA_RANK), lambda b: (b, 0)),
            pl.BlockSpec((None, None, past, KV_LORA_RANK), lambda b: (a, b, 0, 0)),
            pl.BlockSpec((None, None, QK_ROPE_DIM, past), lambda b: (a, b, 0, 0)),
            _const_spec(w_uk.shape), _const_spec(g_kn.shape),
        ],
        out_specs=pl.BlockSpec((n_new, N_HEADS * KV_LORA_RANK), lambda b: (b, 0)),
        out_shape=jax.ShapeDtypeStruct((n, N_HEADS * KV_LORA_RANK), BF16),
        scratch_shapes=[
            pltpu.VMEM((rows, 1), F32),
            pltpu.VMEM((rows, 1), F32),
            pltpu.VMEM((rows, KV_LORA_RANK), F32),
        ],
        compiler_params=_cparams(1),
        name="sample_attention",
    )(q_cat, k_cat, ckv_b, cache_ckv, cache_kpe_t, w_uk, g_kn)


def _ffn_tail(x1, g2_ref, sh2_ref, sc2_ref, gt2_ref, w1_ref, w2_ref, ff_chunk):
    h2 = _mod_norm(x1, g2_ref[...], sh2_ref[...], sc2_ref[...]).astype(BF16)
    return x1 + gt2_ref[...] * _sq_relu_ffn(h2, w1_ref, w2_ref, ff_chunk)


def _mla_tail_kernel(x_ref, ol_ref, w_uv_ref, w_o_ref, gt1_ref, g2_ref, sh2_ref, sc2_ref, gt2_ref,
                     w1_ref, w2_ref, o_ref, *, ff_chunk, latent_major):
    heads = []
    for hd in range(N_HEADS):
        lat = slice(hd * KV_LORA_RANK, (hd + 1) * KV_LORA_RANK)
        if latent_major:
            up = lax.dot_general(ol_ref[lat, :], w_uv_ref[hd], (((0,), (0,)), ((), ())),
                                 preferred_element_type=F32)
        else:
            up = _dot(ol_ref[:, lat], w_uv_ref[hd])
        heads.append(up.astype(BF16))
    m = _dot(jnp.concatenate(heads, axis=-1), w_o_ref[...])
    x1 = x_ref[...] + gt1_ref[...] * m
    o_ref[...] = _ffn_tail(x1, g2_ref, sh2_ref, sc2_ref, gt2_ref, w1_ref, w2_ref, ff_chunk)


def _mla_tail(x, o_lat, mods, mod_spec, p, tm, ff_chunk, latent_major):
    n, d = x.shape
    row = lambda w: pl.BlockSpec((tm, w), lambda i: (i, 0))
    consts_a = [p["w_uv"], p["w_o"]]
    consts_b = [p["w1"], p["w2"]]
    if latent_major:
        ol_spec = pl.BlockSpec((o_lat.shape[0], tm), lambda i: (0, i))
    else:
        ol_spec = row(o_lat.shape[1])
    return pl.pallas_call(
        functools.partial(_mla_tail_kernel, ff_chunk=ff_chunk, latent_major=latent_major),
        grid=(n // tm,),
        in_specs=[row(d), ol_spec] + [_const_spec(c.shape) for c in consts_a]
        + [mod_spec, _const_spec(p["g2"].shape), mod_spec, mod_spec, mod_spec]
        + [_layer_spec(c, p["layer"]) for c in consts_b],
        out_specs=row(d),
        out_shape=jax.ShapeDtypeStruct((n, d), F32),
        compiler_params=_cparams(1),
        name="mla_tail_ffn",
    )(x, o_lat, *consts_a, mods[2], p["g2"], mods[3], mods[4], mods[5], *consts_b)


def _sgu_layer_kernel(x_ref, g1_ref, sh1_ref, sc1_ref, gt1_ref, w_in_ref, g_v_ref, w_s_ref, b_s_ref,
                      w_o_ref, g2_ref, sh2_ref, sc2_ref, gt2_ref, w1_ref, w2_ref, *out_refs,
                      ff_chunk, period, offset, emit_v):
    x = x_ref[...]
    tm = x.shape[0]
    width = w_o_ref.shape[0]
    gdim = width // SGU_GROUPS
    h = _mod_norm(x, g1_ref[...], sh1_ref[...], sc1_ref[...]).astype(BF16)
    z = jax.nn.gelu(_dot(h, w_in_ref[...]))
    u = z[:, :width]
    v = _rms(z[:, width:], width) * g_v_ref[...]
    if emit_v:
        out_refs[1][...] = v
    vb = v.astype(BF16)

    ri = lax.broadcasted_iota(jnp.int32, (SGU_CHUNK, SGU_CHUNK), 0)
    ci = lax.broadcasted_iota(jnp.int32, (SGU_CHUNK, SGU_CHUNK), 1)
    vis = (ri // period == ci // period) & ((ci % period + offset) // CHUNK <= (ri % period + offset) // CHUNK)
    w_mix = [jnp.where(vis, w_s_ref[g], 0.0).astype(BF16) for g in range(SGU_GROUPS)]

    rows = []
    for c in range(tm // SGU_CHUNK):
        r0 = c * SGU_CHUNK
        cols = []
        for g in range(SGU_GROUPS):
            l0 = g * gdim
            mixed = _dot(w_mix[g], vb[r0:r0 + SGU_CHUNK, l0:l0 + gdim]) + b_s_ref[g]
            cols.append((u[r0:r0 + SGU_CHUNK, l0:l0 + gdim] * mixed).astype(BF16))
        rows.append(jnp.concatenate(cols, axis=-1))
    gated = jnp.concatenate(rows, axis=0)
    x1 = x + gt1_ref[...] * _dot(gated, w_o_ref[...])
    out_refs[0][...] = _ffn_tail(x1, g2_ref, sh2_ref, sc2_ref, gt2_ref, w1_ref, w2_ref, ff_chunk)


def _sgu_layer(x, mods, mod_spec, p, tm, ff_chunk, period, offset, emit_v):
    n, d = x.shape
    width = p["w_o"].shape[0]
    row = lambda w: pl.BlockSpec((tm, w), lambda i: (i, 0))
    cs = lambda a: _const_spec(a.shape)
    out_specs = [row(d)]
    out_shape = [jax.ShapeDtypeStruct((n, d), F32)]
    if emit_v:
        out_specs.append(row(width))
        out_shape.append(jax.ShapeDtypeStruct((n, width), F32))
    return pl.pallas_call(
        functools.partial(_sgu_layer_kernel, ff_chunk=ff_chunk, period=period, offset=offset, emit_v=emit_v),
        grid=(n // tm,),
        in_specs=[row(d), cs(p["g1"]), mod_spec, mod_spec, mod_spec, cs(p["w_in"]), cs(p["g_v"]),
                  cs(p["w_s"]), cs(p["b_s"]), cs(p["w_o"]), cs(p["g2"]), mod_spec, mod_spec, mod_spec,
                  _layer_spec(p["w1"], p["layer"]), _layer_spec(p["w2"], p["layer"])],
        out_specs=out_specs,
        out_shape=out_shape,
        compiler_params=_cparams(1),
        name="sgu_layer_ffn",
    )(x, p["g1"], mods[0], mods[1], mods[2], p["w_in"], p["g_v"], p["w_s"], p["b_s"], p["w_o"],
      p["g2"], mods[3], mods[4], mods[5], p["w1"], p["w2"])


def _rope_tables(pos):
    half = QK_ROPE_DIM // 2
    inv = 1.0 / (ROPE_THETA ** (jnp.arange(half, dtype=F32) / half))
    ang = pos.astype(F32)[:, None] * inv[None, :]
    cos, sin = jnp.cos(ang), jnp.sin(ang)
    z = jnp.zeros((pos.shape[0], LANES - QK_ROPE_DIM), F32)
    return jnp.concatenate([cos, cos, z], axis=-1), jnp.concatenate([-sin, sin, z], axis=-1)


def _pad_lanes(g, n):
    return jnp.pad(g, (0, n - g.shape[0])).reshape(1, n)


def _prep_mla(a, norm1_g, norm2_g, ffn_w1, ffn_w2, layer, mla_w_in, mla_g_qa, mla_g_kva, mla_w_q_up,
              mla_w_uk, mla_w_uv, mla_g_qn, mla_g_qr, mla_g_kn, mla_g_kr, mla_w_o):
    d = mla_w_in.shape[1]
    w_in = jnp.pad(mla_w_in[a], ((0, 0), (0, LANES - QK_ROPE_DIM))).astype(BF16)
    half = QK_ROPE_DIM // 2
    w_q_up = mla_w_q_up[a].reshape(Q_LORA_RANK, N_HEADS, QK_NOPE_DIM + QK_ROPE_DIM)
    w_r1 = w_q_up[:, :, QK_NOPE_DIM:QK_NOPE_DIM + half]
    w_r2 = w_q_up[:, :, QK_NOPE_DIM + half:]
    w_z = jnp.zeros((Q_LORA_RANK, N_HEADS, LANES - QK_ROPE_DIM), w_q_up.dtype)
    w_q_up = jnp.concatenate([w_q_up[:, :, :QK_NOPE_DIM], w_r1, w_r2, w_z, w_r2, w_r1, w_z], axis=-1)
    g_qr = mla_g_qr[a]
    g_qr = jnp.concatenate([_pad_lanes(g_qr, LANES),
                            _pad_lanes(jnp.concatenate([g_qr[half:], g_qr[:half]]), LANES)], axis=0)
    return {
        "g1": norm1_g[layer].reshape(1, d), "g2": norm2_g[layer].reshape(1, d),
        "w1": ffn_w1, "w2": ffn_w2, "layer": layer,
        "w_in": w_in,
        "g_qa": mla_g_qa[a].reshape(1, -1), "g_kva": mla_g_kva[a].reshape(1, -1),
        "g_kr": _pad_lanes(mla_g_kr[a], LANES),
        "w_q_up": w_q_up.reshape(Q_LORA_RANK, N_HEADS * Q_HEAD_COLS).astype(BF16),
        "g_qn": mla_g_qn[a].reshape(1, -1), "g_qr": g_qr,
        "w_uk": mla_w_uk[a].reshape(KV_LORA_RANK, N_HEADS * QK_NOPE_DIM).astype(BF16),
        "g_kn": mla_g_kn[a].reshape(1, -1),
        "w_uv": jnp.swapaxes(mla_w_uv[a], 0, 1).astype(BF16),
        "w_o": mla_w_o[a].astype(BF16),
    }


def _prep_sgu(b, norm1_g, norm2_g, ffn_w1, ffn_w2, layer, cm_w_in, cm_g_v, cm_w_s, cm_b_s, cm_w_o, idx):
    d = cm_w_in.shape[1]
    period = len(idx)
    lo = int(idx[0])
    assert SGU_CHUNK % period == 0 and np.array_equal(idx, lo + np.arange(period)), idx
    reps = SGU_CHUNK // period
    w_s = jnp.tile(cm_w_s[b][:, lo:lo + period, lo:lo + period], (1, reps, reps))
    b_s = jnp.tile(cm_b_s[b][:, lo:lo + period], (1, reps))
    gdim = cm_w_o.shape[1] // SGU_GROUPS
    return {
        "g1": norm1_g[layer].reshape(1, d), "g2": norm2_g[layer].reshape(1, d),
        "w1": ffn_w1, "w2": ffn_w2, "layer": layer,
        "w_in": cm_w_in[b].astype(BF16), "g_v": cm_g_v[b].reshape(1, -1),
        "w_s": w_s, "b_s": jnp.broadcast_to(b_s[:, :, None], (SGU_GROUPS, SGU_CHUNK, gdim)),
        "w_o": cm_w_o[b].astype(BF16),
    }, period, lo


def kernel(x_prompt, x_sample, cache_ckv, cache_kpe, c_prompt, c_sample, ada_w, ada_b, norm1_g, norm2_g,
           ffn_w1, ffn_w2, mla_w_in, mla_g_qa, mla_g_kva, mla_w_q_up, mla_w_uk, mla_w_uv, mla_g_qn,
           mla_g_qr, mla_g_kn, mla_g_kr, mla_w_o, cm_w_in, cm_g_v, cm_w_s, cm_b_s, cm_w_o):
    batch, seq, d = x_prompt.shape
    dec_batch, dec_seq, _ = x_sample.shape
    past = cache_ckv.shape[2]
    depth = ada_w.shape[0]
    n_p, n_s = batch * seq, dec_batch * dec_seq
    tm_p, tm_s, attn_tile, attn_heads, ff_chunk = 512, n_s, 512, 4, 1024
    tiles_per_seq = seq // tm_p

    pos_p = np.arange(seq)
    pos_s = past + np.arange(dec_seq)
    rc_p, rs_p = _rope_tables(jnp.asarray(pos_p))
    rc_s, rs_s = _rope_tables(jnp.asarray(np.tile(pos_s, dec_batch)))
    rope_spec_p = pl.BlockSpec((tm_p, LANES), lambda i: (i % tiles_per_seq, 0))
    rope_spec_s = pl.BlockSpec((tm_s, LANES), lambda i: (i, 0))

    n_c = batch + dec_batch
    c_all = jnp.concatenate([c_prompt, c_sample], axis=0)
    c_all = jnp.pad(c_all, ((0, -n_c % BF16_ROWS), (0, 0)))
    mod = _ada_modulation(c_all, ada_w, ada_b)[:, :n_c]
    mod_spec_p = pl.BlockSpec((None, 1, d), lambda i: (i // tiles_per_seq, 0, 0))
    mod_spec_s = pl.BlockSpec((tm_s, d), lambda i: (i, 0))

    def split_mods(layer):
        m6 = mod[layer].reshape(batch + dec_batch, 6, d)
        m_p = [m6[:batch, j].reshape(batch, 1, d) for j in range(6)]
        m_s = [jnp.repeat(m6[batch:, j], dec_seq, axis=0) for j in range(6)]
        return m_p, m_s

    ffn_w1 = ffn_w1.astype(BF16)
    ffn_w2 = ffn_w2.astype(BF16)
    x_p = x_prompt.reshape(n_p, d)
    x_s = x_sample.reshape(n_s, d)
    ckv_p_l, kpe_p_l, ckv_s_l, kpe_s_l, v_s_l = [], [], [], [], []
    for layer in range(depth):
        m_p, m_s = split_mods(layer)
        if layer % 2 == 0:
            a = layer // 2
            p = _prep_mla(a, norm1_g, norm2_g, ffn_w1, ffn_w2, layer, mla_w_in, mla_g_qa, mla_g_kva,
                          mla_w_q_up, mla_w_uk, mla_w_uv, mla_g_qn, mla_g_qr, mla_g_kn, mla_g_kr, mla_w_o)
            q_p, k_p, ckv_p, _, ckvt_p, kr_p = _mla_project(x_p, m_p, mod_spec_p, p, rc_p, rs_p, rope_spec_p, tm_p)
            q_s, k_s, ckv_s, ckvb_s, _, kr_s = _mla_project(x_s, m_s, mod_spec_s, p, rc_s, rs_s, rope_spec_s, tm_s)
            ol_p = _prompt_attention(q_p, k_p, ckvt_p, batch, seq, attn_tile, attn_heads)
            ol_s = _sample_attention(q_s, k_s, ckvb_s, cache_ckv, cache_kpe, a, p["w_uk"].T, p["g_kn"],
                                     dec_seq, attn_tile)
            x_p = _mla_tail(x_p, ol_p, m_p, mod_spec_p, p, tm_p, ff_chunk, True)
            x_s = _mla_tail(x_s, ol_s, m_s, mod_spec_s, p, tm_s, ff_chunk, False)
            ckv_p_l.append(ckv_p.reshape(batch, seq, -1))
            kpe_p_l.append(kr_p.reshape(batch, seq, -1))
            ckv_s_l.append(ckv_s.reshape(dec_batch, dec_seq, -1))
            kpe_s_l.append(kr_s.reshape(dec_batch, dec_seq, -1))
        else:
            b = layer // 2
            args = (b, norm1_g, norm2_g, ffn_w1, ffn_w2, layer, cm_w_in, cm_g_v, cm_w_s, cm_b_s, cm_w_o)
            pp, period_p, off_p = _prep_sgu(*args, np.arange(SGU_CHUNK))
            ps, period_s, off_s = _prep_sgu(*args, pos_s % SGU_CHUNK)
            (x_p,) = _sgu_layer(x_p, m_p, mod_spec_p, pp, tm_p, ff_chunk, period_p, off_p, False)
            x_s, v_s = _sgu_layer(x_s, m_s, mod_spec_s, ps, tm_s, ff_chunk, period_s, off_s, True)
            v_s_l.append(v_s.reshape(dec_batch, dec_seq, -1))

    return (x_p.reshape(batch, seq, d), x_s.reshape(dec_batch, dec_seq, d),
            jnp.stack(ckv_p_l), jnp.stack(kpe_p_l), jnp.stack(ckv_s_l), jnp.stack(kpe_s_l),
            jnp.stack(v_s_l))
```

```python
import functools

import numpy as np
import jax
import jax.numpy as jnp
from jax import lax
from jax.experimental import pallas as pl
from jax.experimental.pallas import tpu as pltpu

F32 = jnp.float32
BF16 = jnp.bfloat16

CHUNK = 64
N_HEADS = 8
QK_NOPE_DIM = 128
QK_ROPE_DIM = 64
V_HEAD_DIM = 128
Q_LORA_RANK = 384
KV_LORA_RANK = 256
ROPE_THETA = 10000.0
SGU_CHUNK = 128
SGU_GROUPS = 8
EPS = 1e-6

LANES = 128
BF16_ROWS = 16
MXU_DIM = 256
HEAD_PAD = 2 * LANES
Q_HEAD_COLS = 3 * LANES
VMEM_LIMIT = 56 * 1024 * 1024

NEG_INF = float(np.finfo(np.float32).min)
Q_SCALE = float((QK_NOPE_DIM + QK_ROPE_DIM) ** -0.5 * np.log2(np.e))


def _cparams(n_axes):
    return pltpu.CompilerParams(
        dimension_semantics=("arbitrary",) * n_axes,
        vmem_limit_bytes=VMEM_LIMIT,
    )


def _const_spec(shape):
    nd = len(shape)
    return pl.BlockSpec(shape, lambda *_: (0,) * nd, pipeline_mode=pl.Buffered(1))


def _layer_spec(stacked, layer):
    return pl.BlockSpec((None,) + stacked.shape[1:], lambda *_: (layer, 0, 0), pipeline_mode=pl.Buffered(1))


def _dot(a, b):
    return jnp.dot(a, b, preferred_element_type=F32)


def _dot_nt(a, b):
    return lax.dot_general(a, b, (((1,), (1,)), ((), ())), preferred_element_type=F32)


def _rms(x, n):
    ms = jnp.sum(x * x, axis=-1, keepdims=True) / n
    return x * lax.rsqrt(ms + EPS)


def _rope128(x, c, s):
    half = QK_ROPE_DIM // 2
    rot = pltpu.roll(x, LANES - half, axis=1) + pltpu.roll(x, half, axis=1)
    return x * c + rot * s


def _mod_norm(x, g, shift, scale):
    return _rms(x, x.shape[-1]) * g * (1.0 + scale) + shift


def _sq_relu_ffn(hb, w1_ref, w2_ref, ff_chunk):
    d_ff = w1_ref.shape[1]
    acc = None
    for c in range(d_ff // ff_chunk):
        a = _dot(hb, w1_ref[:, c * ff_chunk:(c + 1) * ff_chunk])
        a = jnp.maximum(a, 0.0)
        a = (a * a).astype(BF16)
        part = _dot(a, w2_ref[c * ff_chunk:(c + 1) * ff_chunk, :])
        acc = part if acc is None else acc + part
    return acc


def _ada_kernel(c_ref, w_ref, b_ref, o_ref):
    s = jax.nn.silu(c_ref[...]).astype(BF16)
    o_ref[...] = _dot(s, w_ref[...].astype(BF16)) + b_ref[...]


def _ada_modulation(c_all, ada_w, ada_b):
    depth, d, n6 = ada_w.shape
    rows = c_all.shape[0]
    tn = 1536
    return pl.pallas_call(
        _ada_kernel,
        grid=(depth, n6 // tn),
        in_specs=[
            pl.BlockSpec((rows, d), lambda l, j: (0, 0)),
            pl.BlockSpec((None, d, tn), lambda l, j: (l, 0, j)),
            pl.BlockSpec((None, 1, tn), lambda l, j: (l, 0, j)),
        ],
        out_specs=pl.BlockSpec((None, rows, tn), lambda l, j: (l, 0, j)),
        out_shape=jax.ShapeDtypeStruct((depth, rows, n6), F32),
        compiler_params=_cparams(2),
        name="ada_modulation",
    )(c_all, ada_w, ada_b.reshape(depth, 1, n6))


def _mla_proj_kernel(x_ref, g1_ref, sh_ref, sc_ref, w_in_ref, g_qa_ref, g_kva_ref, g_kr_ref,
                     w_qup_ref, g_qn_ref, g_qr_ref, w_uk_ref, g_kn_ref, rc_ref, rs_ref,
                     q_ref, k_ref, ckv_ref, ckvb_ref, ckvt_ref, kr_ref):
    x = x_ref[...]
    h = _mod_norm(x, g1_ref[...], sh_ref[...], sc_ref[...]).astype(BF16)
    a = _dot(h, w_in_ref[...])
    cq = _rms(a[:, :Q_LORA_RANK], Q_LORA_RANK) * g_qa_ref[...]
    c0 = Q_LORA_RANK
    ckv = _rms(a[:, c0:c0 + KV_LORA_RANK], KV_LORA_RANK) * g_kva_ref[...]
    c1 = c0 + KV_LORA_RANK
    rc = rc_ref[...]
    rs = rs_ref[...]
    kr = _rope128(_rms(a[:, c1:c1 + LANES], QK_ROPE_DIM) * g_kr_ref[...], rc, rs)
    ckv_ref[...] = ckv
    ckv_b = ckv.astype(BF16)
    ckvb_ref[...] = ckv_b
    vt_tile = ckvt_ref.shape[-1]
    for j in range(ckvt_ref.shape[0]):
        ckvt_ref[j] = ckv[j * vt_tile:(j + 1) * vt_tile, :].T.astype(BF16)
    kr_ref[...] = kr[:, :QK_ROPE_DIM]
    kr_b = kr.astype(BF16)

    q = _dot(cq.astype(BF16), w_qup_ref[...])
    kn_all = _dot(ckv_b, w_uk_ref[...])
    g_qn = g_qn_ref[...] * Q_SCALE
    rot_c = rc * (g_qr_ref[0:1, :] * Q_SCALE)
    rot_s = rs * (g_qr_ref[1:2, :] * Q_SCALE)
    g_kn = g_kn_ref[...]
    for hd in range(N_HEADS):
        o = hd * Q_HEAD_COLS
        qn = _rms(q[:, o:o + QK_NOPE_DIM], QK_NOPE_DIM) * g_qn
        x = q[:, o + QK_NOPE_DIM:o + 2 * LANES]
        x_sw = q[:, o + 2 * LANES:o + 3 * LANES]
        inv = lax.rsqrt(jnp.sum(x * x, axis=-1, keepdims=True) / QK_ROPE_DIM + EPS)
        qr = (x * rot_c + x_sw * rot_s) * inv
        q_ref[hd] = jnp.concatenate([qn.astype(BF16), qr.astype(BF16)], axis=-1)
        kn = _rms(kn_all[:, hd * QK_NOPE_DIM:(hd + 1) * QK_NOPE_DIM], QK_NOPE_DIM) * g_kn
        k_ref[hd] = jnp.concatenate([kn.astype(BF16), kr_b], axis=-1)


def _mla_project(x, mods, mod_spec, p, rope_c, rope_s, rope_spec, tm, vt_tile):
    n, d = x.shape
    row = lambda w: pl.BlockSpec((tm, w), lambda i: (i, 0))
    hrow = pl.BlockSpec((N_HEADS, tm, HEAD_PAD), lambda i: (0, i, 0))
    consts = [p["w_in"], p["g_qa"], p["g_kva"], p["g_kr"], p["w_q_up"], p["g_qn"], p["g_qr"],
              p["w_uk"], p["g_kn"]]
    return pl.pallas_call(
        _mla_proj_kernel,
        grid=(n // tm,),
        in_specs=[row(d), _const_spec(p["g1"].shape), mod_spec, mod_spec]
        + [_const_spec(c.shape) for c in consts] + [rope_spec, rope_spec],
        out_specs=[hrow, hrow, row(KV_LORA_RANK), row(KV_LORA_RANK),
                   pl.BlockSpec((tm // vt_tile, KV_LORA_RANK, vt_tile), lambda i: (i, 0, 0)), row(QK_ROPE_DIM)],
        out_shape=[
            jax.ShapeDtypeStruct((N_HEADS, n, HEAD_PAD), BF16),
            jax.ShapeDtypeStruct((N_HEADS, n, HEAD_PAD), BF16),
            jax.ShapeDtypeStruct((n, KV_LORA_RANK), F32),
            jax.ShapeDtypeStruct((n, KV_LORA_RANK), BF16),
            jax.ShapeDtypeStruct((n // vt_tile, KV_LORA_RANK, vt_tile), BF16),
            jax.ShapeDtypeStruct((n, QK_ROPE_DIM), F32),
        ],
        compiler_params=_cparams(1),
        name="mla_project",
    )(x, p["g1"], mods[0], mods[1], *consts, rope_c, rope_s)


def _prompt_attn_kernel(q_ref, k_ref, vt_ref, o_ref, m_ref, l_ref, acc_ref, st_ref, *, tile, heads, nq):
    first = 2

    def reset():
        m_ref[...] = jnp.full(m_ref.shape, NEG_INF, F32)
        l_ref[...] = jnp.zeros(l_ref.shape, F32)
        acc_ref[...] = jnp.zeros(acc_ref.shape, F32)

    def scores(qt, kt, slot):
        q0 = qt * tile if isinstance(qt, int) else pl.multiple_of(qt * tile, tile)
        k0 = kt * tile if isinstance(kt, int) else pl.multiple_of(kt * tile, tile)
        for g in range(heads):
            st_ref[slot, g] = _dot_nt(k_ref[g, pl.ds(k0, tile), :], q_ref[g, pl.ds(q0, tile), :])

    def finish(qt):
        for g in range(heads):
            o_ref[qt, g * KV_LORA_RANK:(g + 1) * KV_LORA_RANK, :] = (acc_ref[g] / l_ref[g]).astype(o_ref.dtype)
        reset()

    def step(kt, slot, masked):
        vt = vt_ref[kt]
        for g in range(heads):
            for w in range(tile // MXU_DIM):
                nk = (w + 1) * MXU_DIM if masked else tile
                alphas, probs = [], []
                for j in range(w * MXU_DIM // LANES, (w + 1) * MXU_DIM // LANES):
                    cols = slice(j * LANES, (j + 1) * LANES)
                    col = st_ref[slot, g, :nk, cols]
                    if masked:
                        kc = lax.broadcasted_iota(jnp.int32, col.shape, 0) // CHUNK
                        qc = (lax.broadcasted_iota(jnp.int32, col.shape, 1) + j * LANES) // CHUNK
                        col = jnp.where(kc <= qc, col, NEG_INF)
                    m_prev = m_ref[g, :, cols]
                    m_new = jnp.maximum(m_prev, jnp.max(col, axis=0, keepdims=True))
                    alpha = jnp.exp2(m_prev - m_new)
                    pr = jnp.exp2(col - m_new)
                    l_ref[g, :, cols] = alpha * l_ref[g, :, cols] + jnp.sum(pr, axis=0, keepdims=True)
                    m_ref[g, :, cols] = m_new
                    probs.append(pr.astype(BF16))
                    alphas.append(alpha)
                wide = slice(w * MXU_DIM, (w + 1) * MXU_DIM)
                acc_ref[g, :, wide] = (acc_ref[g, :, wide] * jnp.concatenate(alphas, axis=-1)
                                       + _dot(vt[:, :nk], jnp.concatenate(probs, axis=-1)))

    reset()
    scores(0, 0, 1)
    scores(min(1, nq - 1), 0, first)
    step(0, 1, True)
    finish(0)

    def query_tile(qt, carry):
        nxt = jnp.minimum(qt + 1, nq - 1)
        scores(qt, 1, 0)
        step(0, first, False)

        def pair(i, c):
            kt = 1 + 2 * i
            scores(qt, kt + 1, 1)
            step(kt, 0, False)
            scores(qt, kt + 2, 0)
            step(kt + 1, 1, False)
            return c

        lax.fori_loop(0, (qt - 1) // 2, pair, 0)

        @pl.when((qt - 1) % 2 == 0)
        def _():
            scores(nxt, 0, first)
            step(qt, 0, True)

        @pl.when((qt - 1) % 2 == 1)
        def _():
            scores(qt, qt, 1)
            step(qt - 1, 0, False)
            scores(nxt, 0, first)
            step(qt, 1, True)

        finish(qt)
        return carry

    lax.fori_loop(1, nq, query_tile, 0)


def _prompt_attention(q_cat, k_cat, ckv_t, batch, seq, tile, heads):
    n = batch * seq
    nq = seq // tile
    assert ckv_t.shape == (n // tile, KV_LORA_RANK, tile), ckv_t.shape
    return pl.pallas_call(
        functools.partial(_prompt_attn_kernel, tile=tile, heads=heads, nq=nq),
        grid=(batch, N_HEADS // heads),
        in_specs=[
            pl.BlockSpec((heads, seq, HEAD_PAD), lambda b, h: (h, b, 0)),
            pl.BlockSpec((heads, seq, HEAD_PAD), lambda b, h: (h, b, 0)),
            pl.BlockSpec((nq, KV_LORA_RANK, tile), lambda b, h: (b, 0, 0)),
        ],
        out_specs=pl.BlockSpec((None, nq, heads * KV_LORA_RANK, tile), lambda b, h: (b, 0, h, 0)),
        out_shape=jax.ShapeDtypeStruct((batch, nq, N_HEADS * KV_LORA_RANK, tile), BF16),
        scratch_shapes=[
            pltpu.VMEM((heads, 1, tile), F32),
            pltpu.VMEM((heads, 1, tile), F32),
            pltpu.VMEM((heads, KV_LORA_RANK, tile), F32),
            pltpu.VMEM((3, heads, tile, tile), F32),
        ],
        compiler_params=_cparams(2),
        name="prompt_attention",
    )(q_cat, k_cat, ckv_t)


def _sample_attn_kernel(q_ref, kn_ref, ckvn_ref, cckv_ref, ckpe_ref, w_uk_ref, g_kn_ref, o_ref,
                        m_ref, l_ref, acc_ref, *, tile, past, n_new):
    m_ref[...] = jnp.full(m_ref.shape, NEG_INF, F32)
    l_ref[...] = jnp.zeros(l_ref.shape, F32)
    acc_ref[...] = jnp.zeros(acc_ref.shape, F32)
    g_kn = g_kn_ref[...]
    qr_all = jnp.concatenate([q_ref[hd][:, QK_NOPE_DIM:] for hd in range(N_HEADS)], axis=0)
    zero = jnp.zeros((n_new, QK_NOPE_DIM), BF16)
    q_bd = jnp.concatenate([
        jnp.concatenate([(q_ref[hd][:, :QK_NOPE_DIM] * g_kn).astype(BF16) if c == hd else zero
                         for c in range(N_HEADS)], axis=1)
        for hd in range(N_HEADS)], axis=0)

    def update(s, v):
        m_prev = m_ref[...]
        m_new = jnp.maximum(m_prev, jnp.max(s, axis=-1, keepdims=True))
        alpha = jnp.exp2(m_prev - m_new)
        pr = jnp.exp2(s - m_new)
        l_ref[...] = alpha * l_ref[...] + jnp.sum(pr, axis=-1, keepdims=True)
        acc_ref[...] = alpha * acc_ref[...] + _dot(pr.astype(BF16), v)
        m_ref[...] = m_new

    def up_project(kt):
        v = cckv_ref[kt * tile:(kt + 1) * tile, :].astype(BF16)
        return v, _dot_nt(w_uk_ref[...], v)

    def attend(kt, v, kn_t):
        kpe_t = ckpe_ref[:, kt * tile:(kt + 1) * tile].astype(BF16)
        kpe_t = jnp.concatenate([kpe_t, jnp.zeros_like(kpe_t)], axis=0)
        s_rope = _dot(qr_all, kpe_t)
        inv = []
        for hd in range(N_HEADS):
            blk = kn_t[hd * QK_NOPE_DIM:(hd + 1) * QK_NOPE_DIM, :]
            ms = jnp.sum(blk * blk, axis=0, keepdims=True) / QK_NOPE_DIM
            inv.append(jnp.broadcast_to(lax.rsqrt(ms + EPS), (n_new, tile)))
        s = _dot(q_bd, kn_t.astype(BF16)) * jnp.concatenate(inv, axis=0) + s_rope
        update(s, v)

    n_tiles = past // tile
    nxt = up_project(0)
    for kt in range(n_tiles):
        cur = nxt
        if kt + 1 < n_tiles:
            nxt = up_project(kt + 1)
        attend(kt, *cur)

    s_new = jnp.concatenate([_dot_nt(q_ref[hd], kn_ref[hd]) for hd in range(N_HEADS)], axis=0)
    q_pos = past + lax.broadcasted_iota(jnp.int32, s_new.shape, 0) % n_new
    k_pos = past + lax.broadcasted_iota(jnp.int32, s_new.shape, 1)
    s_new = jnp.where(k_pos // CHUNK <= q_pos // CHUNK, s_new, NEG_INF)
    update(s_new, ckvn_ref[...])

    o = (acc_ref[...] / l_ref[...]).astype(o_ref.dtype)
    for hd in range(N_HEADS):
        o_ref[:, hd * KV_LORA_RANK:(hd + 1) * KV_LORA_RANK] = o[hd * n_new:(hd + 1) * n_new, :]


def _sample_attention(q_cat, k_cat, ckv_b, cache_ckv, cache_kpe, a, w_uk, g_kn, n_new, tile):
    _, dec_batch, past, _ = cache_ckv.shape
    n = dec_batch * n_new
    rows = N_HEADS * n_new
    hrow = pl.BlockSpec((N_HEADS, n_new, HEAD_PAD), lambda b: (0, b, 0))
    cache_kpe_t = jnp.swapaxes(cache_kpe, 2, 3)
    return pl.pallas_call(
        functools.partial(_sample_attn_kernel, tile=tile, past=past, n_new=n_new),
        grid=(dec_batch,),
        in_specs=[
            hrow, hrow,
            pl.BlockSpec((n_new, KV_LORA_RANK), lambda b: (b, 0)),
            pl.BlockSpec((None, None, past, KV_LORA_RANK), lambda b: (a, b, 0, 0)),
            pl.BlockSpec((None, None, QK_ROPE_DIM, past), lambda b: (a, b, 0, 0)),
            _const_spec(w_uk.shape), _const_spec(g_kn.shape),
        ],
        out_specs=pl.BlockSpec((n_new, N_HEADS * KV_LORA_RANK), lambda b: (b, 0)),
        out_shape=jax.ShapeDtypeStruct((n, N_HEADS * KV_LORA_RANK), BF16),
        scratch_shapes=[
            pltpu.VMEM((rows, 1), F32),
            pltpu.VMEM((rows, 1), F32),
            pltpu.VMEM((rows, KV_LORA_RANK), F32),
        ],
        compiler_params=_cparams(1),
        name="sample_attention",
    )(q_cat, k_cat, ckv_b, cache_ckv, cache_kpe_t, w_uk, g_kn)


def _ffn_tail(x1, g2_ref, sh2_ref, sc2_ref, gt2_ref, w1_ref, w2_ref, ff_chunk):
    h2 = _mod_norm(x1, g2_ref[...], sh2_ref[...], sc2_ref[...]).astype(BF16)
    return x1 + gt2_ref[...] * _sq_relu_ffn(h2, w1_ref, w2_ref, ff_chunk)


def _mla_tail_kernel(x_ref, ol_ref, w_uv_ref, w_o_ref, gt1_ref, g2_ref, sh2_ref, sc2_ref, gt2_ref,
                     w1_ref, w2_ref, o_ref, *, ff_chunk, latent_major):
    heads = []
    for hd in range(N_HEADS):
        lat = slice(hd * KV_LORA_RANK, (hd + 1) * KV_LORA_RANK)
        if latent_major:
            up = lax.dot_general(ol_ref[lat, :], w_uv_ref[hd], (((0,), (0,)), ((), ())),
                                 preferred_element_type=F32)
        else:
            up = _dot(ol_ref[:, lat], w_uv_ref[hd])
        heads.append(up.astype(BF16))
    m = _dot(jnp.concatenate(heads, axis=-1), w_o_ref[...])
    x1 = x_ref[...] + gt1_ref[...] * m
    o_ref[...] = _ffn_tail(x1, g2_ref, sh2_ref, sc2_ref, gt2_ref, w1_ref, w2_ref, ff_chunk)


def _mla_tail(x, o_lat, mods, mod_spec, p, tm, ff_chunk, latent_major):
    n, d = x.shape
    row = lambda w: pl.BlockSpec((tm, w), lambda i: (i, 0))
    consts_a = [p["w_uv"], p["w_o"]]
    consts_b = [p["w1"], p["w2"]]
    if latent_major:
        _, nq, rows, tile = o_lat.shape
        assert tile == tm, (tile, tm)
        ol_spec = pl.BlockSpec((None, None, rows, tm), lambda i: (i // nq, i % nq, 0, 0))
    else:
        ol_spec = row(o_lat.shape[1])
    return pl.pallas_call(
        functools.partial(_mla_tail_kernel, ff_chunk=ff_chunk, latent_major=latent_major),
        grid=(n // tm,),
        in_specs=[row(d), ol_spec] + [_const_spec(c.shape) for c in consts_a]
        + [mod_spec, _const_spec(p["g2"].shape), mod_spec, mod_spec, mod_spec]
        + [_layer_spec(c, p["layer"]) for c in consts_b],
        out_specs=row(d),
        out_shape=jax.ShapeDtypeStruct((n, d), F32),
        compiler_params=_cparams(1),
        name="mla_tail_ffn",
    )(x, o_lat, *consts_a, mods[2], p["g2"], mods[3], mods[4], mods[5], *consts_b)


def _sgu_layer_kernel(x_ref, g1_ref, sh1_ref, sc1_ref, gt1_ref, w_in_ref, g_v_ref, w_s_ref, b_s_ref,
                      w_o_ref, g2_ref, sh2_ref, sc2_ref, gt2_ref, w1_ref, w2_ref, *out_refs,
                      ff_chunk, period, offset, emit_v):
    x = x_ref[...]
    tm = x.shape[0]
    width = w_o_ref.shape[0]
    gdim = width // SGU_GROUPS
    h = _mod_norm(x, g1_ref[...], sh1_ref[...], sc1_ref[...]).astype(BF16)
    z = jax.nn.gelu(_dot(h, w_in_ref[...]))
    u = z[:, :width]
    v = _rms(z[:, width:], width) * g_v_ref[...]
    if emit_v:
        out_refs[1][...] = v
    vb = v.astype(BF16)

    ri = lax.broadcasted_iota(jnp.int32, (SGU_CHUNK, SGU_CHUNK), 0)
    ci = lax.broadcasted_iota(jnp.int32, (SGU_CHUNK, SGU_CHUNK), 1)
    vis = (ri // period == ci // period) & ((ci % period + offset) // CHUNK <= (ri % period + offset) // CHUNK)
    w_mix = [jnp.where(vis, w_s_ref[g], 0.0).astype(BF16) for g in range(SGU_GROUPS)]

    rows = []
    for c in range(tm // SGU_CHUNK):
        r0 = c * SGU_CHUNK
        cols = []
        for g in range(SGU_GROUPS):
            l0 = g * gdim
            mixed = _dot(w_mix[g], vb[r0:r0 + SGU_CHUNK, l0:l0 + gdim]) + b_s_ref[g]
            cols.append((u[r0:r0 + SGU_CHUNK, l0:l0 + gdim] * mixed).astype(BF16))
        rows.append(jnp.concatenate(cols, axis=-1))
    gated = jnp.concatenate(rows, axis=0)
    x1 = x + gt1_ref[...] * _dot(gated, w_o_ref[...])
    out_refs[0][...] = _ffn_tail(x1, g2_ref, sh2_ref, sc2_ref, gt2_ref, w1_ref, w2_ref, ff_chunk)


def _sgu_layer(x, mods, mod_spec, p, tm, ff_chunk, period, offset, emit_v):
    n, d = x.shape
    width = p["w_o"].shape[0]
    row = lambda w: pl.BlockSpec((tm, w), lambda i: (i, 0))
    cs = lambda a: _const_spec(a.shape)
    out_specs = [row(d)]
    out_shape = [jax.ShapeDtypeStruct((n, d), F32)]
    if emit_v:
        out_specs.append(row(width))
        out_shape.append(jax.ShapeDtypeStruct((n, width), F32))
    return pl.pallas_call(
        functools.partial(_sgu_layer_kernel, ff_chunk=ff_chunk, period=period, offset=offset, emit_v=emit_v),
        grid=(n // tm,),
        in_specs=[row(d), cs(p["g1"]), mod_spec, mod_spec, mod_spec, cs(p["w_in"]), cs(p["g_v"]),
                  cs(p["w_s"]), cs(p["b_s"]), cs(p["w_o"]), cs(p["g2"]), mod_spec, mod_spec, mod_spec,
                  _layer_spec(p["w1"], p["layer"]), _layer_spec(p["w2"], p["layer"])],
        out_specs=out_specs,
        out_shape=out_shape,
        compiler_params=_cparams(1),
        name="sgu_layer_ffn",
    )(x, p["g1"], mods[0], mods[1], mods[2], p["w_in"], p["g_v"], p["w_s"], p["b_s"], p["w_o"],
      p["g2"], mods[3], mods[4], mods[5], p["w1"], p["w2"])


def _rope_tables(pos):
    half = QK_ROPE_DIM // 2
    inv = 1.0 / (ROPE_THETA ** (jnp.arange(half, dtype=F32) / half))
    ang = pos.astype(F32)[:, None] * inv[None, :]
    cos, sin = jnp.cos(ang), jnp.sin(ang)
    z = jnp.zeros((pos.shape[0], LANES - QK_ROPE_DIM), F32)
    return jnp.concatenate([cos, cos, z], axis=-1), jnp.concatenate([-sin, sin, z], axis=-1)


def _pad_lanes(g, n):
    return jnp.pad(g, (0, n - g.shape[0])).reshape(1, n)


def _prep_mla(a, norm1_g, norm2_g, ffn_w1, ffn_w2, layer, mla_w_in, mla_g_qa, mla_g_kva, mla_w_q_up,
              mla_w_uk, mla_w_uv, mla_g_qn, mla_g_qr, mla_g_kn, mla_g_kr, mla_w_o):
    d = mla_w_in.shape[1]
    w_in = jnp.pad(mla_w_in[a], ((0, 0), (0, LANES - QK_ROPE_DIM))).astype(BF16)
    half = QK_ROPE_DIM // 2
    w_q_up = mla_w_q_up[a].reshape(Q_LORA_RANK, N_HEADS, QK_NOPE_DIM + QK_ROPE_DIM)
    w_r1 = w_q_up[:, :, QK_NOPE_DIM:QK_NOPE_DIM + half]
    w_r2 = w_q_up[:, :, QK_NOPE_DIM + half:]
    w_z = jnp.zeros((Q_LORA_RANK, N_HEADS, LANES - QK_ROPE_DIM), w_q_up.dtype)
    w_q_up = jnp.concatenate([w_q_up[:, :, :QK_NOPE_DIM], w_r1, w_r2, w_z, w_r2, w_r1, w_z], axis=-1)
    g_qr = mla_g_qr[a]
    g_qr = jnp.concatenate([_pad_lanes(g_qr, LANES),
                            _pad_lanes(jnp.concatenate([g_qr[half:], g_qr[:half]]), LANES)], axis=0)
    return {
        "g1": norm1_g[layer].reshape(1, d), "g2": norm2_g[layer].reshape(1, d),
        "w1": ffn_w1, "w2": ffn_w2, "layer": layer,
        "w_in": w_in,
        "g_qa": mla_g_qa[a].reshape(1, -1), "g_kva": mla_g_kva[a].reshape(1, -1),
        "g_kr": _pad_lanes(mla_g_kr[a], LANES),
        "w_q_up": w_q_up.reshape(Q_LORA_RANK, N_HEADS * Q_HEAD_COLS).astype(BF16),
        "g_qn": mla_g_qn[a].reshape(1, -1), "g_qr": g_qr,
        "w_uk": mla_w_uk[a].reshape(KV_LORA_RANK, N_HEADS * QK_NOPE_DIM).astype(BF16),
        "g_kn": mla_g_kn[a].reshape(1, -1),
        "w_uv": jnp.swapaxes(mla_w_uv[a], 0, 1).astype(BF16),
        "w_o": mla_w_o[a].astype(BF16),
    }


def _prep_sgu(b, norm1_g, norm2_g, ffn_w1, ffn_w2, layer, cm_w_in, cm_g_v, cm_w_s, cm_b_s, cm_w_o, idx):
    d = cm_w_in.shape[1]
    period = len(idx)
    lo = int(idx[0])
    assert SGU_CHUNK % period == 0 and np.array_equal(idx, lo + np.arange(period)), idx
    reps = SGU_CHUNK // period
    w_s = jnp.tile(cm_w_s[b][:, lo:lo + period, lo:lo + period], (1, reps, reps))
    b_s = jnp.tile(cm_b_s[b][:, lo:lo + period], (1, reps))
    gdim = cm_w_o.shape[1] // SGU_GROUPS
    return {
        "g1": norm1_g[layer].reshape(1, d), "g2": norm2_g[layer].reshape(1, d),
        "w1": ffn_w1, "w2": ffn_w2, "layer": layer,
        "w_in": cm_w_in[b].astype(BF16), "g_v": cm_g_v[b].reshape(1, -1),
        "w_s": w_s, "b_s": jnp.broadcast_to(b_s[:, :, None], (SGU_GROUPS, SGU_CHUNK, gdim)),
        "w_o": cm_w_o[b].astype(BF16),
    }, period, lo


def kernel(x_prompt, x_sample, cache_ckv, cache_kpe, c_prompt, c_sample, ada_w, ada_b, norm1_g, norm2_g,
           ffn_w1, ffn_w2, mla_w_in, mla_g_qa, mla_g_kva, mla_w_q_up, mla_w_uk, mla_w_uv, mla_g_qn,
           mla_g_qr, mla_g_kn, mla_g_kr, mla_w_o, cm_w_in, cm_g_v, cm_w_s, cm_b_s, cm_w_o):
    batch, seq, d = x_prompt.shape
    dec_batch, dec_seq, _ = x_sample.shape
    past = cache_ckv.shape[2]
    depth = ada_w.shape[0]
    n_p, n_s = batch * seq, dec_batch * dec_seq
    tm_p, tm_proj, tm_s, attn_tile, attn_heads, ff_chunk = 512, 1024, n_s, 512, 2, 1024
    tiles_per_seq = seq // tm_p

    pos_p = np.arange(seq)
    pos_s = past + np.arange(dec_seq)
    rc_p, rs_p = _rope_tables(jnp.asarray(pos_p))
    rc_s, rs_s = _rope_tables(jnp.asarray(np.tile(pos_s, dec_batch)))
    rope_spec_p = pl.BlockSpec((tm_proj, LANES), lambda i: (i % (seq // tm_proj), 0))
    rope_spec_s = pl.BlockSpec((tm_s, LANES), lambda i: (i, 0))

    n_c = batch + dec_batch
    c_all = jnp.concatenate([c_prompt, c_sample], axis=0)
    c_all = jnp.pad(c_all, ((0, -n_c % BF16_ROWS), (0, 0)))
    mod = _ada_modulation(c_all, ada_w, ada_b)[:, :n_c]
    mod_spec_p = pl.BlockSpec((None, 1, d), lambda i: (i // tiles_per_seq, 0, 0))
    mod_spec_proj = pl.BlockSpec((None, 1, d), lambda i: (i // (seq // tm_proj), 0, 0))
    mod_spec_s = pl.BlockSpec((tm_s, d), lambda i: (i, 0))

    def split_mods(layer):
        m6 = mod[layer].reshape(batch + dec_batch, 6, d)
        m_p = [m6[:batch, j].reshape(batch, 1, d) for j in range(6)]
        m_s = [jnp.repeat(m6[batch:, j], dec_seq, axis=0) for j in range(6)]
        return m_p, m_s

    ffn_w1 = ffn_w1.astype(BF16)
    ffn_w2 = ffn_w2.astype(BF16)
    x_p = x_prompt.reshape(n_p, d)
    x_s = x_sample.reshape(n_s, d)
    ckv_p_l, kpe_p_l, ckv_s_l, kpe_s_l, v_s_l = [], [], [], [], []
    for layer in range(depth):
        m_p, m_s = split_mods(layer)
        if layer % 2 == 0:
            a = layer // 2
            p = _prep_mla(a, norm1_g, norm2_g, ffn_w1, ffn_w2, layer, mla_w_in, mla_g_qa, mla_g_kva,
                          mla_w_q_up, mla_w_uk, mla_w_uv, mla_g_qn, mla_g_qr, mla_g_kn, mla_g_kr, mla_w_o)
            q_p, k_p, ckv_p, _, ckvt_p, kr_p = _mla_project(x_p, m_p, mod_spec_proj, p, rc_p, rs_p, rope_spec_p,
                                                            tm_proj, attn_tile)
            q_s, k_s, ckv_s, ckvb_s, _, kr_s = _mla_project(x_s, m_s, mod_spec_s, p, rc_s, rs_s, rope_spec_s,
                                                            tm_s, attn_tile)
            ol_p = _prompt_attention(q_p, k_p, ckvt_p, batch, seq, attn_tile, attn_heads)
            ol_s = _sample_attention(q_s, k_s, ckvb_s, cache_ckv, cache_kpe, a, p["w_uk"].T, p["g_kn"],
                                     dec_seq, attn_tile)
            x_p = _mla_tail(x_p, ol_p, m_p, mod_spec_p, p, tm_p, ff_chunk, True)
            x_s = _mla_tail(x_s, ol_s, m_s, mod_spec_s, p, tm_s, ff_chunk, False)
            ckv_p_l.append(ckv_p.reshape(batch, seq, -1))
            kpe_p_l.append(kr_p.reshape(batch, seq, -1))
            ckv_s_l.append(ckv_s.reshape(dec_batch, dec_seq, -1))
            kpe_s_l.append(kr_s.reshape(dec_batch, dec_seq, -1))
        else:
            b = layer // 2
            args = (b, norm1_g, norm2_g, ffn_w1, ffn_w2, layer, cm_w_in, cm_g_v, cm_w_s, cm_b_s, cm_w_o)
            pp, period_p, off_p = _prep_sgu(*args, np.arange(SGU_CHUNK))
            ps, period_s, off_s = _prep_sgu(*args, pos_s % SGU_CHUNK)
            (x_p,) = _sgu_layer(x_p, m_p, mod_spec_p, pp, tm_p, ff_chunk, period_p, off_p, False)
            x_s, v_s = _sgu_layer(x_s, m_s, mod_spec_s, ps, tm_s, ff_chunk, period_s, off_s, True)
            v_s_l.append(v_s.reshape(dec_batch, dec_seq, -1))

    return (x_p.reshape(batch, seq, d), x_s.reshape(dec_batch, dec_seq, d),
            jnp.stack(ckv_p_l), jnp.stack(kpe_p_l), jnp.stack(ckv_s_l), jnp.stack(kpe_s_l),
            jnp.stack(v_s_l))
```

```python
import functools

import numpy as np
import jax
import jax.numpy as jnp
from jax import lax
from jax.experimental import pallas as pl
from jax.experimental.pallas import tpu as pltpu

F32 = jnp.float32
BF16 = jnp.bfloat16

CHUNK = 64
N_HEADS = 8
QK_NOPE_DIM = 128
QK_ROPE_DIM = 64
V_HEAD_DIM = 128
Q_LORA_RANK = 384
KV_LORA_RANK = 256
ROPE_THETA = 10000.0
SGU_CHUNK = 128
SGU_GROUPS = 8
EPS = 1e-6

LANES = 128
BF16_ROWS = 16
MXU_DIM = 256
HEAD_PAD = 2 * LANES
Q_HEAD_COLS = 3 * LANES
VMEM_LIMIT = 56 * 1024 * 1024

NEG_INF = float(np.finfo(np.float32).min)
Q_SCALE = float((QK_NOPE_DIM + QK_ROPE_DIM) ** -0.5 * np.log2(np.e))


def _cparams(n_axes):
    return pltpu.CompilerParams(
        dimension_semantics=("arbitrary",) * n_axes,
        vmem_limit_bytes=VMEM_LIMIT,
    )


def _const_spec(shape):
    nd = len(shape)
    return pl.BlockSpec(shape, lambda *_: (0,) * nd, pipeline_mode=pl.Buffered(1))


def _layer_spec(stacked, layer):
    return pl.BlockSpec((None,) + stacked.shape[1:], lambda *_: (layer, 0, 0), pipeline_mode=pl.Buffered(1))


def _dot(a, b):
    return jnp.dot(a, b, preferred_element_type=F32)


def _dot_nt(a, b):
    return lax.dot_general(a, b, (((1,), (1,)), ((), ())), preferred_element_type=F32)


def _rms(x, n):
    ms = jnp.sum(x * x, axis=-1, keepdims=True) / n
    return x * lax.rsqrt(ms + EPS)


def _rope128(x, c, s):
    half = QK_ROPE_DIM // 2
    rot = pltpu.roll(x, LANES - half, axis=1) + pltpu.roll(x, half, axis=1)
    return x * c + rot * s


def _mod_norm(x, g, shift, scale):
    return _rms(x, x.shape[-1]) * g * (1.0 + scale) + shift


def _sq_relu_ffn(hb, w1_ref, w2_ref, ff_chunk):
    d_ff = w1_ref.shape[1]
    acc = None
    for c in range(d_ff // ff_chunk):
        a = _dot(hb, w1_ref[:, c * ff_chunk:(c + 1) * ff_chunk])
        a = jnp.maximum(a, 0.0)
        a = (a * a).astype(BF16)
        part = _dot(a, w2_ref[c * ff_chunk:(c + 1) * ff_chunk, :])
        acc = part if acc is None else acc + part
    return acc


def _ada_kernel(c_ref, w_ref, b_ref, o_ref):
    s = jax.nn.silu(c_ref[...]).astype(BF16)
    o_ref[...] = _dot(s, w_ref[...].astype(BF16)) + b_ref[...]


def _ada_modulation(c_all, ada_w, ada_b):
    depth, d, n6 = ada_w.shape
    rows = c_all.shape[0]
    tn = 1536
    return pl.pallas_call(
        _ada_kernel,
        grid=(depth, n6 // tn),
        in_specs=[
            pl.BlockSpec((rows, d), lambda l, j: (0, 0)),
            pl.BlockSpec((None, d, tn), lambda l, j: (l, 0, j)),
            pl.BlockSpec((None, 1, tn), lambda l, j: (l, 0, j)),
        ],
        out_specs=pl.BlockSpec((None, rows, tn), lambda l, j: (l, 0, j)),
        out_shape=jax.ShapeDtypeStruct((depth, rows, n6), F32),
        compiler_params=_cparams(2),
        name="ada_modulation",
    )(c_all, ada_w, ada_b.reshape(depth, 1, n6))


def _mla_proj_kernel(x_ref, g1_ref, sh_ref, sc_ref, w_in_ref, g_qa_ref, g_kva_ref, g_kr_ref,
                     w_qup_ref, g_qn_ref, g_qr_ref, w_uk_ref, g_kn_ref, rc_ref, rs_ref,
                     q_ref, k_ref, ckv_ref, ckvb_ref, ckvt_ref, kr_ref):
    x = x_ref[...]
    h = _mod_norm(x, g1_ref[...], sh_ref[...], sc_ref[...]).astype(BF16)
    a = _dot(h, w_in_ref[...])
    cq = _rms(a[:, :Q_LORA_RANK], Q_LORA_RANK) * g_qa_ref[...]
    c0 = Q_LORA_RANK
    ckv = _rms(a[:, c0:c0 + KV_LORA_RANK], KV_LORA_RANK) * g_kva_ref[...]
    c1 = c0 + KV_LORA_RANK
    rc = rc_ref[...]
    rs = rs_ref[...]
    kr = _rope128(_rms(a[:, c1:c1 + LANES], QK_ROPE_DIM) * g_kr_ref[...], rc, rs)
    ckv_ref[...] = ckv
    ckv_b = ckv.astype(BF16)
    ckvb_ref[...] = ckv_b
    vt_tile = ckvt_ref.shape[-1]
    for j in range(ckvt_ref.shape[0]):
        ckvt_ref[j] = ckv[j * vt_tile:(j + 1) * vt_tile, :].T.astype(BF16)
    kr_ref[...] = kr[:, :QK_ROPE_DIM]
    kr_b = kr.astype(BF16)

    q = _dot(cq.astype(BF16), w_qup_ref[...])
    kn_all = _dot(ckv_b, w_uk_ref[...])
    g_qn = g_qn_ref[...] * Q_SCALE
    rot_c = rc * (g_qr_ref[0:1, :] * Q_SCALE)
    rot_s = rs * (g_qr_ref[1:2, :] * Q_SCALE)
    g_kn = g_kn_ref[...]
    for hd in range(N_HEADS):
        o = hd * Q_HEAD_COLS
        qn = _rms(q[:, o:o + QK_NOPE_DIM], QK_NOPE_DIM) * g_qn
        x = q[:, o + QK_NOPE_DIM:o + 2 * LANES]
        x_sw = q[:, o + 2 * LANES:o + 3 * LANES]
        inv = lax.rsqrt(jnp.sum(x * x, axis=-1, keepdims=True) / QK_ROPE_DIM + EPS)
        qr = (x * rot_c + x_sw * rot_s) * inv
        q_ref[hd] = jnp.concatenate([qn.astype(BF16), qr.astype(BF16)], axis=-1)
        kn = _rms(kn_all[:, hd * QK_NOPE_DIM:(hd + 1) * QK_NOPE_DIM], QK_NOPE_DIM) * g_kn
        k_ref[hd] = jnp.concatenate([kn.astype(BF16), kr_b], axis=-1)


def _mla_project(x, mods, mod_spec, p, rope_c, rope_s, rope_spec, tm, vt_tile):
    n, d = x.shape
    row = lambda w: pl.BlockSpec((tm, w), lambda i: (i, 0))
    hrow = pl.BlockSpec((N_HEADS, tm, HEAD_PAD), lambda i: (0, i, 0))
    consts = [p["w_in"], p["g_qa"], p["g_kva"], p["g_kr"], p["w_q_up"], p["g_qn"], p["g_qr"],
              p["w_uk"], p["g_kn"]]
    return pl.pallas_call(
        _mla_proj_kernel,
        grid=(n // tm,),
        in_specs=[row(d), _const_spec(p["g1"].shape), mod_spec, mod_spec]
        + [_const_spec(c.shape) for c in consts] + [rope_spec, rope_spec],
        out_specs=[hrow, hrow, row(KV_LORA_RANK), row(KV_LORA_RANK),
                   pl.BlockSpec((tm // vt_tile, KV_LORA_RANK, vt_tile), lambda i: (i, 0, 0)), row(QK_ROPE_DIM)],
        out_shape=[
            jax.ShapeDtypeStruct((N_HEADS, n, HEAD_PAD), BF16),
            jax.ShapeDtypeStruct((N_HEADS, n, HEAD_PAD), BF16),
            jax.ShapeDtypeStruct((n, KV_LORA_RANK), F32),
            jax.ShapeDtypeStruct((n, KV_LORA_RANK), BF16),
            jax.ShapeDtypeStruct((n // vt_tile, KV_LORA_RANK, vt_tile), BF16),
            jax.ShapeDtypeStruct((n, QK_ROPE_DIM), F32),
        ],
        compiler_params=_cparams(1),
        name="mla_project",
    )(x, p["g1"], mods[0], mods[1], *consts, rope_c, rope_s)


def _prompt_attn_kernel(q_ref, k_ref, vt_ref, o_ref, m_ref, l_ref, acc_ref, st_ref, *, tile, heads):
    qi = pl.program_id(2)
    m_ref[...] = jnp.full(m_ref.shape, NEG_INF, F32)
    l_ref[...] = jnp.zeros(l_ref.shape, F32)
    acc_ref[...] = jnp.zeros(acc_ref.shape, F32)

    def scores(kt, slot, gs=range(heads)):
        start = pl.multiple_of(kt * tile, tile)
        for g in gs:
            st_ref[slot, g] = _dot_nt(k_ref[g, pl.ds(start, tile), :], q_ref[g])

    def step(kt, slot, masked, gs=range(heads)):
        vt = vt_ref[kt]
        for g in gs:
            for w in range(tile // MXU_DIM):
                nk = (w + 1) * MXU_DIM if masked else tile
                alphas, probs = [], []
                for j in range(w * MXU_DIM // LANES, (w + 1) * MXU_DIM // LANES):
                    cols = slice(j * LANES, (j + 1) * LANES)
                    col = st_ref[slot, g, :nk, cols]
                    if masked:
                        kc = lax.broadcasted_iota(jnp.int32, col.shape, 0) // CHUNK
                        qc = (lax.broadcasted_iota(jnp.int32, col.shape, 1) + j * LANES) // CHUNK
                        col = jnp.where(kc <= qc, col, NEG_INF)
                    m_prev = m_ref[g, :, cols]
                    m_new = jnp.maximum(m_prev, jnp.max(col, axis=0, keepdims=True))
                    alpha = jnp.exp2(m_prev - m_new)
                    pr = jnp.exp2(col - m_new)
                    l_ref[g, :, cols] = alpha * l_ref[g, :, cols] + jnp.sum(pr, axis=0, keepdims=True)
                    m_ref[g, :, cols] = m_new
                    probs.append(pr.astype(BF16))
                    alphas.append(alpha)
                wide = slice(w * MXU_DIM, (w + 1) * MXU_DIM)
                acc_ref[g, :, wide] = (acc_ref[g, :, wide] * jnp.concatenate(alphas, axis=-1)
                                       + _dot(vt[:, :nk], jnp.concatenate(probs, axis=-1)))

    def body(i, carry):
        kt = 2 * i
        for g in range(heads):
            scores(kt + 1, 1, [g])
            step(kt, 0, False, [g])
        for g in range(heads):
            scores(kt + 2, 0, [g])
            step(kt + 1, 1, False, [g])
        return carry

    scores(0, 0)
    lax.fori_loop(0, qi // 2, body, 0)

    @pl.when(qi % 2 == 0)
    def _():
        step(qi, 0, True)

    @pl.when(qi % 2 == 1)
    def _():
        scores(qi, 1)
        step(qi - 1, 0, False)
        step(qi, 1, True)

    for g in range(heads):
        o_ref[g * KV_LORA_RANK:(g + 1) * KV_LORA_RANK, :] = (acc_ref[g] / l_ref[g]).astype(o_ref.dtype)


def _prompt_attention(q_cat, k_cat, ckv_t, batch, seq, tile, heads):
    n = batch * seq
    nq = seq // tile
    assert ckv_t.shape == (n // tile, KV_LORA_RANK, tile), ckv_t.shape
    return pl.pallas_call(
        functools.partial(_prompt_attn_kernel, tile=tile, heads=heads),
        grid=(batch, N_HEADS // heads, nq),
        in_specs=[
            pl.BlockSpec((heads, tile, HEAD_PAD), lambda b, h, i: (h, b * nq + i, 0)),
            pl.BlockSpec((heads, seq, HEAD_PAD), lambda b, h, i: (h, b, 0)),
            pl.BlockSpec((nq, KV_LORA_RANK, tile), lambda b, h, i: (b, 0, 0)),
        ],
        out_specs=pl.BlockSpec((heads * KV_LORA_RANK, tile), lambda b, h, i: (h, b * nq + i)),
        out_shape=jax.ShapeDtypeStruct((N_HEADS * KV_LORA_RANK, n), BF16),
        scratch_shapes=[
            pltpu.VMEM((heads, 1, tile), F32),
            pltpu.VMEM((heads, 1, tile), F32),
            pltpu.VMEM((heads, KV_LORA_RANK, tile), F32),
            pltpu.VMEM((2, heads, tile, tile), F32),
        ],
        compiler_params=_cparams(3),
        name="prompt_attention",
    )(q_cat, k_cat, ckv_t)


def _sample_attn_kernel(q_ref, kn_ref, ckvn_ref, cckv_ref, ckpe_ref, w_uk_ref, g_kn_ref, o_ref,
                        m_ref, l_ref, acc_ref, *, tile, past, n_new):
    m_ref[...] = jnp.full(m_ref.shape, NEG_INF, F32)
    l_ref[...] = jnp.zeros(l_ref.shape, F32)
    acc_ref[...] = jnp.zeros(acc_ref.shape, F32)
    g_kn = g_kn_ref[...]
    qr_all = jnp.concatenate([q_ref[hd][:, QK_NOPE_DIM:] for hd in range(N_HEADS)], axis=0)
    zero = jnp.zeros((n_new, QK_NOPE_DIM), BF16)
    q_bd = jnp.concatenate([
        jnp.concatenate([(q_ref[hd][:, :QK_NOPE_DIM] * g_kn).astype(BF16) if c == hd else zero
                         for c in range(N_HEADS)], axis=1)
        for hd in range(N_HEADS)], axis=0)

    def update(s, v):
        m_prev = m_ref[...]
        m_new = jnp.maximum(m_prev, jnp.max(s, axis=-1, keepdims=True))
        alpha = jnp.exp2(m_prev - m_new)
        pr = jnp.exp2(s - m_new)
        l_ref[...] = alpha * l_ref[...] + jnp.sum(pr, axis=-1, keepdims=True)
        acc_ref[...] = alpha * acc_ref[...] + _dot(pr.astype(BF16), v)
        m_ref[...] = m_new

    def up_project(kt):
        v = cckv_ref[kt * tile:(kt + 1) * tile, :].astype(BF16)
        return v, _dot_nt(w_uk_ref[...], v)

    def attend(kt, v, kn_t):
        kpe_t = ckpe_ref[:, kt * tile:(kt + 1) * tile].astype(BF16)
        kpe_t = jnp.concatenate([kpe_t, jnp.zeros_like(kpe_t)], axis=0)
        s_rope = _dot(qr_all, kpe_t)
        inv = []
        for hd in range(N_HEADS):
            blk = kn_t[hd * QK_NOPE_DIM:(hd + 1) * QK_NOPE_DIM, :]
            ms = jnp.sum(blk * blk, axis=0, keepdims=True) / QK_NOPE_DIM
            inv.append(jnp.broadcast_to(lax.rsqrt(ms + EPS), (n_new, tile)))
        s = _dot(q_bd, kn_t.astype(BF16)) * jnp.concatenate(inv, axis=0) + s_rope
        update(s, v)

    n_tiles = past // tile
    nxt = up_project(0)
    for kt in range(n_tiles):
        cur = nxt
        if kt + 1 < n_tiles:
            nxt = up_project(kt + 1)
        attend(kt, *cur)

    s_new = jnp.concatenate([_dot_nt(q_ref[hd], kn_ref[hd]) for hd in range(N_HEADS)], axis=0)
    q_pos = past + lax.broadcasted_iota(jnp.int32, s_new.shape, 0) % n_new
    k_pos = past + lax.broadcasted_iota(jnp.int32, s_new.shape, 1)
    s_new = jnp.where(k_pos // CHUNK <= q_pos // CHUNK, s_new, NEG_INF)
    update(s_new, ckvn_ref[...])

    o = (acc_ref[...] / l_ref[...]).astype(o_ref.dtype)
    for hd in range(N_HEADS):
        o_ref[:, hd * KV_LORA_RANK:(hd + 1) * KV_LORA_RANK] = o[hd * n_new:(hd + 1) * n_new, :]


def _sample_attention(q_cat, k_cat, ckv_b, cache_ckv, cache_kpe, a, w_uk, g_kn, n_new, tile):
    _, dec_batch, past, _ = cache_ckv.shape
    n = dec_batch * n_new
    rows = N_HEADS * n_new
    hrow = pl.BlockSpec((N_HEADS, n_new, HEAD_PAD), lambda b: (0, b, 0))
    cache_kpe_t = jnp.swapaxes(cache_kpe, 2, 3)
    return pl.pallas_call(
        functools.partial(_sample_attn_kernel, tile=tile, past=past, n_new=n_new),
        grid=(dec_batch,),
        in_specs=[
            hrow, hrow,
            pl.BlockSpec((n_new, KV_LORA_RANK), lambda b: (b, 0)),
            pl.BlockSpec((None, None, past, KV_LORA_RANK), lambda b: (a, b, 0, 0)),
            pl.BlockSpec((None, None, QK_ROPE_DIM, past), lambda b: (a, b, 0, 0)),
            _const_spec(w_uk.shape), _const_spec(g_kn.shape),
        ],
        out_specs=pl.BlockSpec((n_new, N_HEADS * KV_LORA_RANK), lambda b: (b, 0)),
        out_shape=jax.ShapeDtypeStruct((n, N_HEADS * KV_LORA_RANK), BF16),
        scratch_shapes=[
            pltpu.VMEM((rows, 1), F32),
            pltpu.VMEM((rows, 1), F32),
            pltpu.VMEM((rows, KV_LORA_RANK), F32),
        ],
        compiler_params=_cparams(1),
        name="sample_attention",
    )(q_cat, k_cat, ckv_b, cache_ckv, cache_kpe_t, w_uk, g_kn)


def _ffn_tail(x1, g2_ref, sh2_ref, sc2_ref, gt2_ref, w1_ref, w2_ref, ff_chunk):
    h2 = _mod_norm(x1, g2_ref[...], sh2_ref[...], sc2_ref[...]).astype(BF16)
    return x1 + gt2_ref[...] * _sq_relu_ffn(h2, w1_ref, w2_ref, ff_chunk)


def _mla_tail_kernel(x_ref, ol_ref, w_uv_ref, w_o_ref, gt1_ref, g2_ref, sh2_ref, sc2_ref, gt2_ref,
                     w1_ref, w2_ref, o_ref, *, ff_chunk, latent_major):
    heads = []
    for hd in range(N_HEADS):
        lat = slice(hd * KV_LORA_RANK, (hd + 1) * KV_LORA_RANK)
        if latent_major:
            up = lax.dot_general(ol_ref[lat, :], w_uv_ref[hd], (((0,), (0,)), ((), ())),
                                 preferred_element_type=F32)
        else:
            up = _dot(ol_ref[:, lat], w_uv_ref[hd])
        heads.append(up.astype(BF16))
    m = _dot(jnp.concatenate(heads, axis=-1), w_o_ref[...])
    x1 = x_ref[...] + gt1_ref[...] * m
    o_ref[...] = _ffn_tail(x1, g2_ref, sh2_ref, sc2_ref, gt2_ref, w1_ref, w2_ref, ff_chunk)


def _mla_tail(x, o_lat, mods, mod_spec, p, tm, ff_chunk, latent_major):
    n, d = x.shape
    row = lambda w: pl.BlockSpec((tm, w), lambda i: (i, 0))
    consts_a = [p["w_uv"], p["w_o"]]
    consts_b = [p["w1"], p["w2"]]
    if latent_major:
        ol_spec = pl.BlockSpec((o_lat.shape[0], tm), lambda i: (0, i))
    else:
        ol_spec = row(o_lat.shape[1])
    return pl.pallas_call(
        functools.partial(_mla_tail_kernel, ff_chunk=ff_chunk, latent_major=latent_major),
        grid=(n // tm,),
        in_specs=[row(d), ol_spec] + [_const_spec(c.shape) for c in consts_a]
        + [mod_spec, _const_spec(p["g2"].shape), mod_spec, mod_spec, mod_spec]
        + [_layer_spec(c, p["layer"]) for c in consts_b],
        out_specs=row(d),
        out_shape=jax.ShapeDtypeStruct((n, d), F32),
        compiler_params=_cparams(1),
        name="mla_tail_ffn",
    )(x, o_lat, *consts_a, mods[2], p["g2"], mods[3], mods[4], mods[5], *consts_b)


def _sgu_layer_kernel(x_ref, g1_ref, sh1_ref, sc1_ref, gt1_ref, w_in_ref, g_v_ref, w_s_ref, b_s_ref,
                      w_o_ref, g2_ref, sh2_ref, sc2_ref, gt2_ref, w1_ref, w2_ref, *out_refs,
                      ff_chunk, period, offset, emit_v):
    x = x_ref[...]
    tm = x.shape[0]
    width = w_o_ref.shape[0]
    gdim = width // SGU_GROUPS
    h = _mod_norm(x, g1_ref[...], sh1_ref[...], sc1_ref[...]).astype(BF16)
    z = jax.nn.gelu(_dot(h, w_in_ref[...]))
    u = z[:, :width]
    v = _rms(z[:, width:], width) * g_v_ref[...]
    if emit_v:
        out_refs[1][...] = v
    vb = v.astype(BF16)

    ri = lax.broadcasted_iota(jnp.int32, (SGU_CHUNK, SGU_CHUNK), 0)
    ci = lax.broadcasted_iota(jnp.int32, (SGU_CHUNK, SGU_CHUNK), 1)
    vis = (ri // period == ci // period) & ((ci % period + offset) // CHUNK <= (ri % period + offset) // CHUNK)
    w_mix = [jnp.where(vis, w_s_ref[g], 0.0).astype(BF16) for g in range(SGU_GROUPS)]

    rows = []
    for c in range(tm // SGU_CHUNK):
        r0 = c * SGU_CHUNK
        cols = []
        for g in range(SGU_GROUPS):
            l0 = g * gdim
            mixed = _dot(w_mix[g], vb[r0:r0 + SGU_CHUNK, l0:l0 + gdim]) + b_s_ref[g]
            cols.append((u[r0:r0 + SGU_CHUNK, l0:l0 + gdim] * mixed).astype(BF16))
        rows.append(jnp.concatenate(cols, axis=-1))
    gated = jnp.concatenate(rows, axis=0)
    x1 = x + gt1_ref[...] * _dot(gated, w_o_ref[...])
    out_refs[0][...] = _ffn_tail(x1, g2_ref, sh2_ref, sc2_ref, gt2_ref, w1_ref, w2_ref, ff_chunk)


def _sgu_layer(x, mods, mod_spec, p, tm, ff_chunk, period, offset, emit_v):
    n, d = x.shape
    width = p["w_o"].shape[0]
    row = lambda w: pl.BlockSpec((tm, w), lambda i: (i, 0))
    cs = lambda a: _const_spec(a.shape)
    out_specs = [row(d)]
    out_shape = [jax.ShapeDtypeStruct((n, d), F32)]
    if emit_v:
        out_specs.append(row(width))
        out_shape.append(jax.ShapeDtypeStruct((n, width), F32))
    return pl.pallas_call(
        functools.partial(_sgu_layer_kernel, ff_chunk=ff_chunk, period=period, offset=offset, emit_v=emit_v),
        grid=(n // tm,),
        in_specs=[row(d), cs(p["g1"]), mod_spec, mod_spec, mod_spec, cs(p["w_in"]), cs(p["g_v"]),
                  cs(p["w_s"]), cs(p["b_s"]), cs(p["w_o"]), cs(p["g2"]), mod_spec, mod_spec, mod_spec,
                  _layer_spec(p["w1"], p["layer"]), _layer_spec(p["w2"], p["layer"])],
        out_specs=out_specs,
        out_shape=out_shape,
        compiler_params=_cparams(1),
        name="sgu_layer_ffn",
    )(x, p["g1"], mods[0], mods[1], mods[2], p["w_in"], p["g_v"], p["w_s"], p["b_s"], p["w_o"],
      p["g2"], mods[3], mods[4], mods[5], p["w1"], p["w2"])


def _rope_tables(pos):
    half = QK_ROPE_DIM // 2
    inv = 1.0 / (ROPE_THETA ** (jnp.arange(half, dtype=F32) / half))
    ang = pos.astype(F32)[:, None] * inv[None, :]
    cos, sin = jnp.cos(ang), jnp.sin(ang)
    z = jnp.zeros((pos.shape[0], LANES - QK_ROPE_DIM), F32)
    return jnp.concatenate([cos, cos, z], axis=-1), jnp.concatenate([-sin, sin, z], axis=-1)


def _pad_lanes(g, n):
    return jnp.pad(g, (0, n - g.shape[0])).reshape(1, n)


def _prep_mla(a, norm1_g, norm2_g, ffn_w1, ffn_w2, layer, mla_w_in, mla_g_qa, mla_g_kva, mla_w_q_up,
              mla_w_uk, mla_w_uv, mla_g_qn, mla_g_qr, mla_g_kn, mla_g_kr, mla_w_o):
    d = mla_w_in.shape[1]
    w_in = jnp.pad(mla_w_in[a], ((0, 0), (0, LANES - QK_ROPE_DIM))).astype(BF16)
    half = QK_ROPE_DIM // 2
    w_q_up = mla_w_q_up[a].reshape(Q_LORA_RANK, N_HEADS, QK_NOPE_DIM + QK_ROPE_DIM)
    w_r1 = w_q_up[:, :, QK_NOPE_DIM:QK_NOPE_DIM + half]
    w_r2 = w_q_up[:, :, QK_NOPE_DIM + half:]
    w_z = jnp.zeros((Q_LORA_RANK, N_HEADS, LANES - QK_ROPE_DIM), w_q_up.dtype)
    w_q_up = jnp.concatenate([w_q_up[:, :, :QK_NOPE_DIM], w_r1, w_r2, w_z, w_r2, w_r1, w_z], axis=-1)
    g_qr = mla_g_qr[a]
    g_qr = jnp.concatenate([_pad_lanes(g_qr, LANES),
                            _pad_lanes(jnp.concatenate([g_qr[half:], g_qr[:half]]), LANES)], axis=0)
    return {
        "g1": norm1_g[layer].reshape(1, d), "g2": norm2_g[layer].reshape(1, d),
        "w1": ffn_w1, "w2": ffn_w2, "layer": layer,
        "w_in": w_in,
        "g_qa": mla_g_qa[a].reshape(1, -1), "g_kva": mla_g_kva[a].reshape(1, -1),
        "g_kr": _pad_lanes(mla_g_kr[a], LANES),
        "w_q_up": w_q_up.reshape(Q_LORA_RANK, N_HEADS * Q_HEAD_COLS).astype(BF16),
        "g_qn": mla_g_qn[a].reshape(1, -1), "g_qr": g_qr,
        "w_uk": mla_w_uk[a].reshape(KV_LORA_RANK, N_HEADS * QK_NOPE_DIM).astype(BF16),
        "g_kn": mla_g_kn[a].reshape(1, -1),
        "w_uv": jnp.swapaxes(mla_w_uv[a], 0, 1).astype(BF16),
        "w_o": mla_w_o[a].astype(BF16),
    }


def _prep_sgu(b, norm1_g, norm2_g, ffn_w1, ffn_w2, layer, cm_w_in, cm_g_v, cm_w_s, cm_b_s, cm_w_o, idx):
    d = cm_w_in.shape[1]
    period = len(idx)
    lo = int(idx[0])
    assert SGU_CHUNK % period == 0 and np.array_equal(idx, lo + np.arange(period)), idx
    reps = SGU_CHUNK // period
    w_s = jnp.tile(cm_w_s[b][:, lo:lo + period, lo:lo + period], (1, reps, reps))
    b_s = jnp.tile(cm_b_s[b][:, lo:lo + period], (1, reps))
    gdim = cm_w_o.shape[1] // SGU_GROUPS
    return {
        "g1": norm1_g[layer].reshape(1, d), "g2": norm2_g[layer].reshape(1, d),
        "w1": ffn_w1, "w2": ffn_w2, "layer": layer,
        "w_in": cm_w_in[b].astype(BF16), "g_v": cm_g_v[b].reshape(1, -1),
        "w_s": w_s, "b_s": jnp.broadcast_to(b_s[:, :, None], (SGU_GROUPS, SGU_CHUNK, gdim)),
        "w_o": cm_w_o[b].astype(BF16),
    }, period, lo


def kernel(x_prompt, x_sample, cache_ckv, cache_kpe, c_prompt, c_sample, ada_w, ada_b, norm1_g, norm2_g,
           ffn_w1, ffn_w2, mla_w_in, mla_g_qa, mla_g_kva, mla_w_q_up, mla_w_uk, mla_w_uv, mla_g_qn,
           mla_g_qr, mla_g_kn, mla_g_kr, mla_w_o, cm_w_in, cm_g_v, cm_w_s, cm_b_s, cm_w_o):
    batch, seq, d = x_prompt.shape
    dec_batch, dec_seq, _ = x_sample.shape
    past = cache_ckv.shape[2]
    depth = ada_w.shape[0]
    n_p, n_s = batch * seq, dec_batch * dec_seq
    tm_p, tm_proj, tm_s, attn_tile, attn_heads, ff_chunk = 512, 1024, n_s, 512, 4, 1024
    tiles_per_seq = seq // tm_p

    pos_p = np.arange(seq)
    pos_s = past + np.arange(dec_seq)
    rc_p, rs_p = _rope_tables(jnp.asarray(pos_p))
    rc_s, rs_s = _rope_tables(jnp.asarray(np.tile(pos_s, dec_batch)))
    rope_spec_p = pl.BlockSpec((tm_proj, LANES), lambda i: (i % (seq // tm_proj), 0))
    rope_spec_s = pl.BlockSpec((tm_s, LANES), lambda i: (i, 0))

    n_c = batch + dec_batch
    c_all = jnp.concatenate([c_prompt, c_sample], axis=0)
    c_all = jnp.pad(c_all, ((0, -n_c % BF16_ROWS), (0, 0)))
    mod = _ada_modulation(c_all, ada_w, ada_b)[:, :n_c]
    mod_spec_p = pl.BlockSpec((None, 1, d), lambda i: (i // tiles_per_seq, 0, 0))
    mod_spec_proj = pl.BlockSpec((None, 1, d), lambda i: (i // (seq // tm_proj), 0, 0))
    mod_spec_s = pl.BlockSpec((tm_s, d), lambda i: (i, 0))

    def split_mods(layer):
        m6 = mod[layer].reshape(batch + dec_batch, 6, d)
        m_p = [m6[:batch, j].reshape(batch, 1, d) for j in range(6)]
        m_s = [jnp.repeat(m6[batch:, j], dec_seq, axis=0) for j in range(6)]
        return m_p, m_s

    ffn_w1 = ffn_w1.astype(BF16)
    ffn_w2 = ffn_w2.astype(BF16)
    x_p = x_prompt.reshape(n_p, d)
    x_s = x_sample.reshape(n_s, d)
    ckv_p_l, kpe_p_l, ckv_s_l, kpe_s_l, v_s_l = [], [], [], [], []
    for layer in range(depth):
        m_p, m_s = split_mods(layer)
        if layer % 2 == 0:
            a = layer // 2
            p = _prep_mla(a, norm1_g, norm2_g, ffn_w1, ffn_w2, layer, mla_w_in, mla_g_qa, mla_g_kva,
                          mla_w_q_up, mla_w_uk, mla_w_uv, mla_g_qn, mla_g_qr, mla_g_kn, mla_g_kr, mla_w_o)
            q_p, k_p, ckv_p, _, ckvt_p, kr_p = _mla_project(x_p, m_p, mod_spec_proj, p, rc_p, rs_p, rope_spec_p,
                                                            tm_proj, attn_tile)
            q_s, k_s, ckv_s, ckvb_s, _, kr_s = _mla_project(x_s, m_s, mod_spec_s, p, rc_s, rs_s, rope_spec_s,
                                                            tm_s, attn_tile)
            ol_p = _prompt_attention(q_p, k_p, ckvt_p, batch, seq, attn_tile, attn_heads)
            ol_s = _sample_attention(q_s, k_s, ckvb_s, cache_ckv, cache_kpe, a, p["w_uk"].T, p["g_kn"],
                                     dec_seq, attn_tile)
            x_p = _mla_tail(x_p, ol_p, m_p, mod_spec_p, p, tm_p, ff_chunk, True)
            x_s = _mla_tail(x_s, ol_s, m_s, mod_spec_s, p, tm_s, ff_chunk, False)
            ckv_p_l.append(ckv_p.reshape(batch, seq, -1))
            kpe_p_l.append(kr_p.reshape(batch, seq, -1))
            ckv_s_l.append(ckv_s.reshape(dec_batch, dec_seq, -1))
            kpe_s_l.append(kr_s.reshape(dec_batch, dec_seq, -1))
        else:
            b = layer // 2
            args = (b, norm1_g, norm2_g, ffn_w1, ffn_w2, layer, cm_w_in, cm_g_v, cm_w_s, cm_b_s, cm_w_o)
            pp, period_p, off_p = _prep_sgu(*args, np.arange(SGU_CHUNK))
            ps, period_s, off_s = _prep_sgu(*args, pos_s % SGU_CHUNK)
            (x_p,) = _sgu_layer(x_p, m_p, mod_spec_p, pp, tm_p, ff_chunk, period_p, off_p, False)
            x_s, v_s = _sgu_layer(x_s, m_s, mod_spec_s, ps, tm_s, ff_chunk, period_s, off_s, True)
            v_s_l.append(v_s.reshape(dec_batch, dec_seq, -1))

    return (x_p.reshape(batch, seq, d), x_s.reshape(dec_batch, dec_seq, d),
            jnp.stack(ckv_p_l), jnp.stack(kpe_p_l), jnp.stack(ckv_s_l), jnp.stack(kpe_s_l),
            jnp.stack(v_s_l))
```

```python
import functools

import numpy as np
import jax
import jax.numpy as jnp
from jax import lax
from jax.experimental import pallas as pl
from jax.experimental.pallas import tpu as pltpu

F32 = jnp.float32
BF16 = jnp.bfloat16

CHUNK = 64
N_HEADS = 8
QK_NOPE_DIM = 128
QK_ROPE_DIM = 64
V_HEAD_DIM = 128
Q_LORA_RANK = 384
KV_LORA_RANK = 256
ROPE_THETA = 10000.0
SGU_CHUNK = 128
SGU_GROUPS = 8
EPS = 1e-6

LANES = 128
BF16_ROWS = 16
MXU_DIM = 256
HEAD_PAD = 2 * LANES
Q_HEAD_COLS = 3 * LANES
VMEM_LIMIT = 56 * 1024 * 1024

NEG_INF = float(np.finfo(np.float32).min)
Q_SCALE = float((QK_NOPE_DIM + QK_ROPE_DIM) ** -0.5 * np.log2(np.e))


def _cparams(n_axes):
    return pltpu.CompilerParams(
        dimension_semantics=("arbitrary",) * n_axes,
        vmem_limit_bytes=VMEM_LIMIT,
    )


def _const_spec(shape):
    nd = len(shape)
    return pl.BlockSpec(shape, lambda *_: (0,) * nd, pipeline_mode=pl.Buffered(1))


def _layer_spec(stacked, layer):
    return pl.BlockSpec((None,) + stacked.shape[1:], lambda *_: (layer, 0, 0), pipeline_mode=pl.Buffered(1))


def _dot(a, b):
    return jnp.dot(a, b, preferred_element_type=F32)


def _dot_nt(a, b):
    return lax.dot_general(a, b, (((1,), (1,)), ((), ())), preferred_element_type=F32)


def _rms(x, n):
    ms = jnp.sum(x * x, axis=-1, keepdims=True) / n
    return x * lax.rsqrt(ms + EPS)


def _rope128(x, c, s):
    half = QK_ROPE_DIM // 2
    rot = pltpu.roll(x, LANES - half, axis=1) + pltpu.roll(x, half, axis=1)
    return x * c + rot * s


def _mod_norm(x, g, shift, scale):
    return _rms(x, x.shape[-1]) * g * (1.0 + scale) + shift


def _sq_relu_ffn(hb, w1_ref, w2_ref, ff_chunk):
    d_ff = w1_ref.shape[1]
    acc = None
    for c in range(d_ff // ff_chunk):
        a = _dot(hb, w1_ref[:, c * ff_chunk:(c + 1) * ff_chunk])
        a = jnp.maximum(a, 0.0)
        a = (a * a).astype(BF16)
        part = _dot(a, w2_ref[c * ff_chunk:(c + 1) * ff_chunk, :])
        acc = part if acc is None else acc + part
    return acc


def _ada_kernel(c_ref, w_ref, b_ref, o_ref):
    s = jax.nn.silu(c_ref[...]).astype(BF16)
    o_ref[...] = _dot(s, w_ref[...].astype(BF16)) + b_ref[...]


def _ada_modulation(c_all, ada_w, ada_b):
    depth, d, n6 = ada_w.shape
    rows = c_all.shape[0]
    tn = 1536
    return pl.pallas_call(
        _ada_kernel,
        grid=(depth, n6 // tn),
        in_specs=[
            pl.BlockSpec((rows, d), lambda l, j: (0, 0)),
            pl.BlockSpec((None, d, tn), lambda l, j: (l, 0, j)),
            pl.BlockSpec((None, 1, tn), lambda l, j: (l, 0, j)),
        ],
        out_specs=pl.BlockSpec((None, rows, tn), lambda l, j: (l, 0, j)),
        out_shape=jax.ShapeDtypeStruct((depth, rows, n6), F32),
        compiler_params=_cparams(2),
        name="ada_modulation",
    )(c_all, ada_w, ada_b.reshape(depth, 1, n6))


def _mla_proj_kernel(x_ref, g1_ref, sh_ref, sc_ref, w_in_ref, g_qa_ref, g_kva_ref, g_kr_ref,
                     w_qup_ref, g_qn_ref, g_qr_ref, w_uk_ref, g_kn_ref, rc_ref, rs_ref,
                     q_ref, k_ref, ckv_ref, ckvb_ref, ckvt_ref, kr_ref):
    x = x_ref[...]
    h = _mod_norm(x, g1_ref[...], sh_ref[...], sc_ref[...]).astype(BF16)
    a = _dot(h, w_in_ref[...])
    cq = _rms(a[:, :Q_LORA_RANK], Q_LORA_RANK) * g_qa_ref[...]
    c0 = Q_LORA_RANK
    ckv = _rms(a[:, c0:c0 + KV_LORA_RANK], KV_LORA_RANK) * g_kva_ref[...]
    c1 = c0 + KV_LORA_RANK
    rc = rc_ref[...]
    rs = rs_ref[...]
    kr = _rope128(_rms(a[:, c1:c1 + LANES], QK_ROPE_DIM) * g_kr_ref[...], rc, rs)
    ckv_ref[...] = ckv
    ckv_b = ckv.astype(BF16)
    ckvb_ref[...] = ckv_b
    vt_tile = ckvt_ref.shape[-1]
    for j in range(ckvt_ref.shape[0]):
        ckvt_ref[j] = ckv[j * vt_tile:(j + 1) * vt_tile, :].T.astype(BF16)
    kr_ref[...] = kr[:, :QK_ROPE_DIM]
    kr_b = kr.astype(BF16)

    q = _dot(cq.astype(BF16), w_qup_ref[...])
    kn_all = _dot(ckv_b, w_uk_ref[...])
    g_qn = g_qn_ref[...] * Q_SCALE
    rot_c = rc * (g_qr_ref[0:1, :] * Q_SCALE)
    rot_s = rs * (g_qr_ref[1:2, :] * Q_SCALE)
    g_kn = g_kn_ref[...]
    for hd in range(N_HEADS):
        o = hd * Q_HEAD_COLS
        qn = _rms(q[:, o:o + QK_NOPE_DIM], QK_NOPE_DIM) * g_qn
        x = q[:, o + QK_NOPE_DIM:o + 2 * LANES]
        x_sw = q[:, o + 2 * LANES:o + 3 * LANES]
        inv = lax.rsqrt(jnp.sum(x * x, axis=-1, keepdims=True) / QK_ROPE_DIM + EPS)
        qr = (x * rot_c + x_sw * rot_s) * inv
        q_ref[hd] = jnp.concatenate([qn.astype(BF16), qr.astype(BF16)], axis=-1)
        kn = _rms(kn_all[:, hd * QK_NOPE_DIM:(hd + 1) * QK_NOPE_DIM], QK_NOPE_DIM) * g_kn
        k_ref[hd] = jnp.concatenate([kn.astype(BF16), kr_b], axis=-1)


def _mla_project(x, mods, mod_spec, p, rope_c, rope_s, rope_spec, tm, vt_tile):
    n, d = x.shape
    row = lambda w: pl.BlockSpec((tm, w), lambda i: (i, 0))
    hrow = pl.BlockSpec((N_HEADS, tm, HEAD_PAD), lambda i: (0, i, 0))
    consts = [p["w_in"], p["g_qa"], p["g_kva"], p["g_kr"], p["w_q_up"], p["g_qn"], p["g_qr"],
              p["w_uk"], p["g_kn"]]
    return pl.pallas_call(
        _mla_proj_kernel,
        grid=(n // tm,),
        in_specs=[row(d), _const_spec(p["g1"].shape), mod_spec, mod_spec]
        + [_const_spec(c.shape) for c in consts] + [rope_spec, rope_spec],
        out_specs=[hrow, hrow, row(KV_LORA_RANK), row(KV_LORA_RANK),
                   pl.BlockSpec((tm // vt_tile, KV_LORA_RANK, vt_tile), lambda i: (i, 0, 0)), row(QK_ROPE_DIM)],
        out_shape=[
            jax.ShapeDtypeStruct((N_HEADS, n, HEAD_PAD), BF16),
            jax.ShapeDtypeStruct((N_HEADS, n, HEAD_PAD), BF16),
            jax.ShapeDtypeStruct((n, KV_LORA_RANK), F32),
            jax.ShapeDtypeStruct((n, KV_LORA_RANK), BF16),
            jax.ShapeDtypeStruct((n // vt_tile, KV_LORA_RANK, vt_tile), BF16),
            jax.ShapeDtypeStruct((n, QK_ROPE_DIM), F32),
        ],
        compiler_params=_cparams(1),
        name="mla_project",
    )(x, p["g1"], mods[0], mods[1], *consts, rope_c, rope_s)


def _prompt_attn_kernel(q_ref, k_ref, vt_ref, o_ref, m_ref, l_ref, acc_ref, st_ref, *, tile, heads):
    qi = pl.program_id(2)
    m_ref[...] = jnp.full(m_ref.shape, NEG_INF, F32)
    l_ref[...] = jnp.zeros(l_ref.shape, F32)
    acc_ref[...] = jnp.zeros(acc_ref.shape, F32)

    first = 2

    def scores(kt, slot, gs=range(heads), qt=qi):
        k0 = kt * tile if isinstance(kt, int) else pl.multiple_of(kt * tile, tile)
        q0 = pl.multiple_of(qt * tile, tile)
        for g in gs:
            st_ref[slot, g] = _dot_nt(k_ref[g, pl.ds(k0, tile), :], q_ref[g, pl.ds(q0, tile), :])

    def step(kt, slot, masked, gs=range(heads)):
        vt = vt_ref[kt]
        for g in gs:
            for w in range(tile // MXU_DIM):
                nk = (w + 1) * MXU_DIM if masked else tile
                alphas, probs = [], []
                for j in range(w * MXU_DIM // LANES, (w + 1) * MXU_DIM // LANES):
                    cols = slice(j * LANES, (j + 1) * LANES)
                    col = st_ref[slot, g, :nk, cols]
                    if masked:
                        kc = lax.broadcasted_iota(jnp.int32, col.shape, 0) // CHUNK
                        qc = (lax.broadcasted_iota(jnp.int32, col.shape, 1) + j * LANES) // CHUNK
                        col = jnp.where(kc <= qc, col, NEG_INF)
                    m_prev = m_ref[g, :, cols]
                    m_new = jnp.maximum(m_prev, jnp.max(col, axis=0, keepdims=True))
                    alpha = jnp.exp2(m_prev - m_new)
                    pr = jnp.exp2(col - m_new)
                    l_ref[g, :, cols] = alpha * l_ref[g, :, cols] + jnp.sum(pr, axis=0, keepdims=True)
                    m_ref[g, :, cols] = m_new
                    probs.append(pr.astype(BF16))
                    alphas.append(alpha)
                wide = slice(w * MXU_DIM, (w + 1) * MXU_DIM)
                acc_ref[g, :, wide] = (acc_ref[g, :, wide] * jnp.concatenate(alphas, axis=-1)
                                       + _dot(vt[:, :nk], jnp.concatenate(probs, axis=-1)))

    nxt = jnp.minimum(qi + 1, pl.num_programs(2) - 1)

    def ahead(gs=range(heads)):
        scores(0, first, gs, qt=nxt)

    @pl.when(qi == 0)
    def _():
        scores(0, 1)
        for g in range(heads):
            ahead([g])
            step(0, 1, True, [g])

    @pl.when(qi > 0)
    def _():
        for g in range(heads):
            scores(1, 0, [g])
            step(0, first, False, [g])

        def body(i, carry):
            kt = 1 + 2 * i
            for g in range(heads):
                scores(kt + 1, 1, [g])
                step(kt, 0, False, [g])
            for g in range(heads):
                scores(kt + 2, 0, [g])
                step(kt + 1, 1, False, [g])
            return carry

        lax.fori_loop(0, (qi - 1) // 2, body, 0)

        @pl.when((qi - 1) % 2 == 0)
        def _():
            for g in range(heads):
                ahead([g])
                step(qi, 0, True, [g])

        @pl.when((qi - 1) % 2 == 1)
        def _():
            for g in range(heads):
                scores(qi, 1, [g])
                step(qi - 1, 0, False, [g])
            for g in range(heads):
                ahead([g])
                step(qi, 1, True, [g])

    for g in range(heads):
        o_ref[g * KV_LORA_RANK:(g + 1) * KV_LORA_RANK, :] = (acc_ref[g] / l_ref[g]).astype(o_ref.dtype)


def _prompt_attention(q_cat, k_cat, ckv_t, batch, seq, tile, heads):
    n = batch * seq
    nq = seq // tile
    assert ckv_t.shape == (n // tile, KV_LORA_RANK, tile), ckv_t.shape
    return pl.pallas_call(
        functools.partial(_prompt_attn_kernel, tile=tile, heads=heads),
        grid=(batch, N_HEADS // heads, nq),
        in_specs=[
            pl.BlockSpec((heads, seq, HEAD_PAD), lambda b, h, i: (h, b, 0)),
            pl.BlockSpec((heads, seq, HEAD_PAD), lambda b, h, i: (h, b, 0)),
            pl.BlockSpec((nq, KV_LORA_RANK, tile), lambda b, h, i: (b, 0, 0)),
        ],
        out_specs=pl.BlockSpec((heads * KV_LORA_RANK, tile), lambda b, h, i: (h, b * nq + i)),
        out_shape=jax.ShapeDtypeStruct((N_HEADS * KV_LORA_RANK, n), BF16),
        scratch_shapes=[
            pltpu.VMEM((heads, 1, tile), F32),
            pltpu.VMEM((heads, 1, tile), F32),
            pltpu.VMEM((heads, KV_LORA_RANK, tile), F32),
            pltpu.VMEM((3, heads, tile, tile), F32),
        ],
        compiler_params=_cparams(3),
        name="prompt_attention",
    )(q_cat, k_cat, ckv_t)


def _sample_attn_kernel(q_ref, kn_ref, ckvn_ref, cckv_ref, ckpe_ref, w_uk_ref, g_kn_ref, o_ref,
                        m_ref, l_ref, acc_ref, *, tile, past, n_new):
    m_ref[...] = jnp.full(m_ref.shape, NEG_INF, F32)
    l_ref[...] = jnp.zeros(l_ref.shape, F32)
    acc_ref[...] = jnp.zeros(acc_ref.shape, F32)
    g_kn = g_kn_ref[...]
    qr_all = jnp.concatenate([q_ref[hd][:, QK_NOPE_DIM:] for hd in range(N_HEADS)], axis=0)
    zero = jnp.zeros((n_new, QK_NOPE_DIM), BF16)
    q_bd = jnp.concatenate([
        jnp.concatenate([(q_ref[hd][:, :QK_NOPE_DIM] * g_kn).astype(BF16) if c == hd else zero
                         for c in range(N_HEADS)], axis=1)
        for hd in range(N_HEADS)], axis=0)

    def update(s, v):
        m_prev = m_ref[...]
        m_new = jnp.maximum(m_prev, jnp.max(s, axis=-1, keepdims=True))
        alpha = jnp.exp2(m_prev - m_new)
        pr = jnp.exp2(s - m_new)
        l_ref[...] = alpha * l_ref[...] + jnp.sum(pr, axis=-1, keepdims=True)
        acc_ref[...] = alpha * acc_ref[...] + _dot(pr.astype(BF16), v)
        m_ref[...] = m_new

    def up_project(kt):
        v = cckv_ref[kt * tile:(kt + 1) * tile, :].astype(BF16)
        return v, _dot_nt(w_uk_ref[...], v)

    def attend(kt, v, kn_t):
        kpe_t = ckpe_ref[:, kt * tile:(kt + 1) * tile].astype(BF16)
        kpe_t = jnp.concatenate([kpe_t, jnp.zeros_like(kpe_t)], axis=0)
        s_rope = _dot(qr_all, kpe_t)
        inv = []
        for hd in range(N_HEADS):
            blk = kn_t[hd * QK_NOPE_DIM:(hd + 1) * QK_NOPE_DIM, :]
            ms = jnp.sum(blk * blk, axis=0, keepdims=True) / QK_NOPE_DIM
            inv.append(jnp.broadcast_to(lax.rsqrt(ms + EPS), (n_new, tile)))
        s = _dot(q_bd, kn_t.astype(BF16)) * jnp.concatenate(inv, axis=0) + s_rope
        update(s, v)

    n_tiles = past // tile
    nxt = up_project(0)
    for kt in range(n_tiles):
        cur = nxt
        if kt + 1 < n_tiles:
            nxt = up_project(kt + 1)
        attend(kt, *cur)

    s_new = jnp.concatenate([_dot_nt(q_ref[hd], kn_ref[hd]) for hd in range(N_HEADS)], axis=0)
    q_pos = past + lax.broadcasted_iota(jnp.int32, s_new.shape, 0) % n_new
    k_pos = past + lax.broadcasted_iota(jnp.int32, s_new.shape, 1)
    s_new = jnp.where(k_pos // CHUNK <= q_pos // CHUNK, s_new, NEG_INF)
    update(s_new, ckvn_ref[...])

    o = (acc_ref[...] / l_ref[...]).astype(o_ref.dtype)
    for hd in range(N_HEADS):
        o_ref[:, hd * KV_LORA_RANK:(hd + 1) * KV_LORA_RANK] = o[hd * n_new:(hd + 1) * n_new, :]


def _sample_attention(q_cat, k_cat, ckv_b, cache_ckv, cache_kpe, a, w_uk, g_kn, n_new, tile):
    _, dec_batch, past, _ = cache_ckv.shape
    n = dec_batch * n_new
    rows = N_HEADS * n_new
    hrow = pl.BlockSpec((N_HEADS, n_new, HEAD_PAD), lambda b: (0, b, 0))
    cache_kpe_t = jnp.swapaxes(cache_kpe, 2, 3)
    return pl.pallas_call(
        functools.partial(_sample_attn_kernel, tile=tile, past=past, n_new=n_new),
        grid=(dec_batch,),
        in_specs=[
            hrow, hrow,
            pl.BlockSpec((n_new, KV_LORA_RANK), lambda b: (b, 0)),
            pl.BlockSpec((None, None, past, KV_LORA_RANK), lambda b: (a, b, 0, 0)),
            pl.BlockSpec((None, None, QK_ROPE_DIM, past), lambda b: (a, b, 0, 0)),
            _const_spec(w_uk.shape), _const_spec(g_kn.shape),
        ],
        out_specs=pl.BlockSpec((n_new, N_HEADS * KV_LORA_RANK), lambda b: (b, 0)),
        out_shape=jax.ShapeDtypeStruct((n, N_HEADS * KV_LORA_RANK), BF16),
        scratch_shapes=[
            pltpu.VMEM((rows, 1), F32),
            pltpu.VMEM((rows, 1), F32),
            pltpu.VMEM((rows, KV_LORA_RANK), F32),
        ],
        compiler_params=_cparams(1),
        name="sample_attention",
    )(q_cat, k_cat, ckv_b, cache_ckv, cache_kpe_t, w_uk, g_kn)


def _ffn_tail(x1, g2_ref, sh2_ref, sc2_ref, gt2_ref, w1_ref, w2_ref, ff_chunk):
    h2 = _mod_norm(x1, g2_ref[...], sh2_ref[...], sc2_ref[...]).astype(BF16)
    return x1 + gt2_ref[...] * _sq_relu_ffn(h2, w1_ref, w2_ref, ff_chunk)


def _mla_tail_kernel(x_ref, ol_ref, w_uv_ref, w_o_ref, gt1_ref, g2_ref, sh2_ref, sc2_ref, gt2_ref,
                     w1_ref, w2_ref, o_ref, *, ff_chunk, latent_major):
    heads = []
    for hd in range(N_HEADS):
        lat = slice(hd * KV_LORA_RANK, (hd + 1) * KV_LORA_RANK)
        if latent_major:
            up = lax.dot_general(ol_ref[lat, :], w_uv_ref[hd], (((0,), (0,)), ((), ())),
                                 preferred_element_type=F32)
        else:
            up = _dot(ol_ref[:, lat], w_uv_ref[hd])
        heads.append(up.astype(BF16))
    m = _dot(jnp.concatenate(heads, axis=-1), w_o_ref[...])
    x1 = x_ref[...] + gt1_ref[...] * m
    o_ref[...] = _ffn_tail(x1, g2_ref, sh2_ref, sc2_ref, gt2_ref, w1_ref, w2_ref, ff_chunk)


def _mla_tail(x, o_lat, mods, mod_spec, p, tm, ff_chunk, latent_major):
    n, d = x.shape
    row = lambda w: pl.BlockSpec((tm, w), lambda i: (i, 0))
    consts_a = [p["w_uv"], p["w_o"]]
    consts_b = [p["w1"], p["w2"]]
    if latent_major:
        ol_spec = pl.BlockSpec((o_lat.shape[0], tm), lambda i: (0, i))
    else:
        ol_spec = row(o_lat.shape[1])
    return pl.pallas_call(
        functools.partial(_mla_tail_kernel, ff_chunk=ff_chunk, latent_major=latent_major),
        grid=(n // tm,),
        in_specs=[row(d), ol_spec] + [_const_spec(c.shape) for c in consts_a]
        + [mod_spec, _const_spec(p["g2"].shape), mod_spec, mod_spec, mod_spec]
        + [_layer_spec(c, p["layer"]) for c in consts_b],
        out_specs=row(d),
        out_shape=jax.ShapeDtypeStruct((n, d), F32),
        compiler_params=_cparams(1),
        name="mla_tail_ffn",
    )(x, o_lat, *consts_a, mods[2], p["g2"], mods[3], mods[4], mods[5], *consts_b)


def _sgu_layer_kernel(x_ref, g1_ref, sh1_ref, sc1_ref, gt1_ref, w_in_ref, g_v_ref, w_s_ref, b_s_ref,
                      w_o_ref, g2_ref, sh2_ref, sc2_ref, gt2_ref, w1_ref, w2_ref, *out_refs,
                      ff_chunk, period, offset, emit_v):
    x = x_ref[...]
    tm = x.shape[0]
    width = w_o_ref.shape[0]
    gdim = width // SGU_GROUPS
    h = _mod_norm(x, g1_ref[...], sh1_ref[...], sc1_ref[...]).astype(BF16)
    z = jax.nn.gelu(_dot(h, w_in_ref[...]))
    u = z[:, :width]
    v = _rms(z[:, width:], width) * g_v_ref[...]
    if emit_v:
        out_refs[1][...] = v
    vb = v.astype(BF16)

    ri = lax.broadcasted_iota(jnp.int32, (SGU_CHUNK, SGU_CHUNK), 0)
    ci = lax.broadcasted_iota(jnp.int32, (SGU_CHUNK, SGU_CHUNK), 1)
    vis = (ri // period == ci // period) & ((ci % period + offset) // CHUNK <= (ri % period + offset) // CHUNK)
    w_mix = [jnp.where(vis, w_s_ref[g], 0.0).astype(BF16) for g in range(SGU_GROUPS)]

    rows = []
    for c in range(tm // SGU_CHUNK):
        r0 = c * SGU_CHUNK
        cols = []
        for g in range(SGU_GROUPS):
            l0 = g * gdim
            mixed = _dot(w_mix[g], vb[r0:r0 + SGU_CHUNK, l0:l0 + gdim]) + b_s_ref[g]
            cols.append((u[r0:r0 + SGU_CHUNK, l0:l0 + gdim] * mixed).astype(BF16))
        rows.append(jnp.concatenate(cols, axis=-1))
    gated = jnp.concatenate(rows, axis=0)
    x1 = x + gt1_ref[...] * _dot(gated, w_o_ref[...])
    out_refs[0][...] = _ffn_tail(x1, g2_ref, sh2_ref, sc2_ref, gt2_ref, w1_ref, w2_ref, ff_chunk)


def _sgu_layer(x, mods, mod_spec, p, tm, ff_chunk, period, offset, emit_v):
    n, d = x.shape
    width = p["w_o"].shape[0]
    row = lambda w: pl.BlockSpec((tm, w), lambda i: (i, 0))
    cs = lambda a: _const_spec(a.shape)
    out_specs = [row(d)]
    out_shape = [jax.ShapeDtypeStruct((n, d), F32)]
    if emit_v:
        out_specs.append(row(width))
        out_shape.append(jax.ShapeDtypeStruct((n, width), F32))
    return pl.pallas_call(
        functools.partial(_sgu_layer_kernel, ff_chunk=ff_chunk, period=period, offset=offset, emit_v=emit_v),
        grid=(n // tm,),
        in_specs=[row(d), cs(p["g1"]), mod_spec, mod_spec, mod_spec, cs(p["w_in"]), cs(p["g_v"]),
                  cs(p["w_s"]), cs(p["b_s"]), cs(p["w_o"]), cs(p["g2"]), mod_spec, mod_spec, mod_spec,
                  _layer_spec(p["w1"], p["layer"]), _layer_spec(p["w2"], p["layer"])],
        out_specs=out_specs,
        out_shape=out_shape,
        compiler_params=_cparams(1),
        name="sgu_layer_ffn",
    )(x, p["g1"], mods[0], mods[1], mods[2], p["w_in"], p["g_v"], p["w_s"], p["b_s"], p["w_o"],
      p["g2"], mods[3], mods[4], mods[5], p["w1"], p["w2"])


def _rope_tables(pos):
    half = QK_ROPE_DIM // 2
    inv = 1.0 / (ROPE_THETA ** (jnp.arange(half, dtype=F32) / half))
    ang = pos.astype(F32)[:, None] * inv[None, :]
    cos, sin = jnp.cos(ang), jnp.sin(ang)
    z = jnp.zeros((pos.shape[0], LANES - QK_ROPE_DIM), F32)
    return jnp.concatenate([cos, cos, z], axis=-1), jnp.concatenate([-sin, sin, z], axis=-1)


def _pad_lanes(g, n):
    return jnp.pad(g, (0, n - g.shape[0])).reshape(1, n)


def _prep_mla(a, norm1_g, norm2_g, ffn_w1, ffn_w2, layer, mla_w_in, mla_g_qa, mla_g_kva, mla_w_q_up,
              mla_w_uk, mla_w_uv, mla_g_qn, mla_g_qr, mla_g_kn, mla_g_kr, mla_w_o):
    d = mla_w_in.shape[1]
    w_in = jnp.pad(mla_w_in[a], ((0, 0), (0, LANES - QK_ROPE_DIM))).astype(BF16)
    half = QK_ROPE_DIM // 2
    w_q_up = mla_w_q_up[a].reshape(Q_LORA_RANK, N_HEADS, QK_NOPE_DIM + QK_ROPE_DIM)
    w_r1 = w_q_up[:, :, QK_NOPE_DIM:QK_NOPE_DIM + half]
    w_r2 = w_q_up[:, :, QK_NOPE_DIM + half:]
    w_z = jnp.zeros((Q_LORA_RANK, N_HEADS, LANES - QK_ROPE_DIM), w_q_up.dtype)
    w_q_up = jnp.concatenate([w_q_up[:, :, :QK_NOPE_DIM], w_r1, w_r2, w_z, w_r2, w_r1, w_z], axis=-1)
    g_qr = mla_g_qr[a]
    g_qr = jnp.concatenate([_pad_lanes(g_qr, LANES),
                            _pad_lanes(jnp.concatenate([g_qr[half:], g_qr[:half]]), LANES)], axis=0)
    return {
        "g1": norm1_g[layer].reshape(1, d), "g2": norm2_g[layer].reshape(1, d),
        "w1": ffn_w1, "w2": ffn_w2, "layer": layer,
        "w_in": w_in,
        "g_qa": mla_g_qa[a].reshape(1, -1), "g_kva": mla_g_kva[a].reshape(1, -1),
        "g_kr": _pad_lanes(mla_g_kr[a], LANES),
        "w_q_up": w_q_up.reshape(Q_LORA_RANK, N_HEADS * Q_HEAD_COLS).astype(BF16),
        "g_qn": mla_g_qn[a].reshape(1, -1), "g_qr": g_qr,
        "w_uk": mla_w_uk[a].reshape(KV_LORA_RANK, N_HEADS * QK_NOPE_DIM).astype(BF16),
        "g_kn": mla_g_kn[a].reshape(1, -1),
        "w_uv": jnp.swapaxes(mla_w_uv[a], 0, 1).astype(BF16),
        "w_o": mla_w_o[a].astype(BF16),
    }


def _prep_sgu(b, norm1_g, norm2_g, ffn_w1, ffn_w2, layer, cm_w_in, cm_g_v, cm_w_s, cm_b_s, cm_w_o, idx):
    d = cm_w_in.shape[1]
    period = len(idx)
    lo = int(idx[0])
    assert SGU_CHUNK % period == 0 and np.array_equal(idx, lo + np.arange(period)), idx
    reps = SGU_CHUNK // period
    w_s = jnp.tile(cm_w_s[b][:, lo:lo + period, lo:lo + period], (1, reps, reps))
    b_s = jnp.tile(cm_b_s[b][:, lo:lo + period], (1, reps))
    gdim = cm_w_o.shape[1] // SGU_GROUPS
    return {
        "g1": norm1_g[layer].reshape(1, d), "g2": norm2_g[layer].reshape(1, d),
        "w1": ffn_w1, "w2": ffn_w2, "layer": layer,
        "w_in": cm_w_in[b].astype(BF16), "g_v": cm_g_v[b].reshape(1, -1),
        "w_s": w_s, "b_s": jnp.broadcast_to(b_s[:, :, None], (SGU_GROUPS, SGU_CHUNK, gdim)),
        "w_o": cm_w_o[b].astype(BF16),
    }, period, lo


def kernel(x_prompt, x_sample, cache_ckv, cache_kpe, c_prompt, c_sample, ada_w, ada_b, norm1_g, norm2_g,
           ffn_w1, ffn_w2, mla_w_in, mla_g_qa, mla_g_kva, mla_w_q_up, mla_w_uk, mla_w_uv, mla_g_qn,
           mla_g_qr, mla_g_kn, mla_g_kr, mla_w_o, cm_w_in, cm_g_v, cm_w_s, cm_b_s, cm_w_o):
    batch, seq, d = x_prompt.shape
    dec_batch, dec_seq, _ = x_sample.shape
    past = cache_ckv.shape[2]
    depth = ada_w.shape[0]
    n_p, n_s = batch * seq, dec_batch * dec_seq
    tm_p, tm_proj, tm_s, attn_tile, attn_heads, ff_chunk = 512, 1024, n_s, 512, 4, 1024
    tiles_per_seq = seq // tm_p

    pos_p = np.arange(seq)
    pos_s = past + np.arange(dec_seq)
    rc_p, rs_p = _rope_tables(jnp.asarray(pos_p))
    rc_s, rs_s = _rope_tables(jnp.asarray(np.tile(pos_s, dec_batch)))
    rope_spec_p = pl.BlockSpec((tm_proj, LANES), lambda i: (i % (seq // tm_proj), 0))
    rope_spec_s = pl.BlockSpec((tm_s, LANES), lambda i: (i, 0))

    n_c = batch + dec_batch
    c_all = jnp.concatenate([c_prompt, c_sample], axis=0)
    c_all = jnp.pad(c_all, ((0, -n_c % BF16_ROWS), (0, 0)))
    mod = _ada_modulation(c_all, ada_w, ada_b)[:, :n_c]
    mod_spec_p = pl.BlockSpec((None, 1, d), lambda i: (i // tiles_per_seq, 0, 0))
    mod_spec_proj = pl.BlockSpec((None, 1, d), lambda i: (i // (seq // tm_proj), 0, 0))
    mod_spec_s = pl.BlockSpec((tm_s, d), lambda i: (i, 0))

    def split_mods(layer):
        m6 = mod[layer].reshape(batch + dec_batch, 6, d)
        m_p = [m6[:batch, j].reshape(batch, 1, d) for j in range(6)]
        m_s = [jnp.repeat(m6[batch:, j], dec_seq, axis=0) for j in range(6)]
        return m_p, m_s

    ffn_w1 = ffn_w1.astype(BF16)
    ffn_w2 = ffn_w2.astype(BF16)
    x_p = x_prompt.reshape(n_p, d)
    x_s = x_sample.reshape(n_s, d)
    ckv_p_l, kpe_p_l, ckv_s_l, kpe_s_l, v_s_l = [], [], [], [], []
    for layer in range(depth):
        m_p, m_s = split_mods(layer)
        if layer % 2 == 0:
            a = layer // 2
            p = _prep_mla(a, norm1_g, norm2_g, ffn_w1, ffn_w2, layer, mla_w_in, mla_g_qa, mla_g_kva,
                          mla_w_q_up, mla_w_uk, mla_w_uv, mla_g_qn, mla_g_qr, mla_g_kn, mla_g_kr, mla_w_o)
            q_p, k_p, ckv_p, _, ckvt_p, kr_p = _mla_project(x_p, m_p, mod_spec_proj, p, rc_p, rs_p, rope_spec_p,
                                                            tm_proj, attn_tile)
            q_s, k_s, ckv_s, ckvb_s, _, kr_s = _mla_project(x_s, m_s, mod_spec_s, p, rc_s, rs_s, rope_spec_s,
                                                            tm_s, attn_tile)
            ol_p = _prompt_attention(q_p, k_p, ckvt_p, batch, seq, attn_tile, attn_heads)
            ol_s = _sample_attention(q_s, k_s, ckvb_s, cache_ckv, cache_kpe, a, p["w_uk"].T, p["g_kn"],
                                     dec_seq, attn_tile)
            x_p = _mla_tail(x_p, ol_p, m_p, mod_spec_p, p, tm_p, ff_chunk, True)
            x_s = _mla_tail(x_s, ol_s, m_s, mod_spec_s, p, tm_s, ff_chunk, False)
            ckv_p_l.append(ckv_p.reshape(batch, seq, -1))
            kpe_p_l.append(kr_p.reshape(batch, seq, -1))
            ckv_s_l.append(ckv_s.reshape(dec_batch, dec_seq, -1))
            kpe_s_l.append(kr_s.reshape(dec_batch, dec_seq, -1))
        else:
            b = layer // 2
            args = (b, norm1_g, norm2_g, ffn_w1, ffn_w2, layer, cm_w_in, cm_g_v, cm_w_s, cm_b_s, cm_w_o)
            pp, period_p, off_p = _prep_sgu(*args, np.arange(SGU_CHUNK))
            ps, period_s, off_s = _prep_sgu(*args, pos_s % SGU_CHUNK)
            (x_p,) = _sgu_layer(x_p, m_p, mod_spec_p, pp, tm_p, ff_chunk, period_p, off_p, False)
            x_s, v_s = _sgu_layer(x_s, m_s, mod_spec_s, ps, tm_s, ff_chunk, period_s, off_s, True)
            v_s_l.append(v_s.reshape(dec_batch, dec_seq, -1))

    return (x_p.reshape(batch, seq, d), x_s.reshape(dec_batch, dec_seq, d),
            jnp.stack(ckv_p_l), jnp.stack(kpe_p_l), jnp.stack(ckv_s_l), jnp.stack(kpe_s_l),
            jnp.stack(v_s_l))
```

```python
import functools

import numpy as np
import jax
import jax.numpy as jnp
from jax import lax
from jax.experimental import pallas as pl
from jax.experimental.pallas import tpu as pltpu

F32 = jnp.float32
BF16 = jnp.bfloat16

CHUNK = 64
N_HEADS = 8
QK_NOPE_DIM = 128
QK_ROPE_DIM = 64
V_HEAD_DIM = 128
Q_LORA_RANK = 384
KV_LORA_RANK = 256
ROPE_THETA = 10000.0
SGU_CHUNK = 128
SGU_GROUPS = 8
EPS = 1e-6

LANES = 128
BF16_ROWS = 16
MXU_DIM = 256
HEAD_PAD = 2 * LANES
Q_HEAD_COLS = 3 * LANES
VMEM_LIMIT = 56 * 1024 * 1024

NEG_INF = float(np.finfo(np.float32).min)
Q_SCALE = float((QK_NOPE_DIM + QK_ROPE_DIM) ** -0.5 * np.log2(np.e))


def _cparams(n_axes):
    return pltpu.CompilerParams(
        dimension_semantics=("arbitrary",) * n_axes,
        vmem_limit_bytes=VMEM_LIMIT,
    )


def _const_spec(shape):
    nd = len(shape)
    return pl.BlockSpec(shape, lambda *_: (0,) * nd, pipeline_mode=pl.Buffered(1))


def _layer_spec(stacked, layer):
    return pl.BlockSpec((None,) + stacked.shape[1:], lambda *_: (layer, 0, 0), pipeline_mode=pl.Buffered(1))


def _dot(a, b):
    return jnp.dot(a, b, preferred_element_type=F32)


def _dot_nt(a, b):
    return lax.dot_general(a, b, (((1,), (1,)), ((), ())), preferred_element_type=F32)


def _rms(x, n):
    ms = jnp.sum(x * x, axis=-1, keepdims=True) / n
    return x * lax.rsqrt(ms + EPS)


def _rope128(x, c, s):
    half = QK_ROPE_DIM // 2
    rot = pltpu.roll(x, LANES - half, axis=1) + pltpu.roll(x, half, axis=1)
    return x * c + rot * s


def _mod_norm(x, g, shift, scale):
    return _rms(x, x.shape[-1]) * g * (1.0 + scale) + shift


def _sq_relu_ffn(hb, w1_ref, w2_ref, ff_chunk):
    d_ff = w1_ref.shape[1]
    acc = None
    for c in range(d_ff // ff_chunk):
        a = _dot(hb, w1_ref[:, c * ff_chunk:(c + 1) * ff_chunk])
        a = jnp.maximum(a, 0.0)
        a = (a * a).astype(BF16)
        part = _dot(a, w2_ref[c * ff_chunk:(c + 1) * ff_chunk, :])
        acc = part if acc is None else acc + part
    return acc


def _ada_kernel(c_ref, w_ref, b_ref, o_ref):
    s = jax.nn.silu(c_ref[...]).astype(BF16)
    o_ref[...] = _dot(s, w_ref[...].astype(BF16)) + b_ref[...]


def _ada_modulation(c_all, ada_w, ada_b):
    depth, d, n6 = ada_w.shape
    rows = c_all.shape[0]
    tn = 1536
    return pl.pallas_call(
        _ada_kernel,
        grid=(depth, n6 // tn),
        in_specs=[
            pl.BlockSpec((rows, d), lambda l, j: (0, 0)),
            pl.BlockSpec((None, d, tn), lambda l, j: (l, 0, j)),
            pl.BlockSpec((None, 1, tn), lambda l, j: (l, 0, j)),
        ],
        out_specs=pl.BlockSpec((None, rows, tn), lambda l, j: (l, 0, j)),
        out_shape=jax.ShapeDtypeStruct((depth, rows, n6), F32),
        compiler_params=_cparams(2),
        name="ada_modulation",
    )(c_all, ada_w, ada_b.reshape(depth, 1, n6))


def _mla_proj_kernel(x_ref, g1_ref, sh_ref, sc_ref, w_in_ref, g_qa_ref, g_kva_ref, g_kr_ref,
                     w_qup_ref, g_qn_ref, g_qr_ref, w_uk_ref, g_kn_ref, rc_ref, rs_ref,
                     q_ref, k_ref, ckv_ref, ckvb_ref, ckvt_ref, kr_ref):
    x = x_ref[...]
    h = _mod_norm(x, g1_ref[...], sh_ref[...], sc_ref[...]).astype(BF16)
    a = _dot(h, w_in_ref[...])
    cq = _rms(a[:, :Q_LORA_RANK], Q_LORA_RANK) * g_qa_ref[...]
    c0 = Q_LORA_RANK
    ckv = _rms(a[:, c0:c0 + KV_LORA_RANK], KV_LORA_RANK) * g_kva_ref[...]
    c1 = c0 + KV_LORA_RANK
    rc = rc_ref[...]
    rs = rs_ref[...]
    kr = _rope128(_rms(a[:, c1:c1 + LANES], QK_ROPE_DIM) * g_kr_ref[...], rc, rs)
    ckv_ref[...] = ckv
    ckv_b = ckv.astype(BF16)
    ckvb_ref[...] = ckv_b
    vt_tile = ckvt_ref.shape[-1]
    for j in range(ckvt_ref.shape[0]):
        ckvt_ref[j] = ckv[j * vt_tile:(j + 1) * vt_tile, :].T.astype(BF16)
    kr_ref[...] = kr[:, :QK_ROPE_DIM]
    kr_b = kr.astype(BF16)

    q = _dot(cq.astype(BF16), w_qup_ref[...])
    kn_all = _dot(ckv_b, w_uk_ref[...])
    g_qn = g_qn_ref[...] * Q_SCALE
    rot_c = rc * (g_qr_ref[0:1, :] * Q_SCALE)
    rot_s = rs * (g_qr_ref[1:2, :] * Q_SCALE)
    g_kn = g_kn_ref[...]
    for hd in range(N_HEADS):
        o = hd * Q_HEAD_COLS
        qn = _rms(q[:, o:o + QK_NOPE_DIM], QK_NOPE_DIM) * g_qn
        x = q[:, o + QK_NOPE_DIM:o + 2 * LANES]
        x_sw = q[:, o + 2 * LANES:o + 3 * LANES]
        inv = lax.rsqrt(jnp.sum(x * x, axis=-1, keepdims=True) / QK_ROPE_DIM + EPS)
        qr = (x * rot_c + x_sw * rot_s) * inv
        q_ref[hd] = jnp.concatenate([qn.astype(BF16), qr.astype(BF16)], axis=-1)
        kn = _rms(kn_all[:, hd * QK_NOPE_DIM:(hd + 1) * QK_NOPE_DIM], QK_NOPE_DIM) * g_kn
        k_ref[hd] = jnp.concatenate([kn.astype(BF16), kr_b], axis=-1)


def _mla_project(x, mods, mod_spec, p, rope_c, rope_s, rope_spec, tm, vt_tile):
    n, d = x.shape
    row = lambda w: pl.BlockSpec((tm, w), lambda i: (i, 0))
    hrow = pl.BlockSpec((N_HEADS, tm, HEAD_PAD), lambda i: (0, i, 0))
    consts = [p["w_in"], p["g_qa"], p["g_kva"], p["g_kr"], p["w_q_up"], p["g_qn"], p["g_qr"],
              p["w_uk"], p["g_kn"]]
    return pl.pallas_call(
        _mla_proj_kernel,
        grid=(n // tm,),
        in_specs=[row(d), _const_spec(p["g1"].shape), mod_spec, mod_spec]
        + [_const_spec(c.shape) for c in consts] + [rope_spec, rope_spec],
        out_specs=[hrow, hrow, row(KV_LORA_RANK), row(KV_LORA_RANK),
                   pl.BlockSpec((tm // vt_tile, KV_LORA_RANK, vt_tile), lambda i: (i, 0, 0)), row(QK_ROPE_DIM)],
        out_shape=[
            jax.ShapeDtypeStruct((N_HEADS, n, HEAD_PAD), BF16),
            jax.ShapeDtypeStruct((N_HEADS, n, HEAD_PAD), BF16),
            jax.ShapeDtypeStruct((n, KV_LORA_RANK), F32),
            jax.ShapeDtypeStruct((n, KV_LORA_RANK), BF16),
            jax.ShapeDtypeStruct((n // vt_tile, KV_LORA_RANK, vt_tile), BF16),
            jax.ShapeDtypeStruct((n, QK_ROPE_DIM), F32),
        ],
        compiler_params=_cparams(1),
        name="mla_project",
    )(x, p["g1"], mods[0], mods[1], *consts, rope_c, rope_s)


def _prompt_attn_kernel(q_ref, k_ref, vt_ref, o_ref, m_ref, l_ref, acc_ref, st_ref, *, tile, heads):
    qi = pl.program_id(2)
    m_ref[...] = jnp.full(m_ref.shape, NEG_INF, F32)
    l_ref[...] = jnp.zeros(l_ref.shape, F32)
    acc_ref[...] = jnp.zeros(acc_ref.shape, F32)

    first = 2

    def scores(kt, slot, gs=range(heads), qt=qi):
        k0 = kt * tile if isinstance(kt, int) else pl.multiple_of(kt * tile, tile)
        q0 = pl.multiple_of(qt * tile, tile)
        for g in gs:
            st_ref[slot, g] = _dot_nt(k_ref[g, pl.ds(k0, tile), :], q_ref[g, pl.ds(q0, tile), :])

    def step(kt, slot, masked, gs=range(heads)):
        vt = vt_ref[kt]
        for g in gs:
            for w in range(tile // MXU_DIM):
                nk = (w + 1) * MXU_DIM if masked else tile
                alphas, probs = [], []
                for j in range(w * MXU_DIM // LANES, (w + 1) * MXU_DIM // LANES):
                    cols = slice(j * LANES, (j + 1) * LANES)
                    col = st_ref[slot, g, :nk, cols]
                    if masked:
                        kc = lax.broadcasted_iota(jnp.int32, col.shape, 0) // CHUNK
                        qc = (lax.broadcasted_iota(jnp.int32, col.shape, 1) + j * LANES) // CHUNK
                        col = jnp.where(kc <= qc, col, NEG_INF)
                    m_prev = m_ref[g, :, cols]
                    m_new = jnp.maximum(m_prev, jnp.max(col, axis=0, keepdims=True))
                    alpha = jnp.exp2(m_prev - m_new)
                    pr = jnp.exp2(col - m_new)
                    l_ref[g, :, cols] = alpha * l_ref[g, :, cols] + jnp.sum(pr, axis=0, keepdims=True)
                    m_ref[g, :, cols] = m_new
                    probs.append(pr.astype(BF16))
                    alphas.append(alpha)
                wide = slice(w * MXU_DIM, (w + 1) * MXU_DIM)
                acc_ref[g, :, wide] = (acc_ref[g, :, wide] * jnp.concatenate(alphas, axis=-1)
                                       + _dot(vt[:, :nk], jnp.concatenate(probs, axis=-1)))

    nxt = jnp.minimum(qi + 1, pl.num_programs(2) - 1)

    def ahead(gs=range(heads)):
        scores(0, first, gs, qt=nxt)

    @pl.when(qi == 0)
    def _():
        scores(0, 1)
        for g in range(heads):
            ahead([g])
            step(0, 1, True, [g])

    @pl.when(qi > 0)
    def _():
        for g in range(heads):
            scores(1, 0, [g])
            step(0, first, False, [g])

        def body(i, carry):
            kt = 1 + 2 * i
            for g in range(heads):
                scores(kt + 1, 1, [g])
                step(kt, 0, False, [g])
            for g in range(heads):
                scores(kt + 2, 0, [g])
                step(kt + 1, 1, False, [g])
            return carry

        lax.fori_loop(0, (qi - 1) // 2, body, 0)

        @pl.when((qi - 1) % 2 == 0)
        def _():
            for g in range(heads):
                ahead([g])
                step(qi, 0, True, [g])

        @pl.when((qi - 1) % 2 == 1)
        def _():
            for g in range(heads):
                scores(qi, 1, [g])
                step(qi - 1, 0, False, [g])
            for g in range(heads):
                ahead([g])
                step(qi, 1, True, [g])

    for g in range(heads):
        o_ref[g * KV_LORA_RANK:(g + 1) * KV_LORA_RANK, :] = (acc_ref[g] / l_ref[g]).astype(o_ref.dtype)


def _prompt_attention(q_cat, k_cat, ckv_t, batch, seq, tile, heads):
    n = batch * seq
    nq = seq // tile
    assert ckv_t.shape == (n // tile, KV_LORA_RANK, tile), ckv_t.shape
    return pl.pallas_call(
        functools.partial(_prompt_attn_kernel, tile=tile, heads=heads),
        grid=(batch, N_HEADS // heads, nq),
        in_specs=[
            pl.BlockSpec((heads, seq, HEAD_PAD), lambda b, h, i: (h, b, 0)),
            pl.BlockSpec((heads, seq, HEAD_PAD), lambda b, h, i: (h, b, 0)),
            pl.BlockSpec((nq, KV_LORA_RANK, tile), lambda b, h, i: (b, 0, 0)),
        ],
        out_specs=pl.BlockSpec((heads * KV_LORA_RANK, tile), lambda b, h, i: (h, b * nq + i)),
        out_shape=jax.ShapeDtypeStruct((N_HEADS * KV_LORA_RANK, n), BF16),
        scratch_shapes=[
            pltpu.VMEM((heads, 1, tile), F32),
            pltpu.VMEM((heads, 1, tile), F32),
            pltpu.VMEM((heads, KV_LORA_RANK, tile), F32),
            pltpu.VMEM((3, heads, tile, tile), F32),
        ],
        compiler_params=_cparams(3),
        name="prompt_attention",
    )(q_cat, k_cat, ckv_t)


def _sample_attn_kernel(q_ref, kn_ref, ckvn_ref, cckv_ref, ckpe_ref, w_uk_ref, g_kn_ref, o_ref,
                        m_ref, l_ref, acc_ref, *, tile, past, n_new):
    m_ref[...] = jnp.full(m_ref.shape, NEG_INF, F32)
    l_ref[...] = jnp.zeros(l_ref.shape, F32)
    acc_ref[...] = jnp.zeros(acc_ref.shape, F32)
    g_kn = g_kn_ref[...]
    qr_all = jnp.concatenate([q_ref[hd][:, QK_NOPE_DIM:] for hd in range(N_HEADS)], axis=0)
    n_kn = N_HEADS * QK_NOPE_DIM
    q_lat = [_dot((q_ref[hd][:, :QK_NOPE_DIM] * g_kn).astype(BF16),
                  w_uk_ref[hd * QK_NOPE_DIM:(hd + 1) * QK_NOPE_DIM, :]).astype(BF16) for hd in range(N_HEADS)]
    w_and_q = jnp.concatenate([w_uk_ref[...]] + q_lat, axis=0)

    def update(s, v):
        m_prev = m_ref[...]
        m_new = jnp.maximum(m_prev, jnp.max(s, axis=-1, keepdims=True))
        alpha = jnp.exp2(m_prev - m_new)
        pr = jnp.exp2(s - m_new)
        l_ref[...] = alpha * l_ref[...] + jnp.sum(pr, axis=-1, keepdims=True)
        acc_ref[...] = alpha * acc_ref[...] + _dot(pr.astype(BF16), v)
        m_ref[...] = m_new

    def up_project(kt):
        v = cckv_ref[kt * tile:(kt + 1) * tile, :].astype(BF16)
        return v, _dot_nt(w_and_q, v)

    def attend(kt, v, kn_s):
        kn_t = kn_s[:n_kn, :]
        s_raw = kn_s[n_kn:, :]
        kpe_t = ckpe_ref[:, kt * tile:(kt + 1) * tile].astype(BF16)
        kpe_t = jnp.concatenate([kpe_t, jnp.zeros_like(kpe_t)], axis=0)
        s_rope = _dot(qr_all, kpe_t)
        inv = []
        for hd in range(N_HEADS):
            blk = kn_t[hd * QK_NOPE_DIM:(hd + 1) * QK_NOPE_DIM, :]
            ms = jnp.sum(blk * blk, axis=0, keepdims=True) / QK_NOPE_DIM
            inv.append(jnp.broadcast_to(lax.rsqrt(ms + EPS), (n_new, tile)))
        s = s_raw * jnp.concatenate(inv, axis=0) + s_rope
        update(s, v)

    n_tiles = past // tile
    nxt = up_project(0)
    for kt in range(n_tiles):
        cur = nxt
        if kt + 1 < n_tiles:
            nxt = up_project(kt + 1)
        attend(kt, *cur)

    s_new = jnp.concatenate([_dot_nt(q_ref[hd], kn_ref[hd]) for hd in range(N_HEADS)], axis=0)
    q_pos = past + lax.broadcasted_iota(jnp.int32, s_new.shape, 0) % n_new
    k_pos = past + lax.broadcasted_iota(jnp.int32, s_new.shape, 1)
    s_new = jnp.where(k_pos // CHUNK <= q_pos // CHUNK, s_new, NEG_INF)
    update(s_new, ckvn_ref[...])

    o = (acc_ref[...] / l_ref[...]).astype(o_ref.dtype)
    for hd in range(N_HEADS):
        o_ref[:, hd * KV_LORA_RANK:(hd + 1) * KV_LORA_RANK] = o[hd * n_new:(hd + 1) * n_new, :]


def _sample_attention(q_cat, k_cat, ckv_b, cache_ckv, cache_kpe, a, w_uk, g_kn, n_new, tile):
    _, dec_batch, past, _ = cache_ckv.shape
    n = dec_batch * n_new
    rows = N_HEADS * n_new
    hrow = pl.BlockSpec((N_HEADS, n_new, HEAD_PAD), lambda b: (0, b, 0))
    cache_kpe_t = jnp.swapaxes(cache_kpe, 2, 3)
    return pl.pallas_call(
        functools.partial(_sample_attn_kernel, tile=tile, past=past, n_new=n_new),
        grid=(dec_batch,),
        in_specs=[
            hrow, hrow,
            pl.BlockSpec((n_new, KV_LORA_RANK), lambda b: (b, 0)),
            pl.BlockSpec((None, None, past, KV_LORA_RANK), lambda b: (a, b, 0, 0)),
            pl.BlockSpec((None, None, QK_ROPE_DIM, past), lambda b: (a, b, 0, 0)),
            _const_spec(w_uk.shape), _const_spec(g_kn.shape),
        ],
        out_specs=pl.BlockSpec((n_new, N_HEADS * KV_LORA_RANK), lambda b: (b, 0)),
        out_shape=jax.ShapeDtypeStruct((n, N_HEADS * KV_LORA_RANK), BF16),
        scratch_shapes=[
            pltpu.VMEM((rows, 1), F32),
            pltpu.VMEM((rows, 1), F32),
            pltpu.VMEM((rows, KV_LORA_RANK), F32),
        ],
        compiler_params=_cparams(1),
        name="sample_attention",
    )(q_cat, k_cat, ckv_b, cache_ckv, cache_kpe_t, w_uk, g_kn)


def _ffn_tail(x1, g2_ref, sh2_ref, sc2_ref, gt2_ref, w1_ref, w2_ref, ff_chunk):
    h2 = _mod_norm(x1, g2_ref[...], sh2_ref[...], sc2_ref[...]).astype(BF16)
    return x1 + gt2_ref[...] * _sq_relu_ffn(h2, w1_ref, w2_ref, ff_chunk)


def _mla_tail_kernel(x_ref, ol_ref, w_uv_ref, w_o_ref, gt1_ref, g2_ref, sh2_ref, sc2_ref, gt2_ref,
                     w1_ref, w2_ref, o_ref, *, ff_chunk, latent_major):
    heads = []
    for hd in range(N_HEADS):
        lat = slice(hd * KV_LORA_RANK, (hd + 1) * KV_LORA_RANK)
        if latent_major:
            up = lax.dot_general(ol_ref[lat, :], w_uv_ref[hd], (((0,), (0,)), ((), ())),
                                 preferred_element_type=F32)
        else:
            up = _dot(ol_ref[:, lat], w_uv_ref[hd])
        heads.append(up.astype(BF16))
    m = _dot(jnp.concatenate(heads, axis=-1), w_o_ref[...])
    x1 = x_ref[...] + gt1_ref[...] * m
    o_ref[...] = _ffn_tail(x1, g2_ref, sh2_ref, sc2_ref, gt2_ref, w1_ref, w2_ref, ff_chunk)


def _mla_tail(x, o_lat, mods, mod_spec, p, tm, ff_chunk, latent_major):
    n, d = x.shape
    row = lambda w: pl.BlockSpec((tm, w), lambda i: (i, 0))
    consts_a = [p["w_uv"], p["w_o"]]
    consts_b = [p["w1"], p["w2"]]
    if latent_major:
        ol_spec = pl.BlockSpec((o_lat.shape[0], tm), lambda i: (0, i))
    else:
        ol_spec = row(o_lat.shape[1])
    return pl.pallas_call(
        functools.partial(_mla_tail_kernel, ff_chunk=ff_chunk, latent_major=latent_major),
        grid=(n // tm,),
        in_specs=[row(d), ol_spec] + [_const_spec(c.shape) for c in consts_a]
        + [mod_spec, _const_spec(p["g2"].shape), mod_spec, mod_spec, mod_spec]
        + [_layer_spec(c, p["layer"]) for c in consts_b],
        out_specs=row(d),
        out_shape=jax.ShapeDtypeStruct((n, d), F32),
        compiler_params=_cparams(1),
        name="mla_tail_ffn",
    )(x, o_lat, *consts_a, mods[2], p["g2"], mods[3], mods[4], mods[5], *consts_b)


def _sgu_layer_kernel(x_ref, g1_ref, sh1_ref, sc1_ref, gt1_ref, w_in_ref, g_v_ref, w_s_ref, b_s_ref,
                      w_o_ref, g2_ref, sh2_ref, sc2_ref, gt2_ref, w1_ref, w2_ref, *out_refs,
                      ff_chunk, period, offset, emit_v):
    x = x_ref[...]
    tm = x.shape[0]
    width = w_o_ref.shape[0]
    gdim = width // SGU_GROUPS
    h = _mod_norm(x, g1_ref[...], sh1_ref[...], sc1_ref[...]).astype(BF16)
    z = jax.nn.gelu(_dot(h, w_in_ref[...]))
    u = z[:, :width]
    v = _rms(z[:, width:], width) * g_v_ref[...]
    if emit_v:
        out_refs[1][...] = v
    vb = v.astype(BF16)

    ri = lax.broadcasted_iota(jnp.int32, (SGU_CHUNK, SGU_CHUNK), 0)
    ci = lax.broadcasted_iota(jnp.int32, (SGU_CHUNK, SGU_CHUNK), 1)
    vis = (ri // period == ci // period) & ((ci % period + offset) // CHUNK <= (ri % period + offset) // CHUNK)
    w_mix = [jnp.where(vis, w_s_ref[g], 0.0).astype(BF16) for g in range(SGU_GROUPS)]

    rows = []
    for c in range(tm // SGU_CHUNK):
        r0 = c * SGU_CHUNK
        cols = []
        for g in range(SGU_GROUPS):
            l0 = g * gdim
            mixed = _dot(w_mix[g], vb[r0:r0 + SGU_CHUNK, l0:l0 + gdim]) + b_s_ref[g]
            cols.append((u[r0:r0 + SGU_CHUNK, l0:l0 + gdim] * mixed).astype(BF16))
        rows.append(jnp.concatenate(cols, axis=-1))
    gated = jnp.concatenate(rows, axis=0)
    x1 = x + gt1_ref[...] * _dot(gated, w_o_ref[...])
    out_refs[0][...] = _ffn_tail(x1, g2_ref, sh2_ref, sc2_ref, gt2_ref, w1_ref, w2_ref, ff_chunk)


def _sgu_layer(x, mods, mod_spec, p, tm, ff_chunk, period, offset, emit_v):
    n, d = x.shape
    width = p["w_o"].shape[0]
    row = lambda w: pl.BlockSpec((tm, w), lambda i: (i, 0))
    cs = lambda a: _const_spec(a.shape)
    out_specs = [row(d)]
    out_shape = [jax.ShapeDtypeStruct((n, d), F32)]
    if emit_v:
        out_specs.append(row(width))
        out_shape.append(jax.ShapeDtypeStruct((n, width), F32))
    return pl.pallas_call(
        functools.partial(_sgu_layer_kernel, ff_chunk=ff_chunk, period=period, offset=offset, emit_v=emit_v),
        grid=(n // tm,),
        in_specs=[row(d), cs(p["g1"]), mod_spec, mod_spec, mod_spec, cs(p["w_in"]), cs(p["g_v"]),
                  cs(p["w_s"]), cs(p["b_s"]), cs(p["w_o"]), cs(p["g2"]), mod_spec, mod_spec, mod_spec,
                  _layer_spec(p["w1"], p["layer"]), _layer_spec(p["w2"], p["layer"])],
        out_specs=out_specs,
        out_shape=out_shape,
        compiler_params=_cparams(1),
        name="sgu_layer_ffn",
    )(x, p["g1"], mods[0], mods[1], mods[2], p["w_in"], p["g_v"], p["w_s"], p["b_s"], p["w_o"],
      p["g2"], mods[3], mods[4], mods[5], p["w1"], p["w2"])


def _rope_tables(pos):
    half = QK_ROPE_DIM // 2
    inv = 1.0 / (ROPE_THETA ** (jnp.arange(half, dtype=F32) / half))
    ang = pos.astype(F32)[:, None] * inv[None, :]
    cos, sin = jnp.cos(ang), jnp.sin(ang)
    z = jnp.zeros((pos.shape[0], LANES - QK_ROPE_DIM), F32)
    return jnp.concatenate([cos, cos, z], axis=-1), jnp.concatenate([-sin, sin, z], axis=-1)


def _pad_lanes(g, n):
    return jnp.pad(g, (0, n - g.shape[0])).reshape(1, n)


def _prep_mla(a, norm1_g, norm2_g, ffn_w1, ffn_w2, layer, mla_w_in, mla_g_qa, mla_g_kva, mla_w_q_up,
              mla_w_uk, mla_w_uv, mla_g_qn, mla_g_qr, mla_g_kn, mla_g_kr, mla_w_o):
    d = mla_w_in.shape[1]
    w_in = jnp.pad(mla_w_in[a], ((0, 0), (0, LANES - QK_ROPE_DIM))).astype(BF16)
    half = QK_ROPE_DIM // 2
    w_q_up = mla_w_q_up[a].reshape(Q_LORA_RANK, N_HEADS, QK_NOPE_DIM + QK_ROPE_DIM)
    w_r1 = w_q_up[:, :, QK_NOPE_DIM:QK_NOPE_DIM + half]
    w_r2 = w_q_up[:, :, QK_NOPE_DIM + half:]
    w_z = jnp.zeros((Q_LORA_RANK, N_HEADS, LANES - QK_ROPE_DIM), w_q_up.dtype)
    w_q_up = jnp.concatenate([w_q_up[:, :, :QK_NOPE_DIM], w_r1, w_r2, w_z, w_r2, w_r1, w_z], axis=-1)
    g_qr = mla_g_qr[a]
    g_qr = jnp.concatenate([_pad_lanes(g_qr, LANES),
                            _pad_lanes(jnp.concatenate([g_qr[half:], g_qr[:half]]), LANES)], axis=0)
    return {
        "g1": norm1_g[layer].reshape(1, d), "g2": norm2_g[layer].reshape(1, d),
        "w1": ffn_w1, "w2": ffn_w2, "layer": layer,
        "w_in": w_in,
        "g_qa": mla_g_qa[a].reshape(1, -1), "g_kva": mla_g_kva[a].reshape(1, -1),
        "g_kr": _pad_lanes(mla_g_kr[a], LANES),
        "w_q_up": w_q_up.reshape(Q_LORA_RANK, N_HEADS * Q_HEAD_COLS).astype(BF16),
        "g_qn": mla_g_qn[a].reshape(1, -1), "g_qr": g_qr,
        "w_uk": mla_w_uk[a].reshape(KV_LORA_RANK, N_HEADS * QK_NOPE_DIM).astype(BF16),
        "g_kn": mla_g_kn[a].reshape(1, -1),
        "w_uv": jnp.swapaxes(mla_w_uv[a], 0, 1).astype(BF16),
        "w_o": mla_w_o[a].astype(BF16),
    }


def _prep_sgu(b, norm1_g, norm2_g, ffn_w1, ffn_w2, layer, cm_w_in, cm_g_v, cm_w_s, cm_b_s, cm_w_o, idx):
    d = cm_w_in.shape[1]
    period = len(idx)
    lo = int(idx[0])
    assert SGU_CHUNK % period == 0 and np.array_equal(idx, lo + np.arange(period)), idx
    reps = SGU_CHUNK // period
    w_s = jnp.tile(cm_w_s[b][:, lo:lo + period, lo:lo + period], (1, reps, reps))
    b_s = jnp.tile(cm_b_s[b][:, lo:lo + period], (1, reps))
    gdim = cm_w_o.shape[1] // SGU_GROUPS
    return {
        "g1": norm1_g[layer].reshape(1, d), "g2": norm2_g[layer].reshape(1, d),
        "w1": ffn_w1, "w2": ffn_w2, "layer": layer,
        "w_in": cm_w_in[b].astype(BF16), "g_v": cm_g_v[b].reshape(1, -1),
        "w_s": w_s, "b_s": jnp.broadcast_to(b_s[:, :, None], (SGU_GROUPS, SGU_CHUNK, gdim)),
        "w_o": cm_w_o[b].astype(BF16),
    }, period, lo


def kernel(x_prompt, x_sample, cache_ckv, cache_kpe, c_prompt, c_sample, ada_w, ada_b, norm1_g, norm2_g,
           ffn_w1, ffn_w2, mla_w_in, mla_g_qa, mla_g_kva, mla_w_q_up, mla_w_uk, mla_w_uv, mla_g_qn,
           mla_g_qr, mla_g_kn, mla_g_kr, mla_w_o, cm_w_in, cm_g_v, cm_w_s, cm_b_s, cm_w_o):
    batch, seq, d = x_prompt.shape
    dec_batch, dec_seq, _ = x_sample.shape
    past = cache_ckv.shape[2]
    depth = ada_w.shape[0]
    n_p, n_s = batch * seq, dec_batch * dec_seq
    tm_p, tm_proj, tm_s, attn_tile, attn_heads, ff_chunk = 512, 1024, n_s, 512, 4, 1024
    tiles_per_seq = seq // tm_p

    pos_p = np.arange(seq)
    pos_s = past + np.arange(dec_seq)
    rc_p, rs_p = _rope_tables(jnp.asarray(pos_p))
    rc_s, rs_s = _rope_tables(jnp.asarray(np.tile(pos_s, dec_batch)))
    rope_spec_p = pl.BlockSpec((tm_proj, LANES), lambda i: (i % (seq // tm_proj), 0))
    rope_spec_s = pl.BlockSpec((tm_s, LANES), lambda i: (i, 0))

    n_c = batch + dec_batch
    c_all = jnp.concatenate([c_prompt, c_sample], axis=0)
    c_all = jnp.pad(c_all, ((0, -n_c % BF16_ROWS), (0, 0)))
    mod = _ada_modulation(c_all, ada_w, ada_b)[:, :n_c]
    mod_spec_p = pl.BlockSpec((None, 1, d), lambda i: (i // tiles_per_seq, 0, 0))
    mod_spec_proj = pl.BlockSpec((None, 1, d), lambda i: (i // (seq // tm_proj), 0, 0))
    mod_spec_s = pl.BlockSpec((tm_s, d), lambda i: (i, 0))

    def split_mods(layer):
        m6 = mod[layer].reshape(batch + dec_batch, 6, d)
        m_p = [m6[:batch, j].reshape(batch, 1, d) for j in range(6)]
        m_s = [jnp.repeat(m6[batch:, j], dec_seq, axis=0) for j in range(6)]
        return m_p, m_s

    ffn_w1 = ffn_w1.astype(BF16)
    ffn_w2 = ffn_w2.astype(BF16)
    x_p = x_prompt.reshape(n_p, d)
    x_s = x_sample.reshape(n_s, d)
    ckv_p_l, kpe_p_l, ckv_s_l, kpe_s_l, v_s_l = [], [], [], [], []
    for layer in range(depth):
        m_p, m_s = split_mods(layer)
        if layer % 2 == 0:
            a = layer // 2
            p = _prep_mla(a, norm1_g, norm2_g, ffn_w1, ffn_w2, layer, mla_w_in, mla_g_qa, mla_g_kva,
                          mla_w_q_up, mla_w_uk, mla_w_uv, mla_g_qn, mla_g_qr, mla_g_kn, mla_g_kr, mla_w_o)
            q_p, k_p, ckv_p, _, ckvt_p, kr_p = _mla_project(x_p, m_p, mod_spec_proj, p, rc_p, rs_p, rope_spec_p,
                                                            tm_proj, attn_tile)
            q_s, k_s, ckv_s, ckvb_s, _, kr_s = _mla_project(x_s, m_s, mod_spec_s, p, rc_s, rs_s, rope_spec_s,
                                                            tm_s, attn_tile)
            ol_p = _prompt_attention(q_p, k_p, ckvt_p, batch, seq, attn_tile, attn_heads)
            ol_s = _sample_attention(q_s, k_s, ckvb_s, cache_ckv, cache_kpe, a, p["w_uk"].T, p["g_kn"],
                                     dec_seq, attn_tile)
            x_p = _mla_tail(x_p, ol_p, m_p, mod_spec_p, p, tm_p, ff_chunk, True)
            x_s = _mla_tail(x_s, ol_s, m_s, mod_spec_s, p, tm_s, ff_chunk, False)
            ckv_p_l.append(ckv_p.reshape(batch, seq, -1))
            kpe_p_l.append(kr_p.reshape(batch, seq, -1))
            ckv_s_l.append(ckv_s.reshape(dec_batch, dec_seq, -1))
            kpe_s_l.append(kr_s.reshape(dec_batch, dec_seq, -1))
        else:
            b = layer // 2
            args = (b, norm1_g, norm2_g, ffn_w1, ffn_w2, layer, cm_w_in, cm_g_v, cm_w_s, cm_b_s, cm_w_o)
            pp, period_p, off_p = _prep_sgu(*args, np.arange(SGU_CHUNK))
            ps, period_s, off_s = _prep_sgu(*args, pos_s % SGU_CHUNK)
            (x_p,) = _sgu_layer(x_p, m_p, mod_spec_p, pp, tm_p, ff_chunk, period_p, off_p, False)
            x_s, v_s = _sgu_layer(x_s, m_s, mod_spec_s, ps, tm_s, ff_chunk, period_s, off_s, True)
            v_s_l.append(v_s.reshape(dec_batch, dec_seq, -1))

    return (x_p.reshape(batch, seq, d), x_s.reshape(dec_batch, dec_seq, d),
            jnp.stack(ckv_p_l), jnp.stack(kpe_p_l), jnp.stack(ckv_s_l), jnp.stack(kpe_s_l),
            jnp.stack(v_s_l))
```

```python
import functools

import numpy as np
import jax
import jax.numpy as jnp
from jax import lax
from jax.experimental import pallas as pl
from jax.experimental.pallas import tpu as pltpu

F32 = jnp.float32
BF16 = jnp.bfloat16

CHUNK = 64
N_HEADS = 8
QK_NOPE_DIM = 128
QK_ROPE_DIM = 64
V_HEAD_DIM = 128
Q_LORA_RANK = 384
KV_LORA_RANK = 256
ROPE_THETA = 10000.0
SGU_CHUNK = 128
SGU_GROUPS = 8
EPS = 1e-6

LANES = 128
BF16_ROWS = 16
MXU_DIM = 256
HEAD_PAD = 2 * LANES
Q_HEAD_COLS = 3 * LANES
VMEM_LIMIT = 56 * 1024 * 1024

NEG_INF = float(np.finfo(np.float32).min)
Q_SCALE = float((QK_NOPE_DIM + QK_ROPE_DIM) ** -0.5 * np.log2(np.e))


def _cparams(n_axes):
    return pltpu.CompilerParams(
        dimension_semantics=("arbitrary",) * n_axes,
        vmem_limit_bytes=VMEM_LIMIT,
    )


def _const_spec(shape):
    nd = len(shape)
    return pl.BlockSpec(shape, lambda *_: (0,) * nd, pipeline_mode=pl.Buffered(1))


def _layer_spec(stacked, layer):
    return pl.BlockSpec((None,) + stacked.shape[1:], lambda *_: (layer, 0, 0), pipeline_mode=pl.Buffered(1))


def _dot(a, b):
    return jnp.dot(a, b, preferred_element_type=F32)


def _dot_nt(a, b):
    return lax.dot_general(a, b, (((1,), (1,)), ((), ())), preferred_element_type=F32)


def _rms(x, n):
    ms = jnp.sum(x * x, axis=-1, keepdims=True) / n
    return x * lax.rsqrt(ms + EPS)


def _rope128(x, c, s):
    half = QK_ROPE_DIM // 2
    rot = pltpu.roll(x, LANES - half, axis=1) + pltpu.roll(x, half, axis=1)
    return x * c + rot * s


def _mod_norm(x, g, shift, scale):
    return _rms(x, x.shape[-1]) * g * (1.0 + scale) + shift


def _sq_relu_ffn(hb, w1_ref, w2_ref, ff_chunk):
    d_ff = w1_ref.shape[1]
    acc = None
    for c in range(d_ff // ff_chunk):
        a = _dot(hb, w1_ref[:, c * ff_chunk:(c + 1) * ff_chunk])
        a = jnp.maximum(a, 0.0)
        a = (a * a).astype(BF16)
        part = _dot(a, w2_ref[c * ff_chunk:(c + 1) * ff_chunk, :])
        acc = part if acc is None else acc + part
    return acc


def _ada_kernel(c_ref, w_ref, b_ref, o_ref):
    s = jax.nn.silu(c_ref[...]).astype(BF16)
    o_ref[...] = _dot(s, w_ref[...].astype(BF16)) + b_ref[...]


def _ada_modulation(c_all, ada_w, ada_b):
    depth, d, n6 = ada_w.shape
    rows = c_all.shape[0]
    tn = 1536
    return pl.pallas_call(
        _ada_kernel,
        grid=(depth, n6 // tn),
        in_specs=[
            pl.BlockSpec((rows, d), lambda l, j: (0, 0)),
            pl.BlockSpec((None, d, tn), lambda l, j: (l, 0, j)),
            pl.BlockSpec((None, 1, tn), lambda l, j: (l, 0, j)),
        ],
        out_specs=pl.BlockSpec((None, rows, tn), lambda l, j: (l, 0, j)),
        out_shape=jax.ShapeDtypeStruct((depth, rows, n6), F32),
        compiler_params=_cparams(2),
        name="ada_modulation",
    )(c_all, ada_w, ada_b.reshape(depth, 1, n6))


def _mla_proj_kernel(x_ref, g1_ref, sh_ref, sc_ref, w_in_ref, g_qa_ref, g_kva_ref, g_kr_ref,
                     w_qup_ref, g_qn_ref, g_qr_ref, w_uk_ref, g_kn_ref, rc_ref, rs_ref,
                     q_ref, k_ref, ckv_ref, ckvb_ref, ckvt_ref, kr_ref):
    x = x_ref[...]
    h = _mod_norm(x, g1_ref[...], sh_ref[...], sc_ref[...]).astype(BF16)
    c0 = KV_LORA_RANK
    ckv = _rms(_dot(h, w_in_ref[:, :c0]), KV_LORA_RANK) * g_kva_ref[...]
    a = _dot(h, w_in_ref[:, c0:])
    cq = _rms(a[:, LANES:], Q_LORA_RANK) * g_qa_ref[...]
    rc = rc_ref[...]
    rs = rs_ref[...]
    kr = _rope128(_rms(a[:, :LANES], QK_ROPE_DIM) * g_kr_ref[...], rc, rs)
    ckv_ref[...] = ckv
    ckv_b = ckv.astype(BF16)
    ckvb_ref[...] = ckv_b
    vt_tile = ckvt_ref.shape[-1]
    for j in range(ckvt_ref.shape[0]):
        ckvt_ref[j] = ckv[j * vt_tile:(j + 1) * vt_tile, :].T.astype(BF16)
    kr_ref[...] = kr[:, :QK_ROPE_DIM]
    kr_b = kr.astype(BF16)

    kn_all = _dot(ckv_b, w_uk_ref[...])
    q = _dot(cq.astype(BF16), w_qup_ref[...])
    g_qn = g_qn_ref[...] * Q_SCALE
    rot_c = rc * (g_qr_ref[0:1, :] * Q_SCALE)
    rot_s = rs * (g_qr_ref[1:2, :] * Q_SCALE)
    g_kn = g_kn_ref[...]
    for hd in range(N_HEADS):
        kn = _rms(kn_all[:, hd * QK_NOPE_DIM:(hd + 1) * QK_NOPE_DIM], QK_NOPE_DIM) * g_kn
        k_ref[hd] = jnp.concatenate([kn.astype(BF16), kr_b], axis=-1)
    for hd in range(N_HEADS):
        o = hd * Q_HEAD_COLS
        qn = _rms(q[:, o:o + QK_NOPE_DIM], QK_NOPE_DIM) * g_qn
        x = q[:, o + QK_NOPE_DIM:o + 2 * LANES]
        x_sw = q[:, o + 2 * LANES:o + 3 * LANES]
        inv = lax.rsqrt(jnp.sum(x * x, axis=-1, keepdims=True) / QK_ROPE_DIM + EPS)
        qr = (x * rot_c + x_sw * rot_s) * inv
        q_ref[hd] = jnp.concatenate([qn.astype(BF16), qr.astype(BF16)], axis=-1)


def _mla_project(x, mods, mod_spec, p, rope_c, rope_s, rope_spec, tm, vt_tile):
    n, d = x.shape
    row = lambda w: pl.BlockSpec((tm, w), lambda i: (i, 0))
    hrow = pl.BlockSpec((N_HEADS, tm, HEAD_PAD), lambda i: (0, i, 0))
    consts = [p["w_in"], p["g_qa"], p["g_kva"], p["g_kr"], p["w_q_up"], p["g_qn"], p["g_qr"],
              p["w_uk"], p["g_kn"]]
    return pl.pallas_call(
        _mla_proj_kernel,
        grid=(n // tm,),
        in_specs=[row(d), _const_spec(p["g1"].shape), mod_spec(0), mod_spec(1)]
        + [_const_spec(c.shape) for c in consts] + [rope_spec, rope_spec],
        out_specs=[hrow, hrow, row(KV_LORA_RANK), row(KV_LORA_RANK),
                   pl.BlockSpec((tm // vt_tile, KV_LORA_RANK, vt_tile), lambda i: (i, 0, 0)), row(QK_ROPE_DIM)],
        out_shape=[
            jax.ShapeDtypeStruct((N_HEADS, n, HEAD_PAD), BF16),
            jax.ShapeDtypeStruct((N_HEADS, n, HEAD_PAD), BF16),
            jax.ShapeDtypeStruct((n, KV_LORA_RANK), F32),
            jax.ShapeDtypeStruct((n, KV_LORA_RANK), BF16),
            jax.ShapeDtypeStruct((n // vt_tile, KV_LORA_RANK, vt_tile), BF16),
            jax.ShapeDtypeStruct((n, QK_ROPE_DIM), F32),
        ],
        compiler_params=_cparams(1),
        name="mla_project",
    )(x, p["g1"], mods, mods, *consts, rope_c, rope_s)


def _prompt_attn_kernel(q_ref, k_ref, vt_ref, o_ref, m_ref, l_ref, acc_ref, st_ref, *, tile, heads):
    qi = pl.program_id(2)
    m_ref[...] = jnp.full(m_ref.shape, NEG_INF, F32)
    l_ref[...] = jnp.zeros(l_ref.shape, F32)
    acc_ref[...] = jnp.zeros(acc_ref.shape, F32)

    first = 2

    def scores(kt, slot, gs=range(heads), qt=qi):
        k0 = kt * tile if isinstance(kt, int) else pl.multiple_of(kt * tile, tile)
        q0 = pl.multiple_of(qt * tile, tile)
        for g in gs:
            st_ref[slot, g] = _dot_nt(k_ref[g, pl.ds(k0, tile), :], q_ref[g, pl.ds(q0, tile), :])

    def step(kt, slot, masked, gs=range(heads)):
        vt = vt_ref[kt]
        for g in gs:
            for w in range(tile // MXU_DIM):
                nk = (w + 1) * MXU_DIM if masked else tile
                alphas, probs = [], []
                for j in range(w * MXU_DIM // LANES, (w + 1) * MXU_DIM // LANES):
                    cols = slice(j * LANES, (j + 1) * LANES)
                    col = st_ref[slot, g, :nk, cols]
                    if masked:
                        kc = lax.broadcasted_iota(jnp.int32, col.shape, 0) // CHUNK
                        qc = (lax.broadcasted_iota(jnp.int32, col.shape, 1) + j * LANES) // CHUNK
                        col = jnp.where(kc <= qc, col, NEG_INF)
                    m_prev = m_ref[g, :, cols]
                    m_new = jnp.maximum(m_prev, jnp.max(col, axis=0, keepdims=True))
                    alpha = jnp.exp2(m_prev - m_new)
                    pr = jnp.exp2(col - m_new)
                    l_ref[g, :, cols] = alpha * l_ref[g, :, cols] + jnp.sum(pr, axis=0, keepdims=True)
                    m_ref[g, :, cols] = m_new
                    probs.append(pr.astype(BF16))
                    alphas.append(alpha)
                wide = slice(w * MXU_DIM, (w + 1) * MXU_DIM)
                acc_ref[g, :, wide] = (acc_ref[g, :, wide] * jnp.concatenate(alphas, axis=-1)
                                       + _dot(vt[:, :nk], jnp.concatenate(probs, axis=-1)))

    nxt = jnp.minimum(qi + 1, pl.num_programs(2) - 1)

    def ahead(gs=range(heads)):
        scores(0, first, gs, qt=nxt)

    @pl.when(qi == 0)
    def _():
        scores(0, 1)
        for g in range(heads):
            ahead([g])
            step(0, 1, True, [g])

    @pl.when(qi > 0)
    def _():
        for g in range(heads):
            scores(1, 0, [g])
            step(0, first, False, [g])

        def body(i, carry):
            kt = 1 + 2 * i
            for g in range(heads):
                scores(kt + 1, 1, [g])
                step(kt, 0, False, [g])
            for g in range(heads):
                scores(kt + 2, 0, [g])
                step(kt + 1, 1, False, [g])
            return carry

        lax.fori_loop(0, (qi - 1) // 2, body, 0)

        @pl.when((qi - 1) % 2 == 0)
        def _():
            for g in range(heads):
                ahead([g])
                step(qi, 0, True, [g])

        @pl.when((qi - 1) % 2 == 1)
        def _():
            for g in range(heads):
                scores(qi, 1, [g])
                step(qi - 1, 0, False, [g])
            for g in range(heads):
                ahead([g])
                step(qi, 1, True, [g])

    for g in range(heads):
        o_ref[g * KV_LORA_RANK:(g + 1) * KV_LORA_RANK, :] = (acc_ref[g] / l_ref[g]).astype(o_ref.dtype)


def _prompt_attention(q_cat, k_cat, ckv_t, batch, seq, tile, heads):
    n = batch * seq
    nq = seq // tile
    assert ckv_t.shape == (n // tile, KV_LORA_RANK, tile), ckv_t.shape
    return pl.pallas_call(
        functools.partial(_prompt_attn_kernel, tile=tile, heads=heads),
        grid=(batch, N_HEADS // heads, nq),
        in_specs=[
            pl.BlockSpec((heads, seq, HEAD_PAD), lambda b, h, i: (h, b, 0)),
            pl.BlockSpec((heads, seq, HEAD_PAD), lambda b, h, i: (h, b, 0)),
            pl.BlockSpec((nq, KV_LORA_RANK, tile), lambda b, h, i: (b, 0, 0)),
        ],
        out_specs=pl.BlockSpec((heads * KV_LORA_RANK, tile), lambda b, h, i: (h, b * nq + i)),
        out_shape=jax.ShapeDtypeStruct((N_HEADS * KV_LORA_RANK, n), BF16),
        scratch_shapes=[
            pltpu.VMEM((heads, 1, tile), F32),
            pltpu.VMEM((heads, 1, tile), F32),
            pltpu.VMEM((heads, KV_LORA_RANK, tile), F32),
            pltpu.VMEM((3, heads, tile, tile), F32),
        ],
        compiler_params=_cparams(3),
        name="prompt_attention",
    )(q_cat, k_cat, ckv_t)


def _sample_attn_kernel(q_ref, kn_ref, ckvn_ref, cckv_ref, ckpe_ref, w_uk_ref, g_kn_ref, o_ref,
                        m_ref, l_ref, acc_ref, *, tile, past, n_new):
    m_ref[...] = jnp.full(m_ref.shape, NEG_INF, F32)
    l_ref[...] = jnp.zeros(l_ref.shape, F32)
    acc_ref[...] = jnp.zeros(acc_ref.shape, F32)
    g_kn = g_kn_ref[...]
    qr_all = jnp.concatenate([q_ref[hd][:, QK_NOPE_DIM:] for hd in range(N_HEADS)], axis=0)
    n_kn = N_HEADS * QK_NOPE_DIM
    q_lat = [_dot((q_ref[hd][:, :QK_NOPE_DIM] * g_kn).astype(BF16),
                  w_uk_ref[hd * QK_NOPE_DIM:(hd + 1) * QK_NOPE_DIM, :]).astype(BF16) for hd in range(N_HEADS)]
    w_and_q = jnp.concatenate([w_uk_ref[...]] + q_lat, axis=0)

    def update(s, v):
        m_prev = m_ref[...]
        m_new = jnp.maximum(m_prev, jnp.max(s, axis=-1, keepdims=True))
        alpha = jnp.exp2(m_prev - m_new)
        pr = jnp.exp2(s - m_new)
        l_ref[...] = alpha * l_ref[...] + jnp.sum(pr, axis=-1, keepdims=True)
        acc_ref[...] = alpha * acc_ref[...] + _dot(pr.astype(BF16), v)
        m_ref[...] = m_new

    def up_project(kt):
        v = cckv_ref[kt * tile:(kt + 1) * tile, :].astype(BF16)
        return v, _dot_nt(w_and_q, v)

    def attend(kt, v, kn_s):
        kn_t = kn_s[:n_kn, :]
        s_raw = kn_s[n_kn:, :]
        kpe_t = ckpe_ref[:, kt * tile:(kt + 1) * tile].astype(BF16)
        kpe_t = jnp.concatenate([kpe_t, jnp.zeros_like(kpe_t)], axis=0)
        s_rope = _dot(qr_all, kpe_t)
        inv = []
        for hd in range(N_HEADS):
            blk = kn_t[hd * QK_NOPE_DIM:(hd + 1) * QK_NOPE_DIM, :]
            ms = jnp.sum(blk * blk, axis=0, keepdims=True) / QK_NOPE_DIM
            inv.append(jnp.broadcast_to(lax.rsqrt(ms + EPS), (n_new, tile)))
        s = s_raw * jnp.concatenate(inv, axis=0) + s_rope
        update(s, v)

    n_tiles = past // tile
    nxt = up_project(0)
    for kt in range(n_tiles):
        cur = nxt
        if kt + 1 < n_tiles:
            nxt = up_project(kt + 1)
        attend(kt, *cur)

    s_new = jnp.concatenate([_dot_nt(q_ref[hd], kn_ref[hd]) for hd in range(N_HEADS)], axis=0)
    q_pos = past + lax.broadcasted_iota(jnp.int32, s_new.shape, 0) % n_new
    k_pos = past + lax.broadcasted_iota(jnp.int32, s_new.shape, 1)
    s_new = jnp.where(k_pos // CHUNK <= q_pos // CHUNK, s_new, NEG_INF)
    update(s_new, ckvn_ref[...])

    o = (acc_ref[...] / l_ref[...]).astype(o_ref.dtype)
    for hd in range(N_HEADS):
        o_ref[:, hd * KV_LORA_RANK:(hd + 1) * KV_LORA_RANK] = o[hd * n_new:(hd + 1) * n_new, :]


def _sample_attention(q_cat, k_cat, ckv_b, cache_ckv, cache_kpe, a, w_uk, g_kn, n_new, tile):
    _, dec_batch, past, _ = cache_ckv.shape
    n = dec_batch * n_new
    rows = N_HEADS * n_new
    hrow = pl.BlockSpec((N_HEADS, n_new, HEAD_PAD), lambda b: (0, b, 0))
    cache_kpe_t = jnp.swapaxes(cache_kpe, 2, 3)
    return pl.pallas_call(
        functools.partial(_sample_attn_kernel, tile=tile, past=past, n_new=n_new),
        grid=(dec_batch,),
        in_specs=[
            hrow, hrow,
            pl.BlockSpec((n_new, KV_LORA_RANK), lambda b: (b, 0)),
            pl.BlockSpec((None, None, past, KV_LORA_RANK), lambda b: (a, b, 0, 0)),
            pl.BlockSpec((None, None, QK_ROPE_DIM, past), lambda b: (a, b, 0, 0)),
            _const_spec(w_uk.shape), _const_spec(g_kn.shape),
        ],
        out_specs=pl.BlockSpec((n_new, N_HEADS * KV_LORA_RANK), lambda b: (b, 0)),
        out_shape=jax.ShapeDtypeStruct((n, N_HEADS * KV_LORA_RANK), BF16),
        scratch_shapes=[
            pltpu.VMEM((rows, 1), F32),
            pltpu.VMEM((rows, 1), F32),
            pltpu.VMEM((rows, KV_LORA_RANK), F32),
        ],
        compiler_params=_cparams(1),
        name="sample_attention",
    )(q_cat, k_cat, ckv_b, cache_ckv, cache_kpe_t, w_uk, g_kn)


def _ffn_tail(x1, g2_ref, sh2_ref, sc2_ref, gt2_ref, w1_ref, w2_ref, ff_chunk):
    h2 = _mod_norm(x1, g2_ref[...], sh2_ref[...], sc2_ref[...]).astype(BF16)
    return x1 + gt2_ref[...] * _sq_relu_ffn(h2, w1_ref, w2_ref, ff_chunk)


def _mla_tail_kernel(x_ref, ol_ref, w_uv_ref, w_o_ref, gt1_ref, g2_ref, sh2_ref, sc2_ref, gt2_ref,
                     w1_ref, w2_ref, o_ref, *, ff_chunk, latent_major):
    heads = []
    for hd in range(N_HEADS):
        lat = slice(hd * KV_LORA_RANK, (hd + 1) * KV_LORA_RANK)
        if latent_major:
            up = lax.dot_general(ol_ref[lat, :], w_uv_ref[hd], (((0,), (0,)), ((), ())),
                                 preferred_element_type=F32)
        else:
            up = _dot(ol_ref[:, lat], w_uv_ref[hd])
        heads.append(up.astype(BF16))
    m = _dot(jnp.concatenate(heads, axis=-1), w_o_ref[...])
    x1 = x_ref[...] + gt1_ref[...] * m
    o_ref[...] = _ffn_tail(x1, g2_ref, sh2_ref, sc2_ref, gt2_ref, w1_ref, w2_ref, ff_chunk)


def _mla_tail(x, o_lat, mods, mod_spec, p, tm, ff_chunk, latent_major):
    n, d = x.shape
    row = lambda w: pl.BlockSpec((tm, w), lambda i: (i, 0))
    consts_a = [p["w_uv"], p["w_o"]]
    consts_b = [p["w1"], p["w2"]]
    if latent_major:
        ol_spec = pl.BlockSpec((o_lat.shape[0], tm), lambda i: (0, i))
    else:
        ol_spec = row(o_lat.shape[1])
    return pl.pallas_call(
        functools.partial(_mla_tail_kernel, ff_chunk=ff_chunk, latent_major=latent_major),
        grid=(n // tm,),
        in_specs=[row(d), ol_spec] + [_const_spec(c.shape) for c in consts_a]
        + [mod_spec(2), _const_spec(p["g2"].shape), mod_spec(3), mod_spec(4), mod_spec(5)]
        + [_layer_spec(c, p["layer"]) for c in consts_b],
        out_specs=row(d),
        out_shape=jax.ShapeDtypeStruct((n, d), F32),
        compiler_params=_cparams(1),
        name="mla_tail_ffn",
    )(x, o_lat, *consts_a, mods, p["g2"], mods, mods, mods, *consts_b)


def _sgu_layer_kernel(x_ref, g1_ref, sh1_ref, sc1_ref, gt1_ref, w_in_ref, g_v_ref, w_s_ref, b_s_ref,
                      w_o_ref, g2_ref, sh2_ref, sc2_ref, gt2_ref, w1_ref, w2_ref, *out_refs,
                      ff_chunk, period, offset, emit_v):
    x = x_ref[...]
    tm = x.shape[0]
    width = w_o_ref.shape[0]
    gdim = width // SGU_GROUPS
    h = _mod_norm(x, g1_ref[...], sh1_ref[...], sc1_ref[...]).astype(BF16)
    v = _rms(jax.nn.gelu(_dot(h, w_in_ref[:, width:])), width) * g_v_ref[...]
    u = jax.nn.gelu(_dot(h, w_in_ref[:, :width]))
    if emit_v:
        out_refs[1][...] = v
    vb = v.astype(BF16)

    ri = lax.broadcasted_iota(jnp.int32, (SGU_CHUNK, SGU_CHUNK), 0)
    ci = lax.broadcasted_iota(jnp.int32, (SGU_CHUNK, SGU_CHUNK), 1)
    vis = (ri // period == ci // period) & ((ci % period + offset) // CHUNK <= (ri % period + offset) // CHUNK)
    w_mix = [jnp.where(vis, w_s_ref[g], 0.0).astype(BF16) for g in range(SGU_GROUPS)]

    rows = []
    for c in range(tm // SGU_CHUNK):
        r0 = c * SGU_CHUNK
        cols = []
        for g in range(SGU_GROUPS):
            l0 = g * gdim
            mixed = _dot(w_mix[g], vb[r0:r0 + SGU_CHUNK, l0:l0 + gdim]) + b_s_ref[g]
            cols.append((u[r0:r0 + SGU_CHUNK, l0:l0 + gdim] * mixed).astype(BF16))
        rows.append(jnp.concatenate(cols, axis=-1))
    gated = jnp.concatenate(rows, axis=0)
    x1 = x + gt1_ref[...] * _dot(gated, w_o_ref[...])
    out_refs[0][...] = _ffn_tail(x1, g2_ref, sh2_ref, sc2_ref, gt2_ref, w1_ref, w2_ref, ff_chunk)


def _sgu_layer(x, mods, mod_spec, p, tm, ff_chunk, period, offset, emit_v):
    n, d = x.shape
    width = p["w_o"].shape[0]
    row = lambda w: pl.BlockSpec((tm, w), lambda i: (i, 0))
    cs = lambda a: _const_spec(a.shape)
    out_specs = [row(d)]
    out_shape = [jax.ShapeDtypeStruct((n, d), F32)]
    if emit_v:
        out_specs.append(row(width))
        out_shape.append(jax.ShapeDtypeStruct((n, width), F32))
    return pl.pallas_call(
        functools.partial(_sgu_layer_kernel, ff_chunk=ff_chunk, period=period, offset=offset, emit_v=emit_v),
        grid=(n // tm,),
        in_specs=[row(d), cs(p["g1"]), mod_spec(0), mod_spec(1), mod_spec(2), cs(p["w_in"]), cs(p["g_v"]),
                  cs(p["w_s"]), cs(p["b_s"]), cs(p["w_o"]), cs(p["g2"]), mod_spec(3), mod_spec(4), mod_spec(5),
                  _layer_spec(p["w1"], p["layer"]), _layer_spec(p["w2"], p["layer"])],
        out_specs=out_specs,
        out_shape=out_shape,
        compiler_params=_cparams(1),
        name="sgu_layer_ffn",
    )(x, p["g1"], mods, mods, mods, p["w_in"], p["g_v"], p["w_s"], p["b_s"], p["w_o"],
      p["g2"], mods, mods, mods, p["w1"], p["w2"])


def _rope_tables(pos):
    half = QK_ROPE_DIM // 2
    inv = 1.0 / (ROPE_THETA ** (jnp.arange(half, dtype=F32) / half))
    ang = pos.astype(F32)[:, None] * inv[None, :]
    cos, sin = jnp.cos(ang), jnp.sin(ang)
    z = jnp.zeros((pos.shape[0], LANES - QK_ROPE_DIM), F32)
    return jnp.concatenate([cos, cos, z], axis=-1), jnp.concatenate([-sin, sin, z], axis=-1)


def _pad_lanes(g, n):
    return jnp.pad(g, (0, n - g.shape[0])).reshape(1, n)


def _prep_mla(a, norm1_g, norm2_g, ffn_w1, ffn_w2, layer, mla_w_in, mla_g_qa, mla_g_kva, mla_w_q_up,
              mla_w_uk, mla_w_uv, mla_g_qn, mla_g_qr, mla_g_kn, mla_g_kr, mla_w_o):
    d = mla_w_in.shape[1]
    w_in = mla_w_in[a]
    w_in = jnp.concatenate([w_in[:, Q_LORA_RANK:Q_LORA_RANK + KV_LORA_RANK],
                            jnp.pad(w_in[:, Q_LORA_RANK + KV_LORA_RANK:], ((0, 0), (0, LANES - QK_ROPE_DIM))),
                            w_in[:, :Q_LORA_RANK]], axis=1).astype(BF16)
    half = QK_ROPE_DIM // 2
    w_q_up = mla_w_q_up[a].reshape(Q_LORA_RANK, N_HEADS, QK_NOPE_DIM + QK_ROPE_DIM)
    w_r1 = w_q_up[:, :, QK_NOPE_DIM:QK_NOPE_DIM + half]
    w_r2 = w_q_up[:, :, QK_NOPE_DIM + half:]
    w_z = jnp.zeros((Q_LORA_RANK, N_HEADS, LANES - QK_ROPE_DIM), w_q_up.dtype)
    w_q_up = jnp.concatenate([w_q_up[:, :, :QK_NOPE_DIM], w_r1, w_r2, w_z, w_r2, w_r1, w_z], axis=-1)
    g_qr = mla_g_qr[a]
    g_qr = jnp.concatenate([_pad_lanes(g_qr, LANES),
                            _pad_lanes(jnp.concatenate([g_qr[half:], g_qr[:half]]), LANES)], axis=0)
    return {
        "g1": norm1_g[layer].reshape(1, d), "g2": norm2_g[layer].reshape(1, d),
        "w1": ffn_w1, "w2": ffn_w2, "layer": layer,
        "w_in": w_in,
        "g_qa": mla_g_qa[a].reshape(1, -1), "g_kva": mla_g_kva[a].reshape(1, -1),
        "g_kr": _pad_lanes(mla_g_kr[a], LANES),
        "w_q_up": w_q_up.reshape(Q_LORA_RANK, N_HEADS * Q_HEAD_COLS).astype(BF16),
        "g_qn": mla_g_qn[a].reshape(1, -1), "g_qr": g_qr,
        "w_uk": mla_w_uk[a].reshape(KV_LORA_RANK, N_HEADS * QK_NOPE_DIM).astype(BF16),
        "g_kn": mla_g_kn[a].reshape(1, -1),
        "w_uv": jnp.swapaxes(mla_w_uv[a], 0, 1).astype(BF16),
        "w_o": mla_w_o[a].astype(BF16),
    }


def _prep_sgu(b, norm1_g, norm2_g, ffn_w1, ffn_w2, layer, cm_w_in, cm_g_v, cm_w_s, cm_b_s, cm_w_o, idx):
    d = cm_w_in.shape[1]
    period = len(idx)
    lo = int(idx[0])
    assert SGU_CHUNK % period == 0 and np.array_equal(idx, lo + np.arange(period)), idx
    reps = SGU_CHUNK // period
    w_s = jnp.tile(cm_w_s[b][:, lo:lo + period, lo:lo + period], (1, reps, reps))
    b_s = jnp.tile(cm_b_s[b][:, lo:lo + period], (1, reps))
    gdim = cm_w_o.shape[1] // SGU_GROUPS
    return {
        "g1": norm1_g[layer].reshape(1, d), "g2": norm2_g[layer].reshape(1, d),
        "w1": ffn_w1, "w2": ffn_w2, "layer": layer,
        "w_in": cm_w_in[b].astype(BF16), "g_v": cm_g_v[b].reshape(1, -1),
        "w_s": w_s, "b_s": jnp.broadcast_to(b_s[:, :, None], (SGU_GROUPS, SGU_CHUNK, gdim)),
        "w_o": cm_w_o[b].astype(BF16),
    }, period, lo


def kernel(x_prompt, x_sample, cache_ckv, cache_kpe, c_prompt, c_sample, ada_w, ada_b, norm1_g, norm2_g,
           ffn_w1, ffn_w2, mla_w_in, mla_g_qa, mla_g_kva, mla_w_q_up, mla_w_uk, mla_w_uv, mla_g_qn,
           mla_g_qr, mla_g_kn, mla_g_kr, mla_w_o, cm_w_in, cm_g_v, cm_w_s, cm_b_s, cm_w_o):
    batch, seq, d = x_prompt.shape
    dec_batch, dec_seq, _ = x_sample.shape
    past = cache_ckv.shape[2]
    depth = ada_w.shape[0]
    n_p, n_s = batch * seq, dec_batch * dec_seq
    tm_p, tm_proj, tm_s, attn_tile, attn_heads, ff_chunk = 512, 1024, n_s, 512, 4, 1024

    pos_p = np.arange(seq)
    pos_s = past + np.arange(dec_seq)
    rc_p, rs_p = _rope_tables(jnp.asarray(pos_p))
    rc_s, rs_s = _rope_tables(jnp.asarray(np.tile(pos_s, dec_batch)))
    rope_spec_p = pl.BlockSpec((tm_proj, LANES), lambda i: (i % (seq // tm_proj), 0))
    rope_spec_s = pl.BlockSpec((tm_s, LANES), lambda i: (i, 0))

    n_c = batch + dec_batch
    c_all = jnp.concatenate([c_prompt, c_sample], axis=0)
    c_all = jnp.pad(c_all, ((0, -n_c % BF16_ROWS), (0, 0)))
    mod = _ada_modulation(c_all, ada_w, ada_b)
    m_p = mod[:, :batch].reshape(depth, batch, 1, 6 * d)
    m_s = jnp.repeat(mod[:, batch:n_c], dec_seq, axis=1)

    def prompt_mods(layer, tm):
        return lambda j: pl.BlockSpec((None, None, 1, d), lambda i: (layer, i // (seq // tm), 0, j))

    def sample_mods(layer):
        return lambda j: pl.BlockSpec((None, tm_s, d), lambda i: (layer, i, j))

    ffn_w1 = ffn_w1.astype(BF16)
    ffn_w2 = ffn_w2.astype(BF16)
    x_p = x_prompt.reshape(n_p, d)
    x_s = x_sample.reshape(n_s, d)
    ckv_p_l, kpe_p_l, ckv_s_l, kpe_s_l, v_s_l = [], [], [], [], []
    for layer in range(depth):
        mod_spec_p, mod_spec_proj, mod_spec_s = prompt_mods(layer, tm_p), prompt_mods(layer, tm_proj), sample_mods(layer)
        if layer % 2 == 0:
            a = layer // 2
            p = _prep_mla(a, norm1_g, norm2_g, ffn_w1, ffn_w2, layer, mla_w_in, mla_g_qa, mla_g_kva,
                          mla_w_q_up, mla_w_uk, mla_w_uv, mla_g_qn, mla_g_qr, mla_g_kn, mla_g_kr, mla_w_o)
            q_p, k_p, ckv_p, _, ckvt_p, kr_p = _mla_project(x_p, m_p, mod_spec_proj, p, rc_p, rs_p, rope_spec_p,
                                                            tm_proj, attn_tile)
            q_s, k_s, ckv_s, ckvb_s, _, kr_s = _mla_project(x_s, m_s, mod_spec_s, p, rc_s, rs_s, rope_spec_s,
                                                            tm_s, attn_tile)
            ol_p = _prompt_attention(q_p, k_p, ckvt_p, batch, seq, attn_tile, attn_heads)
            ol_s = _sample_attention(q_s, k_s, ckvb_s, cache_ckv, cache_kpe, a, p["w_uk"].T, p["g_kn"],
                                     dec_seq, attn_tile)
            x_p = _mla_tail(x_p, ol_p, m_p, mod_spec_p, p, tm_p, ff_chunk, True)
            x_s = _mla_tail(x_s, ol_s, m_s, mod_spec_s, p, tm_s, ff_chunk, False)
            ckv_p_l.append(ckv_p.reshape(batch, seq, -1))
            kpe_p_l.append(kr_p.reshape(batch, seq, -1))
            ckv_s_l.append(ckv_s.reshape(dec_batch, dec_seq, -1))
            kpe_s_l.append(kr_s.reshape(dec_batch, dec_seq, -1))
        else:
            b = layer // 2
            args = (b, norm1_g, norm2_g, ffn_w1, ffn_w2, layer, cm_w_in, cm_g_v, cm_w_s, cm_b_s, cm_w_o)
            pp, period_p, off_p = _prep_sgu(*args, np.arange(SGU_CHUNK))
            ps, period_s, off_s = _prep_sgu(*args, pos_s % SGU_CHUNK)
            (x_p,) = _sgu_layer(x_p, m_p, mod_spec_p, pp, tm_p, ff_chunk, period_p, off_p, False)
            x_s, v_s = _sgu_layer(x_s, m_s, mod_spec_s, ps, tm_s, ff_chunk, period_s, off_s, True)
            v_s_l.append(v_s.reshape(dec_batch, dec_seq, -1))

    return (x_p.reshape(batch, seq, d), x_s.reshape(dec_batch, dec_seq, d),
            jnp.stack(ckv_p_l), jnp.stack(kpe_p_l), jnp.stack(ckv_s_l), jnp.stack(kpe_s_l),
            jnp.stack(v_s_l))
```

```python
import functools

import numpy as np
import jax
import jax.numpy as jnp
from jax import lax
from jax.experimental import pallas as pl
from jax.experimental.pallas import tpu as pltpu

F32 = jnp.float32
BF16 = jnp.bfloat16

CHUNK = 64
N_HEADS = 8
QK_NOPE_DIM = 128
QK_ROPE_DIM = 64
V_HEAD_DIM = 128
Q_LORA_RANK = 384
KV_LORA_RANK = 256
ROPE_THETA = 10000.0
SGU_CHUNK = 128
SGU_GROUPS = 8
EPS = 1e-6

LANES = 128
BF16_ROWS = 16
MXU_DIM = 256
HEAD_PAD = 2 * LANES
Q_HEAD_COLS = 3 * LANES
VMEM_LIMIT = 56 * 1024 * 1024

NEG_INF = float(np.finfo(np.float32).min)
Q_SCALE = float((QK_NOPE_DIM + QK_ROPE_DIM) ** -0.5 * np.log2(np.e))


def _cparams(n_axes):
    return pltpu.CompilerParams(
        dimension_semantics=("arbitrary",) * n_axes,
        vmem_limit_bytes=VMEM_LIMIT,
    )


def _const_spec(shape):
    nd = len(shape)
    return pl.BlockSpec(shape, lambda *_: (0,) * nd, pipeline_mode=pl.Buffered(1))


def _layer_spec(stacked, layer):
    return pl.BlockSpec((None,) + stacked.shape[1:], lambda *_: (layer, 0, 0), pipeline_mode=pl.Buffered(1))


def _dot(a, b):
    return jnp.dot(a, b, preferred_element_type=F32)


def _dot_nt(a, b):
    return lax.dot_general(a, b, (((1,), (1,)), ((), ())), preferred_element_type=F32)


def _rms(x, n):
    ms = jnp.sum(x * x, axis=-1, keepdims=True) / n
    return x * lax.rsqrt(ms + EPS)


def _rope128(x, c, s):
    half = QK_ROPE_DIM // 2
    rot = pltpu.roll(x, LANES - half, axis=1) + pltpu.roll(x, half, axis=1)
    return x * c + rot * s


def _mod_norm(x, g, shift, scale):
    return _rms(x, x.shape[-1]) * g * (1.0 + scale) + shift


def _sq_relu_ffn(hb, w1_ref, w2_ref, ff_chunk):
    d_ff = w1_ref.shape[1]
    acc = None
    for c in range(d_ff // ff_chunk):
        a = _dot(hb, w1_ref[:, c * ff_chunk:(c + 1) * ff_chunk])
        a = jnp.maximum(a, 0.0)
        a = (a * a).astype(BF16)
        part = _dot(a, w2_ref[c * ff_chunk:(c + 1) * ff_chunk, :])
        acc = part if acc is None else acc + part
    return acc


def _ada_kernel(c_ref, w_ref, b_ref, o_ref):
    s = jax.nn.silu(c_ref[...]).astype(BF16)
    o_ref[...] = _dot(s, w_ref[...].astype(BF16)) + b_ref[...]


def _ada_modulation(c_all, ada_w, ada_b):
    depth, d, n6 = ada_w.shape
    rows = c_all.shape[0]
    tn = 1536
    return pl.pallas_call(
        _ada_kernel,
        grid=(depth, n6 // tn),
        in_specs=[
            pl.BlockSpec((rows, d), lambda l, j: (0, 0)),
            pl.BlockSpec((None, d, tn), lambda l, j: (l, 0, j)),
            pl.BlockSpec((None, 1, tn), lambda l, j: (l, 0, j)),
        ],
        out_specs=pl.BlockSpec((None, rows, tn), lambda l, j: (l, 0, j)),
        out_shape=jax.ShapeDtypeStruct((depth, rows, n6), F32),
        compiler_params=_cparams(2),
        name="ada_modulation",
    )(c_all, ada_w, ada_b.reshape(depth, 1, n6))


def _mla_proj_kernel(x_ref, g1_ref, sh_ref, sc_ref, w_in_ref, g_qa_ref, g_kva_ref, g_kr_ref,
                     w_qup_ref, g_qn_ref, g_qr_ref, w_uk_ref, g_kn_ref, rc_ref, rs_ref, *refs):
    n_cast = (len(refs) - 6) // 2
    cast_in, (q_ref, k_ref, ckv_ref, ckvb_ref, ckvt_ref, kr_ref), cast_out = (
        refs[:n_cast], refs[n_cast:n_cast + 6], refs[n_cast + 6:])
    for src, dst in zip(cast_in, cast_out):
        dst[...] = src[...].astype(BF16)
    x = x_ref[...]
    h = _mod_norm(x, g1_ref[...], sh_ref[...], sc_ref[...]).astype(BF16)
    c0 = KV_LORA_RANK
    ckv = _rms(_dot(h, w_in_ref[:, :c0]), KV_LORA_RANK) * g_kva_ref[...]
    a = _dot(h, w_in_ref[:, c0:])
    cq = _rms(a[:, LANES:], Q_LORA_RANK) * g_qa_ref[...]
    rc = rc_ref[...]
    rs = rs_ref[...]
    kr = _rope128(_rms(a[:, :LANES], QK_ROPE_DIM) * g_kr_ref[...], rc, rs)
    ckv_ref[...] = ckv
    ckv_b = ckv.astype(BF16)
    ckvb_ref[...] = ckv_b
    vt_tile = ckvt_ref.shape[-1]
    for j in range(ckvt_ref.shape[0]):
        ckvt_ref[j] = ckv[j * vt_tile:(j + 1) * vt_tile, :].T.astype(BF16)
    kr_ref[...] = kr[:, :QK_ROPE_DIM]
    kr_b = kr.astype(BF16)

    kn_all = _dot(ckv_b, w_uk_ref[...])
    q = _dot(cq.astype(BF16), w_qup_ref[...])
    g_qn = g_qn_ref[...] * Q_SCALE
    rot_c = rc * (g_qr_ref[0:1, :] * Q_SCALE)
    rot_s = rs * (g_qr_ref[1:2, :] * Q_SCALE)
    g_kn = g_kn_ref[...]
    for hd in range(N_HEADS):
        kn = _rms(kn_all[:, hd * QK_NOPE_DIM:(hd + 1) * QK_NOPE_DIM], QK_NOPE_DIM) * g_kn
        k_ref[hd] = jnp.concatenate([kn.astype(BF16), kr_b], axis=-1)
    for hd in range(N_HEADS):
        o = hd * Q_HEAD_COLS
        qn = _rms(q[:, o:o + QK_NOPE_DIM], QK_NOPE_DIM) * g_qn
        x = q[:, o + QK_NOPE_DIM:o + 2 * LANES]
        x_sw = q[:, o + 2 * LANES:o + 3 * LANES]
        inv = lax.rsqrt(jnp.sum(x * x, axis=-1, keepdims=True) / QK_ROPE_DIM + EPS)
        qr = (x * rot_c + x_sw * rot_s) * inv
        q_ref[hd] = jnp.concatenate([qn.astype(BF16), qr.astype(BF16)], axis=-1)


def _mla_project(x, mods, mod_spec, p, rope_c, rope_s, rope_spec, tm, vt_tile, side_casts=()):
    n, d = x.shape
    steps = n // tm
    row = lambda w: pl.BlockSpec((tm, w), lambda i: (i, 0))
    hrow = pl.BlockSpec((N_HEADS, tm, HEAD_PAD), lambda i: (0, i, 0))
    consts = [p["w_in"], p["g_qa"], p["g_kva"], p["g_kr"], p["w_q_up"], p["g_qn"], p["g_qr"],
              p["w_uk"], p["g_kn"]]
    cast_specs = []
    for w in side_casts:
        layers, rows, cols = w.shape
        slab = layers * rows // steps
        assert slab * steps == layers * rows and rows % slab == 0 and slab % BF16_ROWS == 0, (w.shape, steps)
        per_layer = rows // slab
        cast_specs.append(pl.BlockSpec((None, slab, cols), lambda i, per_layer=per_layer: (i // per_layer, i % per_layer, 0)))
    return pl.pallas_call(
        _mla_proj_kernel,
        grid=(steps,),
        in_specs=[row(d), _const_spec(p["g1"].shape), mod_spec(0), mod_spec(1)]
        + [_const_spec(c.shape) for c in consts] + [rope_spec, rope_spec] + cast_specs,
        out_specs=[hrow, hrow, row(KV_LORA_RANK), row(KV_LORA_RANK),
                   pl.BlockSpec((tm // vt_tile, KV_LORA_RANK, vt_tile), lambda i: (i, 0, 0)), row(QK_ROPE_DIM)]
        + cast_specs,
        out_shape=[
            jax.ShapeDtypeStruct((N_HEADS, n, HEAD_PAD), BF16),
            jax.ShapeDtypeStruct((N_HEADS, n, HEAD_PAD), BF16),
            jax.ShapeDtypeStruct((n, KV_LORA_RANK), F32),
            jax.ShapeDtypeStruct((n, KV_LORA_RANK), BF16),
            jax.ShapeDtypeStruct((n // vt_tile, KV_LORA_RANK, vt_tile), BF16),
            jax.ShapeDtypeStruct((n, QK_ROPE_DIM), F32),
        ] + [jax.ShapeDtypeStruct(w.shape, BF16) for w in side_casts],
        compiler_params=_cparams(1),
        name="mla_project",
    )(x, p["g1"], mods, mods, *consts, rope_c, rope_s, *side_casts)


def _prompt_attn_kernel(q_ref, k_ref, vt_ref, o_ref, m_ref, l_ref, acc_ref, st_ref, *, tile, heads):
    qi = pl.program_id(2)
    m_ref[...] = jnp.full(m_ref.shape, NEG_INF, F32)
    l_ref[...] = jnp.zeros(l_ref.shape, F32)
    acc_ref[...] = jnp.zeros(acc_ref.shape, F32)

    first = 2

    def scores(kt, slot, gs=range(heads), qt=qi):
        k0 = kt * tile if isinstance(kt, int) else pl.multiple_of(kt * tile, tile)
        q0 = pl.multiple_of(qt * tile, tile)
        for g in gs:
            st_ref[slot, g] = _dot_nt(k_ref[g, pl.ds(k0, tile), :], q_ref[g, pl.ds(q0, tile), :])

    def step(kt, slot, masked, gs=range(heads)):
        vt = vt_ref[kt]
        for g in gs:
            for w in range(tile // MXU_DIM):
                nk = (w + 1) * MXU_DIM if masked else tile
                alphas, probs = [], []
                for j in range(w * MXU_DIM // LANES, (w + 1) * MXU_DIM // LANES):
                    cols = slice(j * LANES, (j + 1) * LANES)
                    col = st_ref[slot, g, :nk, cols]
                    if masked:
                        kc = lax.broadcasted_iota(jnp.int32, col.shape, 0) // CHUNK
                        qc = (lax.broadcasted_iota(jnp.int32, col.shape, 1) + j * LANES) // CHUNK
                        col = jnp.where(kc <= qc, col, NEG_INF)
                    m_prev = m_ref[g, :, cols]
                    m_new = jnp.maximum(m_prev, jnp.max(col, axis=0, keepdims=True))
                    alpha = jnp.exp2(m_prev - m_new)
                    pr = jnp.exp2(col - m_new)
                    l_ref[g, :, cols] = alpha * l_ref[g, :, cols] + jnp.sum(pr, axis=0, keepdims=True)
                    m_ref[g, :, cols] = m_new
                    probs.append(pr.astype(BF16))
                    alphas.append(alpha)
                wide = slice(w * MXU_DIM, (w + 1) * MXU_DIM)
                acc_ref[g, :, wide] = (acc_ref[g, :, wide] * jnp.concatenate(alphas, axis=-1)
                                       + _dot(vt[:, :nk], jnp.concatenate(probs, axis=-1)))

    nxt = jnp.minimum(qi + 1, pl.num_programs(2) - 1)

    def ahead(gs=range(heads)):
        scores(0, first, gs, qt=nxt)

    @pl.when(qi == 0)
    def _():
        scores(0, 1)
        for g in range(heads):
            ahead([g])
            step(0, 1, True, [g])

    @pl.when(qi > 0)
    def _():
        for g in range(heads):
            scores(1, 0, [g])
            step(0, first, False, [g])

        def body(i, carry):
            kt = 1 + 2 * i
            for g in range(heads):
                scores(kt + 1, 1, [g])
                step(kt, 0, False, [g])
            for g in range(heads):
                scores(kt + 2, 0, [g])
                step(kt + 1, 1, False, [g])
            return carry

        lax.fori_loop(0, (qi - 1) // 2, body, 0)

        @pl.when((qi - 1) % 2 == 0)
        def _():
            for g in range(heads):
                ahead([g])
                step(qi, 0, True, [g])

        @pl.when((qi - 1) % 2 == 1)
        def _():
            for g in range(heads):
                scores(qi, 1, [g])
                step(qi - 1, 0, False, [g])
            for g in range(heads):
                ahead([g])
                step(qi, 1, True, [g])

    for g in range(heads):
        o_ref[g * KV_LORA_RANK:(g + 1) * KV_LORA_RANK, :] = (acc_ref[g] / l_ref[g]).astype(o_ref.dtype)


def _prompt_attention(q_cat, k_cat, ckv_t, batch, seq, tile, heads):
    n = batch * seq
    nq = seq // tile
    assert ckv_t.shape == (n // tile, KV_LORA_RANK, tile), ckv_t.shape
    return pl.pallas_call(
        functools.partial(_prompt_attn_kernel, tile=tile, heads=heads),
        grid=(batch, N_HEADS // heads, nq),
        in_specs=[
            pl.BlockSpec((heads, seq, HEAD_PAD), lambda b, h, i: (h, b, 0)),
            pl.BlockSpec((heads, seq, HEAD_PAD), lambda b, h, i: (h, b, 0)),
            pl.BlockSpec((nq, KV_LORA_RANK, tile), lambda b, h, i: (b, 0, 0)),
        ],
        out_specs=pl.BlockSpec((heads * KV_LORA_RANK, tile), lambda b, h, i: (h, b * nq + i)),
        out_shape=jax.ShapeDtypeStruct((N_HEADS * KV_LORA_RANK, n), BF16),
        scratch_shapes=[
            pltpu.VMEM((heads, 1, tile), F32),
            pltpu.VMEM((heads, 1, tile), F32),
            pltpu.VMEM((heads, KV_LORA_RANK, tile), F32),
            pltpu.VMEM((3, heads, tile, tile), F32),
        ],
        compiler_params=_cparams(3),
        name="prompt_attention",
    )(q_cat, k_cat, ckv_t)


def _sample_attn_kernel(q_ref, kn_ref, ckvn_ref, cckv_ref, ckpe_ref, w_uk_ref, g_kn_ref, o_ref,
                        m_ref, l_ref, acc_ref, *, tile, past, n_new):
    m_ref[...] = jnp.full(m_ref.shape, NEG_INF, F32)
    l_ref[...] = jnp.zeros(l_ref.shape, F32)
    acc_ref[...] = jnp.zeros(acc_ref.shape, F32)
    g_kn = g_kn_ref[...]
    qr_all = jnp.concatenate([q_ref[hd][:, QK_NOPE_DIM:] for hd in range(N_HEADS)], axis=0)
    n_kn = N_HEADS * QK_NOPE_DIM
    q_lat = [_dot((q_ref[hd][:, :QK_NOPE_DIM] * g_kn).astype(BF16),
                  w_uk_ref[hd * QK_NOPE_DIM:(hd + 1) * QK_NOPE_DIM, :]).astype(BF16) for hd in range(N_HEADS)]
    w_and_q = jnp.concatenate([w_uk_ref[...]] + q_lat, axis=0)

    def update(s, v):
        m_prev = m_ref[...]
        m_new = jnp.maximum(m_prev, jnp.max(s, axis=-1, keepdims=True))
        alpha = jnp.exp2(m_prev - m_new)
        pr = jnp.exp2(s - m_new)
        l_ref[...] = alpha * l_ref[...] + jnp.sum(pr, axis=-1, keepdims=True)
        acc_ref[...] = alpha * acc_ref[...] + _dot(pr.astype(BF16), v)
        m_ref[...] = m_new

    def up_project(kt):
        v = cckv_ref[kt * tile:(kt + 1) * tile, :].astype(BF16)
        return v, _dot_nt(w_and_q, v)

    def attend(kt, v, kn_s):
        kn_t = kn_s[:n_kn, :]
        s_raw = kn_s[n_kn:, :]
        kpe_t = ckpe_ref[:, kt * tile:(kt + 1) * tile].astype(BF16)
        kpe_t = jnp.concatenate([kpe_t, jnp.zeros_like(kpe_t)], axis=0)
        s_rope = _dot(qr_all, kpe_t)
        inv = []
        for hd in range(N_HEADS):
            blk = kn_t[hd * QK_NOPE_DIM:(hd + 1) * QK_NOPE_DIM, :]
            ms = jnp.sum(blk * blk, axis=0, keepdims=True) / QK_NOPE_DIM
            inv.append(jnp.broadcast_to(lax.rsqrt(ms + EPS), (n_new, tile)))
        s = s_raw * jnp.concatenate(inv, axis=0) + s_rope
        update(s, v)

    n_tiles = past // tile
    nxt = up_project(0)
    for kt in range(n_tiles):
        cur = nxt
        if kt + 1 < n_tiles:
            nxt = up_project(kt + 1)
        attend(kt, *cur)

    s_new = jnp.concatenate([_dot_nt(q_ref[hd], kn_ref[hd]) for hd in range(N_HEADS)], axis=0)
    q_pos = past + lax.broadcasted_iota(jnp.int32, s_new.shape, 0) % n_new
    k_pos = past + lax.broadcasted_iota(jnp.int32, s_new.shape, 1)
    s_new = jnp.where(k_pos // CHUNK <= q_pos // CHUNK, s_new, NEG_INF)
    update(s_new, ckvn_ref[...])

    o = (acc_ref[...] / l_ref[...]).astype(o_ref.dtype)
    for hd in range(N_HEADS):
        o_ref[:, hd * KV_LORA_RANK:(hd + 1) * KV_LORA_RANK] = o[hd * n_new:(hd + 1) * n_new, :]


def _sample_attention(q_cat, k_cat, ckv_b, cache_ckv, cache_kpe, a, w_uk, g_kn, n_new, tile):
    _, dec_batch, past, _ = cache_ckv.shape
    n = dec_batch * n_new
    rows = N_HEADS * n_new
    hrow = pl.BlockSpec((N_HEADS, n_new, HEAD_PAD), lambda b: (0, b, 0))
    cache_kpe_t = jnp.swapaxes(cache_kpe, 2, 3)
    return pl.pallas_call(
        functools.partial(_sample_attn_kernel, tile=tile, past=past, n_new=n_new),
        grid=(dec_batch,),
        in_specs=[
            hrow, hrow,
            pl.BlockSpec((n_new, KV_LORA_RANK), lambda b: (b, 0)),
            pl.BlockSpec((None, None, past, KV_LORA_RANK), lambda b: (a, b, 0, 0)),
            pl.BlockSpec((None, None, QK_ROPE_DIM, past), lambda b: (a, b, 0, 0)),
            _const_spec(w_uk.shape), _const_spec(g_kn.shape),
        ],
        out_specs=pl.BlockSpec((n_new, N_HEADS * KV_LORA_RANK), lambda b: (b, 0)),
        out_shape=jax.ShapeDtypeStruct((n, N_HEADS * KV_LORA_RANK), BF16),
        scratch_shapes=[
            pltpu.VMEM((rows, 1), F32),
            pltpu.VMEM((rows, 1), F32),
            pltpu.VMEM((rows, KV_LORA_RANK), F32),
        ],
        compiler_params=_cparams(1),
        name="sample_attention",
    )(q_cat, k_cat, ckv_b, cache_ckv, cache_kpe_t, w_uk, g_kn)


def _ffn_tail(x1, g2_ref, sh2_ref, sc2_ref, gt2_ref, w1_ref, w2_ref, ff_chunk):
    h2 = _mod_norm(x1, g2_ref[...], sh2_ref[...], sc2_ref[...]).astype(BF16)
    return x1 + gt2_ref[...] * _sq_relu_ffn(h2, w1_ref, w2_ref, ff_chunk)


def _mla_tail_kernel(x_ref, ol_ref, w_uv_ref, w_o_ref, gt1_ref, g2_ref, sh2_ref, sc2_ref, gt2_ref,
                     w1_ref, w2_ref, o_ref, *, ff_chunk, latent_major):
    heads = []
    for hd in range(N_HEADS):
        lat = slice(hd * KV_LORA_RANK, (hd + 1) * KV_LORA_RANK)
        if latent_major:
            up = lax.dot_general(ol_ref[lat, :], w_uv_ref[hd], (((0,), (0,)), ((), ())),
                                 preferred_element_type=F32)
        else:
            up = _dot(ol_ref[:, lat], w_uv_ref[hd])
        heads.append(up.astype(BF16))
    m = _dot(jnp.concatenate(heads, axis=-1), w_o_ref[...])
    x1 = x_ref[...] + gt1_ref[...] * m
    o_ref[...] = _ffn_tail(x1, g2_ref, sh2_ref, sc2_ref, gt2_ref, w1_ref, w2_ref, ff_chunk)


def _mla_tail(x, o_lat, mods, mod_spec, p, tm, ff_chunk, latent_major):
    n, d = x.shape
    row = lambda w: pl.BlockSpec((tm, w), lambda i: (i, 0))
    consts_a = [p["w_uv"], p["w_o"]]
    consts_b = [p["w1"], p["w2"]]
    if latent_major:
        ol_spec = pl.BlockSpec((o_lat.shape[0], tm), lambda i: (0, i))
    else:
        ol_spec = row(o_lat.shape[1])
    return pl.pallas_call(
        functools.partial(_mla_tail_kernel, ff_chunk=ff_chunk, latent_major=latent_major),
        grid=(n // tm,),
        in_specs=[row(d), ol_spec] + [_const_spec(c.shape) for c in consts_a]
        + [mod_spec(2), _const_spec(p["g2"].shape), mod_spec(3), mod_spec(4), mod_spec(5)]
        + [_layer_spec(c, p["layer"]) for c in consts_b],
        out_specs=row(d),
        out_shape=jax.ShapeDtypeStruct((n, d), F32),
        compiler_params=_cparams(1),
        name="mla_tail_ffn",
    )(x, o_lat, *consts_a, mods, p["g2"], mods, mods, mods, *consts_b)


def _sgu_layer_kernel(x_ref, g1_ref, sh1_ref, sc1_ref, gt1_ref, w_in_ref, g_v_ref, w_s_ref, b_s_ref,
                      w_o_ref, g2_ref, sh2_ref, sc2_ref, gt2_ref, w1_ref, w2_ref, *out_refs,
                      ff_chunk, period, offset, emit_v):
    x = x_ref[...]
    tm = x.shape[0]
    width = w_o_ref.shape[0]
    gdim = width // SGU_GROUPS
    h = _mod_norm(x, g1_ref[...], sh1_ref[...], sc1_ref[...]).astype(BF16)
    v = _rms(jax.nn.gelu(_dot(h, w_in_ref[:, width:])), width) * g_v_ref[...]
    u = jax.nn.gelu(_dot(h, w_in_ref[:, :width]))
    if emit_v:
        out_refs[1][...] = v
    vb = v.astype(BF16)

    ri = lax.broadcasted_iota(jnp.int32, (SGU_CHUNK, SGU_CHUNK), 0)
    ci = lax.broadcasted_iota(jnp.int32, (SGU_CHUNK, SGU_CHUNK), 1)
    vis = (ri // period == ci // period) & ((ci % period + offset) // CHUNK <= (ri % period + offset) // CHUNK)
    w_mix = [jnp.where(vis, w_s_ref[g], 0.0).astype(BF16) for g in range(SGU_GROUPS)]

    rows = []
    for c in range(tm // SGU_CHUNK):
        r0 = c * SGU_CHUNK
        cols = []
        for g in range(SGU_GROUPS):
            l0 = g * gdim
            mixed = _dot(w_mix[g], vb[r0:r0 + SGU_CHUNK, l0:l0 + gdim]) + b_s_ref[g]
            cols.append((u[r0:r0 + SGU_CHUNK, l0:l0 + gdim] * mixed).astype(BF16))
        rows.append(jnp.concatenate(cols, axis=-1))
    gated = jnp.concatenate(rows, axis=0)
    x1 = x + gt1_ref[...] * _dot(gated, w_o_ref[...])
    out_refs[0][...] = _ffn_tail(x1, g2_ref, sh2_ref, sc2_ref, gt2_ref, w1_ref, w2_ref, ff_chunk)


def _sgu_layer(x, mods, mod_spec, p, tm, ff_chunk, period, offset, emit_v):
    n, d = x.shape
    width = p["w_o"].shape[0]
    row = lambda w: pl.BlockSpec((tm, w), lambda i: (i, 0))
    cs = lambda a: _const_spec(a.shape)
    out_specs = [row(d)]
    out_shape = [jax.ShapeDtypeStruct((n, d), F32)]
    if emit_v:
        out_specs.append(row(width))
        out_shape.append(jax.ShapeDtypeStruct((n, width), F32))
    return pl.pallas_call(
        functools.partial(_sgu_layer_kernel, ff_chunk=ff_chunk, period=period, offset=offset, emit_v=emit_v),
        grid=(n // tm,),
        in_specs=[row(d), cs(p["g1"]), mod_spec(0), mod_spec(1), mod_spec(2), cs(p["w_in"]), cs(p["g_v"]),
                  cs(p["w_s"]), cs(p["b_s"]), cs(p["w_o"]), cs(p["g2"]), mod_spec(3), mod_spec(4), mod_spec(5),
                  _layer_spec(p["w1"], p["layer"]), _layer_spec(p["w2"], p["layer"])],
        out_specs=out_specs,
        out_shape=out_shape,
        compiler_params=_cparams(1),
        name="sgu_layer_ffn",
    )(x, p["g1"], mods, mods, mods, p["w_in"], p["g_v"], p["w_s"], p["b_s"], p["w_o"],
      p["g2"], mods, mods, mods, p["w1"], p["w2"])


def _rope_tables(pos):
    half = QK_ROPE_DIM // 2
    inv = 1.0 / (ROPE_THETA ** (jnp.arange(half, dtype=F32) / half))
    ang = pos.astype(F32)[:, None] * inv[None, :]
    cos, sin = jnp.cos(ang), jnp.sin(ang)
    z = jnp.zeros((pos.shape[0], LANES - QK_ROPE_DIM), F32)
    return jnp.concatenate([cos, cos, z], axis=-1), jnp.concatenate([-sin, sin, z], axis=-1)


def _pad_lanes(g, n):
    return jnp.pad(g, (0, n - g.shape[0])).reshape(1, n)


def _prep_mla(a, norm1_g, norm2_g, ffn_w1, ffn_w2, layer, mla_w_in, mla_g_qa, mla_g_kva, mla_w_q_up,
              mla_w_uk, mla_w_uv, mla_g_qn, mla_g_qr, mla_g_kn, mla_g_kr, mla_w_o):
    d = mla_w_in.shape[1]
    w_in = mla_w_in[a]
    w_in = jnp.concatenate([w_in[:, Q_LORA_RANK:Q_LORA_RANK + KV_LORA_RANK],
                            jnp.pad(w_in[:, Q_LORA_RANK + KV_LORA_RANK:], ((0, 0), (0, LANES - QK_ROPE_DIM))),
                            w_in[:, :Q_LORA_RANK]], axis=1).astype(BF16)
    half = QK_ROPE_DIM // 2
    w_q_up = mla_w_q_up[a].reshape(Q_LORA_RANK, N_HEADS, QK_NOPE_DIM + QK_ROPE_DIM)
    w_r1 = w_q_up[:, :, QK_NOPE_DIM:QK_NOPE_DIM + half]
    w_r2 = w_q_up[:, :, QK_NOPE_DIM + half:]
    w_z = jnp.zeros((Q_LORA_RANK, N_HEADS, LANES - QK_ROPE_DIM), w_q_up.dtype)
    w_q_up = jnp.concatenate([w_q_up[:, :, :QK_NOPE_DIM], w_r1, w_r2, w_z, w_r2, w_r1, w_z], axis=-1)
    g_qr = mla_g_qr[a]
    g_qr = jnp.concatenate([_pad_lanes(g_qr, LANES),
                            _pad_lanes(jnp.concatenate([g_qr[half:], g_qr[:half]]), LANES)], axis=0)
    return {
        "g1": norm1_g[layer].reshape(1, d), "g2": norm2_g[layer].reshape(1, d),
        "w1": ffn_w1, "w2": ffn_w2, "layer": layer,
        "w_in": w_in,
        "g_qa": mla_g_qa[a].reshape(1, -1), "g_kva": mla_g_kva[a].reshape(1, -1),
        "g_kr": _pad_lanes(mla_g_kr[a], LANES),
        "w_q_up": w_q_up.reshape(Q_LORA_RANK, N_HEADS * Q_HEAD_COLS).astype(BF16),
        "g_qn": mla_g_qn[a].reshape(1, -1), "g_qr": g_qr,
        "w_uk": mla_w_uk[a].reshape(KV_LORA_RANK, N_HEADS * QK_NOPE_DIM).astype(BF16),
        "g_kn": mla_g_kn[a].reshape(1, -1),
        "w_uv": jnp.swapaxes(mla_w_uv[a], 0, 1).astype(BF16),
        "w_o": mla_w_o[a].astype(BF16),
    }


def _prep_sgu(b, norm1_g, norm2_g, ffn_w1, ffn_w2, layer, cm_w_in, cm_g_v, cm_w_s, cm_b_s, cm_w_o, idx):
    d = cm_w_in.shape[1]
    period = len(idx)
    lo = int(idx[0])
    assert SGU_CHUNK % period == 0 and np.array_equal(idx, lo + np.arange(period)), idx
    reps = SGU_CHUNK // period
    w_s = jnp.tile(cm_w_s[b][:, lo:lo + period, lo:lo + period], (1, reps, reps))
    b_s = jnp.tile(cm_b_s[b][:, lo:lo + period], (1, reps))
    gdim = cm_w_o.shape[1] // SGU_GROUPS
    return {
        "g1": norm1_g[layer].reshape(1, d), "g2": norm2_g[layer].reshape(1, d),
        "w1": ffn_w1, "w2": ffn_w2, "layer": layer,
        "w_in": cm_w_in[b].astype(BF16), "g_v": cm_g_v[b].reshape(1, -1),
        "w_s": w_s, "b_s": jnp.broadcast_to(b_s[:, :, None], (SGU_GROUPS, SGU_CHUNK, gdim)),
        "w_o": cm_w_o[b].astype(BF16),
    }, period, lo


def kernel(x_prompt, x_sample, cache_ckv, cache_kpe, c_prompt, c_sample, ada_w, ada_b, norm1_g, norm2_g,
           ffn_w1, ffn_w2, mla_w_in, mla_g_qa, mla_g_kva, mla_w_q_up, mla_w_uk, mla_w_uv, mla_g_qn,
           mla_g_qr, mla_g_kn, mla_g_kr, mla_w_o, cm_w_in, cm_g_v, cm_w_s, cm_b_s, cm_w_o):
    batch, seq, d = x_prompt.shape
    dec_batch, dec_seq, _ = x_sample.shape
    past = cache_ckv.shape[2]
    depth = ada_w.shape[0]
    n_p, n_s = batch * seq, dec_batch * dec_seq
    tm_p, tm_proj, tm_s, attn_tile, attn_heads, ff_chunk = 512, 1024, n_s, 512, 4, 1024

    pos_p = np.arange(seq)
    pos_s = past + np.arange(dec_seq)
    rc_p, rs_p = _rope_tables(jnp.asarray(pos_p))
    rc_s, rs_s = _rope_tables(jnp.asarray(np.tile(pos_s, dec_batch)))
    rope_spec_p = pl.BlockSpec((tm_proj, LANES), lambda i: (i % (seq // tm_proj), 0))
    rope_spec_s = pl.BlockSpec((tm_s, LANES), lambda i: (i, 0))

    n_c = batch + dec_batch
    c_all = jnp.concatenate([c_prompt, c_sample], axis=0)
    c_all = jnp.pad(c_all, ((0, -n_c % BF16_ROWS), (0, 0)))
    mod = _ada_modulation(c_all, ada_w, ada_b)
    m_p = mod[:, :batch].reshape(depth, batch, 1, 6 * d)
    m_s = jnp.repeat(mod[:, batch:n_c], dec_seq, axis=1)

    def prompt_mods(layer, tm):
        return lambda j: pl.BlockSpec((None, None, 1, d), lambda i: (layer, i // (seq // tm), 0, j))

    def sample_mods(layer):
        return lambda j: pl.BlockSpec((None, tm_s, d), lambda i: (layer, i, j))

    x_p = x_prompt.reshape(n_p, d)
    x_s = x_sample.reshape(n_s, d)
    ckv_p_l, kpe_p_l, ckv_s_l, kpe_s_l, v_s_l = [], [], [], [], []
    for layer in range(depth):
        mod_spec_p, mod_spec_proj, mod_spec_s = prompt_mods(layer, tm_p), prompt_mods(layer, tm_proj), sample_mods(layer)
        if layer % 2 == 0:
            a = layer // 2
            p = _prep_mla(a, norm1_g, norm2_g, ffn_w1, ffn_w2, layer, mla_w_in, mla_g_qa, mla_g_kva,
                          mla_w_q_up, mla_w_uk, mla_w_uv, mla_g_qn, mla_g_qr, mla_g_kn, mla_g_kr, mla_w_o)
            side = (ffn_w1, ffn_w2) if ffn_w1.dtype != BF16 else ()
            q_p, k_p, ckv_p, _, ckvt_p, kr_p, *cast = _mla_project(x_p, m_p, mod_spec_proj, p, rc_p, rs_p,
                                                                   rope_spec_p, tm_proj, attn_tile, side)
            if side:
                ffn_w1, ffn_w2 = cast
                p["w1"], p["w2"] = cast
            q_s, k_s, ckv_s, ckvb_s, _, kr_s = _mla_project(x_s, m_s, mod_spec_s, p, rc_s, rs_s, rope_spec_s,
                                                            tm_s, attn_tile)
            ol_p = _prompt_attention(q_p, k_p, ckvt_p, batch, seq, attn_tile, attn_heads)
            ol_s = _sample_attention(q_s, k_s, ckvb_s, cache_ckv, cache_kpe, a, p["w_uk"].T, p["g_kn"],
                                     dec_seq, attn_tile)
            x_p = _mla_tail(x_p, ol_p, m_p, mod_spec_p, p, tm_p, ff_chunk, True)
            x_s = _mla_tail(x_s, ol_s, m_s, mod_spec_s, p, tm_s, ff_chunk, False)
            ckv_p_l.append(ckv_p.reshape(batch, seq, -1))
            kpe_p_l.append(kr_p.reshape(batch, seq, -1))
            ckv_s_l.append(ckv_s.reshape(dec_batch, dec_seq, -1))
            kpe_s_l.append(kr_s.reshape(dec_batch, dec_seq, -1))
        else:
            b = layer // 2
            args = (b, norm1_g, norm2_g, ffn_w1, ffn_w2, layer, cm_w_in, cm_g_v, cm_w_s, cm_b_s, cm_w_o)
            pp, period_p, off_p = _prep_sgu(*args, np.arange(SGU_CHUNK))
            ps, period_s, off_s = _prep_sgu(*args, pos_s % SGU_CHUNK)
            (x_p,) = _sgu_layer(x_p, m_p, mod_spec_p, pp, tm_p, ff_chunk, period_p, off_p, False)
            x_s, v_s = _sgu_layer(x_s, m_s, mod_spec_s, ps, tm_s, ff_chunk, period_s, off_s, True)
            v_s_l.append(v_s.reshape(dec_batch, dec_seq, -1))

    return (x_p.reshape(batch, seq, d), x_s.reshape(dec_batch, dec_seq, d),
            jnp.stack(ckv_p_l), jnp.stack(kpe_p_l), jnp.stack(ckv_s_l), jnp.stack(kpe_s_l),
            jnp.stack(v_s_l))
```

```python
import functools

import numpy as np
import jax
import jax.numpy as jnp
from jax import lax
from jax.experimental import pallas as pl
from jax.experimental.pallas import tpu as pltpu

F32 = jnp.float32
BF16 = jnp.bfloat16

CHUNK = 64
N_HEADS = 8
QK_NOPE_DIM = 128
QK_ROPE_DIM = 64
Q_LORA_RANK = 384
KV_LORA_RANK = 256
ROPE_THETA = 10000.0
SGU_CHUNK = 128
SGU_GROUPS = 8
EPS = 1e-6

LANES = 128
BF16_ROWS = 16
MXU_DIM = 256
HEAD_PAD = 2 * LANES
Q_HEAD_COLS = 3 * LANES
VMEM_LIMIT = 56 * 1024 * 1024

TOKEN_TILE = 512
PROJ_TILE = 1024
ATTN_TILE = 512
ATTN_HEADS = 4
FF_CHUNK = 1024
ADA_COLS = 1536

NEG_INF = float(np.finfo(np.float32).min)
Q_SCALE = float((QK_NOPE_DIM + QK_ROPE_DIM) ** -0.5 * np.log2(np.e))


def _cparams(n_axes):
    return pltpu.CompilerParams(
        dimension_semantics=("arbitrary",) * n_axes,
        vmem_limit_bytes=VMEM_LIMIT,
    )


def _const_spec(shape):
    nd = len(shape)
    return pl.BlockSpec(shape, lambda *_: (0,) * nd, pipeline_mode=pl.Buffered(1))


def _layer_spec(stacked, layer):
    return pl.BlockSpec((None,) + stacked.shape[1:], lambda *_: (layer, 0, 0), pipeline_mode=pl.Buffered(1))


def _dot(a, b):
    return jnp.dot(a, b, preferred_element_type=F32)


def _dot_nt(a, b):
    return lax.dot_general(a, b, (((1,), (1,)), ((), ())), preferred_element_type=F32)


def _rms(x, n):
    ms = jnp.sum(x * x, axis=-1, keepdims=True) / n
    return x * lax.rsqrt(ms + EPS)


def _rope128(x, c, s):
    half = QK_ROPE_DIM // 2
    rot = pltpu.roll(x, LANES - half, axis=1) + pltpu.roll(x, half, axis=1)
    return x * c + rot * s


def _mod_rows(ref, rows):
    v = ref[...]
    n = v.shape[0]
    if n in (1, rows):
        return v
    return jnp.broadcast_to(v[:, None, :], (n, rows // n, v.shape[1])).reshape(rows, v.shape[1])


def _mod_norm(x, g, shift, scale):
    return _rms(x, x.shape[-1]) * g * (1.0 + scale) + shift


def _sq_relu_ffn(hb, w1_ref, w2_ref, ff_chunk):
    d_ff = w1_ref.shape[1]
    acc = None
    for c in range(d_ff // ff_chunk):
        a = _dot(hb, w1_ref[:, c * ff_chunk:(c + 1) * ff_chunk])
        a = jnp.maximum(a, 0.0)
        a = (a * a).astype(BF16)
        part = _dot(a, w2_ref[c * ff_chunk:(c + 1) * ff_chunk, :])
        acc = part if acc is None else acc + part
    return acc


def _ada_kernel(c_ref, w_ref, b_ref, o_ref):
    s = jax.nn.silu(c_ref[...]).astype(BF16)
    o_ref[...] = _dot(s, w_ref[...].astype(BF16)) + b_ref[...]


def _ada_modulation(c_all, ada_w, ada_b):
    depth, d, n6 = ada_w.shape
    rows = c_all.shape[0]
    tn = ADA_COLS
    return pl.pallas_call(
        _ada_kernel,
        grid=(depth, n6 // tn),
        in_specs=[
            pl.BlockSpec((rows, d), lambda l, j: (0, 0)),
            pl.BlockSpec((None, d, tn), lambda l, j: (l, 0, j)),
            pl.BlockSpec((None, 1, tn), lambda l, j: (l, 0, j)),
        ],
        out_specs=pl.BlockSpec((None, rows, tn), lambda l, j: (l, 0, j)),
        out_shape=jax.ShapeDtypeStruct((depth, rows, n6), F32),
        compiler_params=_cparams(2),
        name="ada_modulation",
    )(c_all, ada_w, ada_b.reshape(depth, 1, n6))


def _mla_proj_kernel(x_ref, g1_ref, sh_ref, sc_ref, w_in_ref, g_qa_ref, g_kva_ref, g_kr_ref,
                     w_qup_ref, g_qn_ref, g_qr_ref, w_uk_ref, g_kn_ref, rc_ref, rs_ref, *refs):
    n_cast = (len(refs) - 6) // 2
    cast_in, (q_ref, k_ref, ckv_ref, ckvb_ref, ckvt_ref, kr_ref), cast_out = (
        refs[:n_cast], refs[n_cast:n_cast + 6], refs[n_cast + 6:])
    for src, dst in zip(cast_in, cast_out):
        dst[...] = src[...].astype(BF16)
    x = x_ref[...]
    tm = x.shape[0]
    h = _mod_norm(x, g1_ref[...], _mod_rows(sh_ref, tm), _mod_rows(sc_ref, tm)).astype(BF16)
    c0 = KV_LORA_RANK
    ckv = _rms(_dot(h, w_in_ref[:, :c0]), KV_LORA_RANK) * g_kva_ref[...]
    a = _dot(h, w_in_ref[:, c0:])
    cq = _rms(a[:, LANES:], Q_LORA_RANK) * g_qa_ref[...]
    rc = rc_ref[...]
    rs = rs_ref[...]
    kr = _rope128(_rms(a[:, :LANES], QK_ROPE_DIM) * g_kr_ref[...], rc, rs)
    ckv_ref[...] = ckv
    ckv_b = ckv.astype(BF16)
    ckvb_ref[...] = ckv_b
    vt_tile = ckvt_ref.shape[-1]
    for j in range(ckvt_ref.shape[0]):
        ckvt_ref[j] = ckv[j * vt_tile:(j + 1) * vt_tile, :].T.astype(BF16)
    kr_ref[...] = kr[:, :QK_ROPE_DIM]
    kr_b = kr.astype(BF16)

    kn_all = _dot(ckv_b, w_uk_ref[...])
    q = _dot(cq.astype(BF16), w_qup_ref[...])
    g_qn = g_qn_ref[...] * Q_SCALE
    rot_c = rc * (g_qr_ref[0:1, :] * Q_SCALE)
    rot_s = rs * (g_qr_ref[1:2, :] * Q_SCALE)
    g_kn = g_kn_ref[...]
    for hd in range(N_HEADS):
        kn = _rms(kn_all[:, hd * QK_NOPE_DIM:(hd + 1) * QK_NOPE_DIM], QK_NOPE_DIM) * g_kn
        k_ref[hd] = jnp.concatenate([kn.astype(BF16), kr_b], axis=-1)
    for hd in range(N_HEADS):
        o = hd * Q_HEAD_COLS
        qn = _rms(q[:, o:o + QK_NOPE_DIM], QK_NOPE_DIM) * g_qn
        x = q[:, o + QK_NOPE_DIM:o + 2 * LANES]
        x_sw = q[:, o + 2 * LANES:o + 3 * LANES]
        inv = lax.rsqrt(jnp.sum(x * x, axis=-1, keepdims=True) / QK_ROPE_DIM + EPS)
        qr = (x * rot_c + x_sw * rot_s) * inv
        q_ref[hd] = jnp.concatenate([qn.astype(BF16), qr.astype(BF16)], axis=-1)


def _mla_project(x, mods, mod_spec, p, rope_c, rope_s, rope_spec, tm, vt_tile, side_casts=()):
    n, d = x.shape
    steps = n // tm
    row = lambda w: pl.BlockSpec((tm, w), lambda i: (i, 0))
    hrow = pl.BlockSpec((N_HEADS, tm, HEAD_PAD), lambda i: (0, i, 0))
    consts = [p["w_in"], p["g_qa"], p["g_kva"], p["g_kr"], p["w_q_up"], p["g_qn"], p["g_qr"],
              p["w_uk"], p["g_kn"]]
    cast_specs = []
    for w in side_casts:
        layers, rows, cols = w.shape
        slab = layers * rows // steps
        assert slab * steps == layers * rows and rows % slab == 0 and slab % BF16_ROWS == 0, (w.shape, steps)
        per_layer = rows // slab
        cast_specs.append(pl.BlockSpec((None, slab, cols), lambda i, per_layer=per_layer: (i // per_layer, i % per_layer, 0)))
    return pl.pallas_call(
        _mla_proj_kernel,
        grid=(steps,),
        in_specs=[row(d), _const_spec(p["g1"].shape), mod_spec(0), mod_spec(1)]
        + [_const_spec(c.shape) for c in consts] + [rope_spec, rope_spec] + cast_specs,
        out_specs=[hrow, hrow, row(KV_LORA_RANK), row(KV_LORA_RANK),
                   pl.BlockSpec((tm // vt_tile, KV_LORA_RANK, vt_tile), lambda i: (i, 0, 0)), row(QK_ROPE_DIM)]
        + cast_specs,
        out_shape=[
            jax.ShapeDtypeStruct((N_HEADS, n, HEAD_PAD), BF16),
            jax.ShapeDtypeStruct((N_HEADS, n, HEAD_PAD), BF16),
            jax.ShapeDtypeStruct((n, KV_LORA_RANK), F32),
            jax.ShapeDtypeStruct((n, KV_LORA_RANK), BF16),
            jax.ShapeDtypeStruct((n // vt_tile, KV_LORA_RANK, vt_tile), BF16),
            jax.ShapeDtypeStruct((n, QK_ROPE_DIM), F32),
        ] + [jax.ShapeDtypeStruct(w.shape, BF16) for w in side_casts],
        compiler_params=_cparams(1),
        name="mla_project",
    )(x, p["g1"], mods, mods, *consts, rope_c, rope_s, *side_casts)


def _prompt_attn_kernel(q_ref, k_ref, vt_ref, o_ref, m_ref, l_ref, acc_ref, st_ref, *, tile, heads):
    qi = pl.program_id(2)
    m_ref[...] = jnp.full(m_ref.shape, NEG_INF, F32)
    l_ref[...] = jnp.zeros(l_ref.shape, F32)
    acc_ref[...] = jnp.zeros(acc_ref.shape, F32)

    first = 2

    def scores(kt, slot, gs=range(heads), qt=qi):
        k0 = kt * tile if isinstance(kt, int) else pl.multiple_of(kt * tile, tile)
        q0 = pl.multiple_of(qt * tile, tile)
        for g in gs:
            st_ref[slot, g] = _dot_nt(k_ref[g, pl.ds(k0, tile), :], q_ref[g, pl.ds(q0, tile), :])

    def step(kt, slot, masked, gs=range(heads)):
        vt = vt_ref[kt]
        for g in gs:
            for w in range(tile // MXU_DIM):
                nk = (w + 1) * MXU_DIM if masked else tile
                alphas, probs = [], []
                for j in range(w * MXU_DIM // LANES, (w + 1) * MXU_DIM // LANES):
                    cols = slice(j * LANES, (j + 1) * LANES)
                    col = st_ref[slot, g, :nk, cols]
                    if masked:
                        kc = lax.broadcasted_iota(jnp.int32, col.shape, 0) // CHUNK
                        qc = (lax.broadcasted_iota(jnp.int32, col.shape, 1) + j * LANES) // CHUNK
                        col = jnp.where(kc <= qc, col, NEG_INF)
                    m_prev = m_ref[g, :, cols]
                    m_new = jnp.maximum(m_prev, jnp.max(col, axis=0, keepdims=True))
                    alpha = jnp.exp2(m_prev - m_new)
                    pr = jnp.exp2(col - m_new)
                    l_ref[g, :, cols] = alpha * l_ref[g, :, cols] + jnp.sum(pr, axis=0, keepdims=True)
                    m_ref[g, :, cols] = m_new
                    probs.append(pr.astype(BF16))
                    alphas.append(alpha)
                wide = slice(w * MXU_DIM, (w + 1) * MXU_DIM)
                acc_ref[g, :, wide] = (acc_ref[g, :, wide] * jnp.concatenate(alphas, axis=-1)
                                       + _dot(vt[:, :nk], jnp.concatenate(probs, axis=-1)))

    nxt = jnp.minimum(qi + 1, pl.num_programs(2) - 1)

    def ahead(gs=range(heads)):
        scores(0, first, gs, qt=nxt)

    @pl.when(qi == 0)
    def _():
        scores(0, 1)
        for g in range(heads):
            ahead([g])
            step(0, 1, True, [g])

    @pl.when(qi > 0)
    def _():
        for g in range(heads):
            scores(1, 0, [g])
            step(0, first, False, [g])

        def body(i, carry):
            kt = 1 + 2 * i
            for g in range(heads):
                scores(kt + 1, 1, [g])
                step(kt, 0, False, [g])
            for g in range(heads):
                scores(kt + 2, 0, [g])
                step(kt + 1, 1, False, [g])
            return carry

        lax.fori_loop(0, (qi - 1) // 2, body, 0)

        @pl.when((qi - 1) % 2 == 0)
        def _():
            for g in range(heads):
                ahead([g])
                step(qi, 0, True, [g])

        @pl.when((qi - 1) % 2 == 1)
        def _():
            for g in range(heads):
                scores(qi, 1, [g])
                step(qi - 1, 0, False, [g])
            for g in range(heads):
                ahead([g])
                step(qi, 1, True, [g])

    for g in range(heads):
        o_ref[g * KV_LORA_RANK:(g + 1) * KV_LORA_RANK, :] = (acc_ref[g] / l_ref[g]).astype(o_ref.dtype)


def _prompt_attention(q_cat, k_cat, ckv_t, batch, seq, tile, heads):
    n = batch * seq
    nq = seq // tile
    assert ckv_t.shape == (n // tile, KV_LORA_RANK, tile), ckv_t.shape
    return pl.pallas_call(
        functools.partial(_prompt_attn_kernel, tile=tile, heads=heads),
        grid=(batch, N_HEADS // heads, nq),
        in_specs=[
            pl.BlockSpec((heads, seq, HEAD_PAD), lambda b, h, i: (h, b, 0)),
            pl.BlockSpec((heads, seq, HEAD_PAD), lambda b, h, i: (h, b, 0)),
            pl.BlockSpec((nq, KV_LORA_RANK, tile), lambda b, h, i: (b, 0, 0)),
        ],
        out_specs=pl.BlockSpec((heads * KV_LORA_RANK, tile), lambda b, h, i: (h, b * nq + i)),
        out_shape=jax.ShapeDtypeStruct((N_HEADS * KV_LORA_RANK, n), BF16),
        scratch_shapes=[
            pltpu.VMEM((heads, 1, tile), F32),
            pltpu.VMEM((heads, 1, tile), F32),
            pltpu.VMEM((heads, KV_LORA_RANK, tile), F32),
            pltpu.VMEM((3, heads, tile, tile), F32),
        ],
        compiler_params=_cparams(3),
        name="prompt_attention",
    )(q_cat, k_cat, ckv_t)


def _sample_attn_kernel(q_ref, kn_ref, ckvn_ref, cckv_ref, ckpe_ref, w_uk_ref, g_kn_ref, o_ref,
                        m_ref, l_ref, acc_ref, *, tile, past, n_new):
    m_ref[...] = jnp.full(m_ref.shape, NEG_INF, F32)
    l_ref[...] = jnp.zeros(l_ref.shape, F32)
    acc_ref[...] = jnp.zeros(acc_ref.shape, F32)
    g_kn = g_kn_ref[...]
    qr_all = jnp.concatenate([q_ref[hd][:, QK_NOPE_DIM:] for hd in range(N_HEADS)], axis=0)
    n_kn = N_HEADS * QK_NOPE_DIM
    q_lat = [_dot((q_ref[hd][:, :QK_NOPE_DIM] * g_kn).astype(BF16),
                  w_uk_ref[hd * QK_NOPE_DIM:(hd + 1) * QK_NOPE_DIM, :]).astype(BF16) for hd in range(N_HEADS)]
    w_and_q = jnp.concatenate([w_uk_ref[...]] + q_lat, axis=0)

    def update(s, v):
        m_prev = m_ref[...]
        m_new = jnp.maximum(m_prev, jnp.max(s, axis=-1, keepdims=True))
        alpha = jnp.exp2(m_prev - m_new)
        pr = jnp.exp2(s - m_new)
        l_ref[...] = alpha * l_ref[...] + jnp.sum(pr, axis=-1, keepdims=True)
        acc_ref[...] = alpha * acc_ref[...] + _dot(pr.astype(BF16), v)
        m_ref[...] = m_new

    def up_project(kt):
        v = cckv_ref[kt * tile:(kt + 1) * tile, :].astype(BF16)
        return v, _dot_nt(w_and_q, v)

    def attend(kt, v, kn_s):
        kn_t = kn_s[:n_kn, :]
        s_raw = kn_s[n_kn:, :]
        kpe_t = ckpe_ref[:, kt * tile:(kt + 1) * tile].astype(BF16)
        kpe_t = jnp.concatenate([kpe_t, jnp.zeros_like(kpe_t)], axis=0)
        s_rope = _dot(qr_all, kpe_t)
        inv = []
        for hd in range(N_HEADS):
            blk = kn_t[hd * QK_NOPE_DIM:(hd + 1) * QK_NOPE_DIM, :]
            ms = jnp.sum(blk * blk, axis=0, keepdims=True) / QK_NOPE_DIM
            inv.append(jnp.broadcast_to(lax.rsqrt(ms + EPS), (n_new, tile)))
        s = s_raw * jnp.concatenate(inv, axis=0) + s_rope
        update(s, v)

    n_tiles = past // tile
    nxt = up_project(0)
    for kt in range(n_tiles):
        cur = nxt
        if kt + 1 < n_tiles:
            nxt = up_project(kt + 1)
        attend(kt, *cur)

    s_new = jnp.concatenate([_dot_nt(q_ref[hd], kn_ref[hd]) for hd in range(N_HEADS)], axis=0)
    q_pos = past + lax.broadcasted_iota(jnp.int32, s_new.shape, 0) % n_new
    k_pos = past + lax.broadcasted_iota(jnp.int32, s_new.shape, 1)
    s_new = jnp.where(k_pos // CHUNK <= q_pos // CHUNK, s_new, NEG_INF)
    update(s_new, ckvn_ref[...])

    o = (acc_ref[...] / l_ref[...]).astype(o_ref.dtype)
    for hd in range(N_HEADS):
        o_ref[:, hd * KV_LORA_RANK:(hd + 1) * KV_LORA_RANK] = o[hd * n_new:(hd + 1) * n_new, :]


def _sample_attention(q_cat, k_cat, ckv_b, cache_ckv, cache_kpe, a, w_uk, g_kn, n_new, tile):
    _, dec_batch, past, _ = cache_ckv.shape
    n = dec_batch * n_new
    rows = N_HEADS * n_new
    hrow = pl.BlockSpec((N_HEADS, n_new, HEAD_PAD), lambda b: (0, b, 0))
    cache_kpe_t = jnp.swapaxes(cache_kpe, 2, 3)
    return pl.pallas_call(
        functools.partial(_sample_attn_kernel, tile=tile, past=past, n_new=n_new),
        grid=(dec_batch,),
        in_specs=[
            hrow, hrow,
            pl.BlockSpec((n_new, KV_LORA_RANK), lambda b: (b, 0)),
            pl.BlockSpec((None, None, past, KV_LORA_RANK), lambda b: (a, b, 0, 0)),
            pl.BlockSpec((None, None, QK_ROPE_DIM, past), lambda b: (a, b, 0, 0)),
            _const_spec(w_uk.shape), _const_spec(g_kn.shape),
        ],
        out_specs=pl.BlockSpec((n_new, N_HEADS * KV_LORA_RANK), lambda b: (b, 0)),
        out_shape=jax.ShapeDtypeStruct((n, N_HEADS * KV_LORA_RANK), BF16),
        scratch_shapes=[
            pltpu.VMEM((rows, 1), F32),
            pltpu.VMEM((rows, 1), F32),
            pltpu.VMEM((rows, KV_LORA_RANK), F32),
        ],
        compiler_params=_cparams(1),
        name="sample_attention",
    )(q_cat, k_cat, ckv_b, cache_ckv, cache_kpe_t, w_uk, g_kn)


def _ffn_tail(x1, g2_ref, sh2_ref, sc2_ref, gt2_ref, w1_ref, w2_ref, ff_chunk):
    tm = x1.shape[0]
    h2 = _mod_norm(x1, g2_ref[...], _mod_rows(sh2_ref, tm), _mod_rows(sc2_ref, tm)).astype(BF16)
    return x1 + _mod_rows(gt2_ref, tm) * _sq_relu_ffn(h2, w1_ref, w2_ref, ff_chunk)


def _mla_tail_kernel(x_ref, ol_ref, w_uv_ref, w_o_ref, gt1_ref, g2_ref, sh2_ref, sc2_ref, gt2_ref,
                     w1_ref, w2_ref, o_ref, *, ff_chunk, latent_major):
    heads = []
    for hd in range(N_HEADS):
        lat = slice(hd * KV_LORA_RANK, (hd + 1) * KV_LORA_RANK)
        if latent_major:
            up = lax.dot_general(ol_ref[lat, :], w_uv_ref[hd], (((0,), (0,)), ((), ())),
                                 preferred_element_type=F32)
        else:
            up = _dot(ol_ref[:, lat], w_uv_ref[hd])
        heads.append(up.astype(BF16))
    m = _dot(jnp.concatenate(heads, axis=-1), w_o_ref[...])
    x1 = x_ref[...] + _mod_rows(gt1_ref, m.shape[0]) * m
    o_ref[...] = _ffn_tail(x1, g2_ref, sh2_ref, sc2_ref, gt2_ref, w1_ref, w2_ref, ff_chunk)


def _mla_tail(x, o_lat, mods, mod_spec, p, tm, ff_chunk, latent_major):
    n, d = x.shape
    row = lambda w: pl.BlockSpec((tm, w), lambda i: (i, 0))
    consts_a = [p["w_uv"], p["w_o"]]
    consts_b = [p["w1"], p["w2"]]
    if latent_major:
        ol_spec = pl.BlockSpec((o_lat.shape[0], tm), lambda i: (0, i))
    else:
        ol_spec = row(o_lat.shape[1])
    return pl.pallas_call(
        functools.partial(_mla_tail_kernel, ff_chunk=ff_chunk, latent_major=latent_major),
        grid=(n // tm,),
        in_specs=[row(d), ol_spec] + [_const_spec(c.shape) for c in consts_a]
        + [mod_spec(2), _const_spec(p["g2"].shape), mod_spec(3), mod_spec(4), mod_spec(5)]
        + [_layer_spec(c, p["layer"]) for c in consts_b],
        out_specs=row(d),
        out_shape=jax.ShapeDtypeStruct((n, d), F32),
        compiler_params=_cparams(1),
        name="mla_tail_ffn",
    )(x, o_lat, *consts_a, mods, p["g2"], mods, mods, mods, *consts_b)


def _sgu_layer_kernel(x_ref, g1_ref, sh1_ref, sc1_ref, gt1_ref, w_in_ref, g_v_ref, w_s_ref, b_s_ref,
                      w_o_ref, g2_ref, sh2_ref, sc2_ref, gt2_ref, w1_ref, w2_ref, *out_refs,
                      ff_chunk, period, offset, emit_v):
    x = x_ref[...]
    tm = x.shape[0]
    width = w_o_ref.shape[0]
    gdim = width // SGU_GROUPS
    h = _mod_norm(x, g1_ref[...], _mod_rows(sh1_ref, tm), _mod_rows(sc1_ref, tm)).astype(BF16)
    v = _rms(jax.nn.gelu(_dot(h, w_in_ref[:, width:])), width) * g_v_ref[...]
    u = jax.nn.gelu(_dot(h, w_in_ref[:, :width]))
    if emit_v:
        out_refs[1][...] = v
    vb = v.astype(BF16)

    ri = lax.broadcasted_iota(jnp.int32, (SGU_CHUNK, SGU_CHUNK), 0)
    ci = lax.broadcasted_iota(jnp.int32, (SGU_CHUNK, SGU_CHUNK), 1)
    vis = (ri // period == ci // period) & ((ci % period + offset) // CHUNK <= (ri % period + offset) // CHUNK)
    w_mix = [jnp.where(vis, w_s_ref[g], 0.0).astype(BF16) for g in range(SGU_GROUPS)]

    rows = []
    for c in range(tm // SGU_CHUNK):
        r0 = c * SGU_CHUNK
        cols = []
        for g in range(SGU_GROUPS):
            l0 = g * gdim
            mixed = _dot(w_mix[g], vb[r0:r0 + SGU_CHUNK, l0:l0 + gdim]) + b_s_ref[g]
            cols.append((u[r0:r0 + SGU_CHUNK, l0:l0 + gdim] * mixed).astype(BF16))
        rows.append(jnp.concatenate(cols, axis=-1))
    gated = jnp.concatenate(rows, axis=0)
    x1 = x + _mod_rows(gt1_ref, tm) * _dot(gated, w_o_ref[...])
    out_refs[0][...] = _ffn_tail(x1, g2_ref, sh2_ref, sc2_ref, gt2_ref, w1_ref, w2_ref, ff_chunk)


def _sgu_layer(x, mods, mod_spec, p, tm, ff_chunk, period, offset, emit_v):
    n, d = x.shape
    width = p["w_o"].shape[0]
    row = lambda w: pl.BlockSpec((tm, w), lambda i: (i, 0))
    cs = lambda a: _const_spec(a.shape)
    out_specs = [row(d)]
    out_shape = [jax.ShapeDtypeStruct((n, d), F32)]
    if emit_v:
        out_specs.append(row(width))
        out_shape.append(jax.ShapeDtypeStruct((n, width), F32))
    return pl.pallas_call(
        functools.partial(_sgu_layer_kernel, ff_chunk=ff_chunk, period=period, offset=offset, emit_v=emit_v),
        grid=(n // tm,),
        in_specs=[row(d), cs(p["g1"]), mod_spec(0), mod_spec(1), mod_spec(2), cs(p["w_in"]), cs(p["g_v"]),
                  cs(p["w_s"]), cs(p["b_s"]), cs(p["w_o"]), cs(p["g2"]), mod_spec(3), mod_spec(4), mod_spec(5),
                  _layer_spec(p["w1"], p["layer"]), _layer_spec(p["w2"], p["layer"])],
        out_specs=out_specs,
        out_shape=out_shape,
        compiler_params=_cparams(1),
        name="sgu_layer_ffn",
    )(x, p["g1"], mods, mods, mods, p["w_in"], p["g_v"], p["w_s"], p["b_s"], p["w_o"],
      p["g2"], mods, mods, mods, p["w1"], p["w2"])


def _rope_tables(pos):
    half = QK_ROPE_DIM // 2
    inv = 1.0 / (ROPE_THETA ** (jnp.arange(half, dtype=F32) / half))
    ang = pos.astype(F32)[:, None] * inv[None, :]
    cos, sin = jnp.cos(ang), jnp.sin(ang)
    z = jnp.zeros((pos.shape[0], LANES - QK_ROPE_DIM), F32)
    return jnp.concatenate([cos, cos, z], axis=-1), jnp.concatenate([-sin, sin, z], axis=-1)


def _pad_lanes(g, n):
    return jnp.pad(g, (0, n - g.shape[0])).reshape(1, n)


def _prep_mla(a, norm1_g, norm2_g, ffn_w1, ffn_w2, layer, mla_w_in, mla_g_qa, mla_g_kva, mla_w_q_up,
              mla_w_uk, mla_w_uv, mla_g_qn, mla_g_qr, mla_g_kn, mla_g_kr, mla_w_o):
    d = mla_w_in.shape[1]
    w_in = mla_w_in[a]
    w_in = jnp.concatenate([w_in[:, Q_LORA_RANK:Q_LORA_RANK + KV_LORA_RANK],
                            jnp.pad(w_in[:, Q_LORA_RANK + KV_LORA_RANK:], ((0, 0), (0, LANES - QK_ROPE_DIM))),
                            w_in[:, :Q_LORA_RANK]], axis=1).astype(BF16)
    half = QK_ROPE_DIM // 2
    w_q_up = mla_w_q_up[a].reshape(Q_LORA_RANK, N_HEADS, QK_NOPE_DIM + QK_ROPE_DIM)
    w_r1 = w_q_up[:, :, QK_NOPE_DIM:QK_NOPE_DIM + half]
    w_r2 = w_q_up[:, :, QK_NOPE_DIM + half:]
    w_z = jnp.zeros((Q_LORA_RANK, N_HEADS, LANES - QK_ROPE_DIM), w_q_up.dtype)
    w_q_up = jnp.concatenate([w_q_up[:, :, :QK_NOPE_DIM], w_r1, w_r2, w_z, w_r2, w_r1, w_z], axis=-1)
    g_qr = mla_g_qr[a]
    g_qr = jnp.concatenate([_pad_lanes(g_qr, LANES),
                            _pad_lanes(jnp.concatenate([g_qr[half:], g_qr[:half]]), LANES)], axis=0)
    return {
        "g1": norm1_g[layer].reshape(1, d), "g2": norm2_g[layer].reshape(1, d),
        "w1": ffn_w1, "w2": ffn_w2, "layer": layer,
        "w_in": w_in,
        "g_qa": mla_g_qa[a].reshape(1, -1), "g_kva": mla_g_kva[a].reshape(1, -1),
        "g_kr": _pad_lanes(mla_g_kr[a], LANES),
        "w_q_up": w_q_up.reshape(Q_LORA_RANK, N_HEADS * Q_HEAD_COLS).astype(BF16),
        "g_qn": mla_g_qn[a].reshape(1, -1), "g_qr": g_qr,
        "w_uk": mla_w_uk[a].reshape(KV_LORA_RANK, N_HEADS * QK_NOPE_DIM).astype(BF16),
        "g_kn": mla_g_kn[a].reshape(1, -1),
        "w_uv": jnp.swapaxes(mla_w_uv[a], 0, 1).astype(BF16),
        "w_o": mla_w_o[a].astype(BF16),
    }


def _prep_sgu(b, norm1_g, norm2_g, ffn_w1, ffn_w2, layer, cm_w_in, cm_g_v, cm_w_s, cm_b_s, cm_w_o, idx):
    d = cm_w_in.shape[1]
    period = len(idx)
    lo = int(idx[0])
    assert SGU_CHUNK % period == 0 and np.array_equal(idx, lo + np.arange(period)), idx
    reps = SGU_CHUNK // period
    w_s = jnp.tile(cm_w_s[b][:, lo:lo + period, lo:lo + period], (1, reps, reps))
    b_s = jnp.tile(cm_b_s[b][:, lo:lo + period], (1, reps))
    gdim = cm_w_o.shape[1] // SGU_GROUPS
    return {
        "g1": norm1_g[layer].reshape(1, d), "g2": norm2_g[layer].reshape(1, d),
        "w1": ffn_w1, "w2": ffn_w2, "layer": layer,
        "w_in": cm_w_in[b].astype(BF16), "g_v": cm_g_v[b].reshape(1, -1),
        "w_s": w_s, "b_s": jnp.broadcast_to(b_s[:, :, None], (SGU_GROUPS, SGU_CHUNK, gdim)),
        "w_o": cm_w_o[b].astype(BF16),
    }, period, lo


def kernel(x_prompt, x_sample, cache_ckv, cache_kpe, c_prompt, c_sample, ada_w, ada_b, norm1_g, norm2_g,
           ffn_w1, ffn_w2, mla_w_in, mla_g_qa, mla_g_kva, mla_w_q_up, mla_w_uk, mla_w_uv, mla_g_qn,
           mla_g_qr, mla_g_kn, mla_g_kr, mla_w_o, cm_w_in, cm_g_v, cm_w_s, cm_b_s, cm_w_o):
    batch, seq, d = x_prompt.shape
    dec_batch, dec_seq, _ = x_sample.shape
    past = cache_ckv.shape[2]
    depth = ada_w.shape[0]
    n_p, n_s = batch * seq, dec_batch * dec_seq
    tm_p, tm_proj, tm_s = TOKEN_TILE, PROJ_TILE, n_s
    attn_tile, attn_heads, ff_chunk = ATTN_TILE, ATTN_HEADS, FF_CHUNK
    assert seq % tm_proj == 0 and seq % attn_tile == 0 and tm_p == attn_tile and n_s % attn_tile == 0
    assert past % attn_tile == 0 and N_HEADS % attn_heads == 0

    pos_p = np.arange(seq)
    pos_s = past + np.arange(dec_seq)
    rc_p, rs_p = _rope_tables(jnp.asarray(pos_p))
    rc_s, rs_s = _rope_tables(jnp.asarray(np.tile(pos_s, dec_batch)))
    rope_spec_p = pl.BlockSpec((tm_proj, LANES), lambda i: (i % (seq // tm_proj), 0))
    rope_spec_s = pl.BlockSpec((tm_s, LANES), lambda i: (i, 0))

    n_c = batch + dec_batch
    c_all = jnp.concatenate([c_prompt, c_sample], axis=0)
    c_all = jnp.pad(c_all, ((0, -n_c % BF16_ROWS), (0, 0)))
    mod = _ada_modulation(c_all, ada_w, ada_b)
    m_p = mod[:, :batch].reshape(depth, batch, 1, 6 * d)
    m_s = mod[:, batch:n_c]

    def prompt_mods(layer, tm):
        return lambda j: pl.BlockSpec((None, None, 1, d), lambda i: (layer, i // (seq // tm), 0, j))

    def sample_mods(layer):
        return lambda j: pl.BlockSpec((None, tm_s // dec_seq, d), lambda i: (layer, i, j))

    x_p = x_prompt.reshape(n_p, d)
    x_s = x_sample.reshape(n_s, d)
    ckv_p_l, kpe_p_l, ckv_s_l, kpe_s_l, v_s_l = [], [], [], [], []
    for layer in range(depth):
        mod_spec_p, mod_spec_proj, mod_spec_s = prompt_mods(layer, tm_p), prompt_mods(layer, tm_proj), sample_mods(layer)
        if layer % 2 == 0:
            a = layer // 2
            p = _prep_mla(a, norm1_g, norm2_g, ffn_w1, ffn_w2, layer, mla_w_in, mla_g_qa, mla_g_kva,
                          mla_w_q_up, mla_w_uk, mla_w_uv, mla_g_qn, mla_g_qr, mla_g_kn, mla_g_kr, mla_w_o)
            side = (ffn_w1, ffn_w2) if ffn_w1.dtype != BF16 else ()
            q_p, k_p, ckv_p, _, ckvt_p, kr_p, *cast = _mla_project(x_p, m_p, mod_spec_proj, p, rc_p, rs_p,
                                                                   rope_spec_p, tm_proj, attn_tile, side)
            if side:
                ffn_w1, ffn_w2 = cast
                p["w1"], p["w2"] = cast
            q_s, k_s, ckv_s, ckvb_s, _, kr_s = _mla_project(x_s, m_s, mod_spec_s, p, rc_s, rs_s, rope_spec_s,
                                                            tm_s, attn_tile)
            ol_p = _prompt_attention(q_p, k_p, ckvt_p, batch, seq, attn_tile, attn_heads)
            ol_s = _sample_attention(q_s, k_s, ckvb_s, cache_ckv, cache_kpe, a, p["w_uk"].T, p["g_kn"],
                                     dec_seq, attn_tile)
            x_p = _mla_tail(x_p, ol_p, m_p, mod_spec_p, p, tm_p, ff_chunk, True)
            x_s = _mla_tail(x_s, ol_s, m_s, mod_spec_s, p, tm_s, ff_chunk, False)
            ckv_p_l.append(ckv_p.reshape(batch, seq, -1))
            kpe_p_l.append(kr_p.reshape(batch, seq, -1))
            ckv_s_l.append(ckv_s.reshape(dec_batch, dec_seq, -1))
            kpe_s_l.append(kr_s.reshape(dec_batch, dec_seq, -1))
        else:
            b = layer // 2
            args = (b, norm1_g, norm2_g, ffn_w1, ffn_w2, layer, cm_w_in, cm_g_v, cm_w_s, cm_b_s, cm_w_o)
            pp, period_p, off_p = _prep_sgu(*args, np.arange(SGU_CHUNK))
            ps, period_s, off_s = _prep_sgu(*args, pos_s % SGU_CHUNK)
            (x_p,) = _sgu_layer(x_p, m_p, mod_spec_p, pp, tm_p, ff_chunk, period_p, off_p, False)
            x_s, v_s = _sgu_layer(x_s, m_s, mod_spec_s, ps, tm_s, ff_chunk, period_s, off_s, True)
            v_s_l.append(v_s.reshape(dec_batch, dec_seq, -1))

    return (x_p.reshape(batch, seq, d), x_s.reshape(dec_batch, dec_seq, d),
            jnp.stack(ckv_p_l), jnp.stack(kpe_p_l), jnp.stack(ckv_s_l), jnp.stack(kpe_s_l),
            jnp.stack(v_s_l))
```

```python
import functools

import numpy as np
import jax
import jax.numpy as jnp
from jax import lax
from jax.experimental import pallas as pl
from jax.experimental.pallas import tpu as pltpu

F32 = jnp.float32
BF16 = jnp.bfloat16

CHUNK = 64
N_HEADS = 8
QK_NOPE_DIM = 128
QK_ROPE_DIM = 64
Q_LORA_RANK = 384
KV_LORA_RANK = 256
ROPE_THETA = 10000.0
SGU_CHUNK = 128
SGU_GROUPS = 8
EPS = 1e-6

LANES = 128
BF16_ROWS = 16
MXU_DIM = 256
HEAD_PAD = 2 * LANES
Q_HEAD_COLS = 3 * LANES
VMEM_LIMIT = 56 * 1024 * 1024

TOKEN_TILE = 512
PROJ_TILE = 1024
ATTN_TILE = 512
ATTN_HEADS = 4
FF_CHUNK = 1024
ADA_COLS = 1536

NEG_INF = float(np.finfo(np.float32).min)
Q_SCALE = float((QK_NOPE_DIM + QK_ROPE_DIM) ** -0.5 * np.log2(np.e))


def _cparams(n_axes):
    return pltpu.CompilerParams(
        dimension_semantics=("arbitrary",) * n_axes,
        vmem_limit_bytes=VMEM_LIMIT,
    )


def _const_spec(shape):
    nd = len(shape)
    return pl.BlockSpec(shape, lambda *_: (0,) * nd, pipeline_mode=pl.Buffered(1))


def _layer_spec(stacked, layer):
    return pl.BlockSpec((None,) + stacked.shape[1:], lambda *_: (layer, 0, 0), pipeline_mode=pl.Buffered(1))


def _dot(a, b):
    return jnp.dot(a, b, preferred_element_type=F32)


def _dot_nt(a, b):
    return lax.dot_general(a, b, (((1,), (1,)), ((), ())), preferred_element_type=F32)


def _rms(x, n):
    ms = jnp.sum(x * x, axis=-1, keepdims=True) / n
    return x * lax.rsqrt(ms + EPS)


def _rope128(x, c, s):
    half = QK_ROPE_DIM // 2
    rot = pltpu.roll(x, LANES - half, axis=1) + pltpu.roll(x, half, axis=1)
    return x * c + rot * s


def _mod_rows(ref, rows):
    v = ref[...]
    n = v.shape[0]
    if n in (1, rows):
        return v
    return jnp.broadcast_to(v[:, None, :], (n, rows // n, v.shape[1])).reshape(rows, v.shape[1])


def _mod_norm(x, g, shift, scale):
    return _rms(x, x.shape[-1]) * g * (1.0 + scale) + shift


def _sq_relu_ffn(hb, w1_ref, w2_ref, ff_chunk):
    d_ff = w1_ref.shape[1]
    acc = None
    for c in range(d_ff // ff_chunk):
        a = _dot(hb, w1_ref[:, c * ff_chunk:(c + 1) * ff_chunk])
        a = jnp.maximum(a, 0.0)
        a = (a * a).astype(BF16)
        part = _dot(a, w2_ref[c * ff_chunk:(c + 1) * ff_chunk, :])
        acc = part if acc is None else acc + part
    return acc


def _ada_kernel(c_ref, w_ref, b_ref, o_ref):
    s = jax.nn.silu(c_ref[...]).astype(BF16)
    o_ref[...] = _dot(s, w_ref[...].astype(BF16)) + b_ref[...]


def _ada_modulation(c_all, ada_w, ada_b):
    depth, d, n6 = ada_w.shape
    rows = c_all.shape[0]
    tn = ADA_COLS
    return pl.pallas_call(
        _ada_kernel,
        grid=(depth, n6 // tn),
        in_specs=[
            pl.BlockSpec((rows, d), lambda l, j: (0, 0)),
            pl.BlockSpec((None, d, tn), lambda l, j: (l, 0, j)),
            pl.BlockSpec((None, 1, tn), lambda l, j: (l, 0, j)),
        ],
        out_specs=pl.BlockSpec((None, rows, tn), lambda l, j: (l, 0, j)),
        out_shape=jax.ShapeDtypeStruct((depth, rows, n6), F32),
        compiler_params=_cparams(2),
        name="ada_modulation",
    )(c_all, ada_w, ada_b.reshape(depth, 1, n6))


def _mla_proj_kernel(x_ref, g1_ref, sh_ref, sc_ref, w_in_ref, g_qa_ref, g_kva_ref, g_kr_ref,
                     w_qup_ref, g_qn_ref, g_qr_ref, w_uk_ref, g_kn_ref, rc_ref, rs_ref, *refs):
    n_cast = (len(refs) - 6) // 2
    cast_in, (q_ref, k_ref, ckv_ref, ckvb_ref, ckvt_ref, kr_ref), cast_out = (
        refs[:n_cast], refs[n_cast:n_cast + 6], refs[n_cast + 6:])
    for src, dst in zip(cast_in, cast_out):
        dst[...] = src[...].astype(BF16)
    x = x_ref[...]
    tm = x.shape[0]
    h = _mod_norm(x, g1_ref[...], _mod_rows(sh_ref, tm), _mod_rows(sc_ref, tm)).astype(BF16)
    c0 = KV_LORA_RANK
    ckv = _rms(_dot(h, w_in_ref[:, :c0]), KV_LORA_RANK) * g_kva_ref[...]
    a = _dot(h, w_in_ref[:, c0:])
    cq = _rms(a[:, LANES:], Q_LORA_RANK) * g_qa_ref[...]
    rc = rc_ref[...]
    rs = rs_ref[...]
    kr = _rope128(_rms(a[:, :LANES], QK_ROPE_DIM) * g_kr_ref[...], rc, rs)
    ckv_ref[...] = ckv
    ckv_b = ckv.astype(BF16)
    ckvb_ref[...] = ckv_b
    vt_tile = ckvt_ref.shape[-1]
    for j in range(ckvt_ref.shape[0]):
        ckvt_ref[j] = ckv[j * vt_tile:(j + 1) * vt_tile, :].T.astype(BF16)
    kr_ref[...] = kr[:, :QK_ROPE_DIM]
    kr_b = kr.astype(BF16)

    kn_all = _dot(ckv_b, w_uk_ref[...])
    q = _dot(cq.astype(BF16), w_qup_ref[...])
    g_qn = g_qn_ref[...] * Q_SCALE
    rot_c = rc * (g_qr_ref[0:1, :] * Q_SCALE)
    rot_s = rs * (g_qr_ref[1:2, :] * Q_SCALE)
    g_kn = g_kn_ref[...]
    for hd in range(N_HEADS):
        kn = _rms(kn_all[:, hd * QK_NOPE_DIM:(hd + 1) * QK_NOPE_DIM], QK_NOPE_DIM) * g_kn
        k_ref[hd] = jnp.concatenate([kn.astype(BF16), kr_b], axis=-1)
    for hd in range(N_HEADS):
        o = hd * Q_HEAD_COLS
        qn = _rms(q[:, o:o + QK_NOPE_DIM], QK_NOPE_DIM) * g_qn
        x = q[:, o + QK_NOPE_DIM:o + 2 * LANES]
        x_sw = q[:, o + 2 * LANES:o + 3 * LANES]
        inv = lax.rsqrt(jnp.sum(x * x, axis=-1, keepdims=True) / QK_ROPE_DIM + EPS)
        qr = (x * rot_c + x_sw * rot_s) * inv
        q_ref[hd] = jnp.concatenate([qn.astype(BF16), qr.astype(BF16)], axis=-1)


def _mla_project(x, mods, mod_spec, p, rope_c, rope_s, rope_spec, tm, vt_tile, side_casts=()):
    n, d = x.shape
    steps = n // tm
    row = lambda w: pl.BlockSpec((tm, w), lambda i: (i, 0))
    hrow = pl.BlockSpec((N_HEADS, tm, HEAD_PAD), lambda i: (0, i, 0))
    consts = [p["w_in"], p["g_qa"], p["g_kva"], p["g_kr"], p["w_q_up"], p["g_qn"], p["g_qr"],
              p["w_uk"], p["g_kn"]]
    cast_specs = []
    for w in side_casts:
        layers, rows, cols = w.shape
        slab = layers * rows // steps
        assert slab * steps == layers * rows and rows % slab == 0 and slab % BF16_ROWS == 0, (w.shape, steps)
        per_layer = rows // slab
        cast_specs.append(pl.BlockSpec((None, slab, cols), lambda i, per_layer=per_layer: (i // per_layer, i % per_layer, 0)))
    return pl.pallas_call(
        _mla_proj_kernel,
        grid=(steps,),
        in_specs=[row(d), _const_spec(p["g1"].shape), mod_spec(0), mod_spec(1)]
        + [_const_spec(c.shape) for c in consts] + [rope_spec, rope_spec] + cast_specs,
        out_specs=[hrow, hrow, row(KV_LORA_RANK), row(KV_LORA_RANK),
                   pl.BlockSpec((tm // vt_tile, KV_LORA_RANK, vt_tile), lambda i: (i, 0, 0)), row(QK_ROPE_DIM)]
        + cast_specs,
        out_shape=[
            jax.ShapeDtypeStruct((N_HEADS, n, HEAD_PAD), BF16),
            jax.ShapeDtypeStruct((N_HEADS, n, HEAD_PAD), BF16),
            jax.ShapeDtypeStruct((n, KV_LORA_RANK), F32),
            jax.ShapeDtypeStruct((n, KV_LORA_RANK), BF16),
            jax.ShapeDtypeStruct((n // vt_tile, KV_LORA_RANK, vt_tile), BF16),
            jax.ShapeDtypeStruct((n, QK_ROPE_DIM), F32),
        ] + [jax.ShapeDtypeStruct(w.shape, BF16) for w in side_casts],
        compiler_params=_cparams(1),
        name="mla_project",
    )(x, p["g1"], mods, mods, *consts, rope_c, rope_s, *side_casts)


def _prompt_attn_kernel(q_ref, k_ref, vt_ref, o_ref, m_ref, l_ref, acc_ref, st_ref, *, tile, heads):
    qi = pl.program_id(2)
    m_ref[...] = jnp.full(m_ref.shape, NEG_INF, F32)
    l_ref[...] = jnp.zeros(l_ref.shape, F32)
    acc_ref[...] = jnp.zeros(acc_ref.shape, F32)

    first = 2

    def scores(kt, slot, gs=range(heads), qt=qi):
        k0 = kt * tile if isinstance(kt, int) else pl.multiple_of(kt * tile, tile)
        q0 = pl.multiple_of(qt * tile, tile)
        for g in gs:
            st_ref[slot, g] = _dot_nt(k_ref[g, pl.ds(k0, tile), :], q_ref[g, pl.ds(q0, tile), :])

    def step(kt, slot, masked, gs=range(heads)):
        vt = vt_ref[kt]
        for g in gs:
            for w in range(tile // MXU_DIM):
                nk = (w + 1) * MXU_DIM if masked else tile
                alphas, probs = [], []
                for j in range(w * MXU_DIM // LANES, (w + 1) * MXU_DIM // LANES):
                    cols = slice(j * LANES, (j + 1) * LANES)
                    col = st_ref[slot, g, :nk, cols]
                    if masked:
                        kc = lax.broadcasted_iota(jnp.int32, col.shape, 0) // CHUNK
                        qc = (lax.broadcasted_iota(jnp.int32, col.shape, 1) + j * LANES) // CHUNK
                        col = jnp.where(kc <= qc, col, NEG_INF)
                    m_prev = m_ref[g, :, cols]
                    m_new = jnp.maximum(m_prev, jnp.max(col, axis=0, keepdims=True))
                    alpha = jnp.exp2(m_prev - m_new)
                    pr = jnp.exp2(col - m_new)
                    l_ref[g, :, cols] = alpha * l_ref[g, :, cols] + jnp.sum(pr, axis=0, keepdims=True)
                    m_ref[g, :, cols] = m_new
                    probs.append(pr.astype(BF16))
                    alphas.append(alpha)
                wide = slice(w * MXU_DIM, (w + 1) * MXU_DIM)
                acc_ref[g, :, wide] = (acc_ref[g, :, wide] * jnp.concatenate(alphas, axis=-1)
                                       + _dot(vt[:, :nk], jnp.concatenate(probs, axis=-1)))

    nxt = jnp.minimum(qi + 1, pl.num_programs(2) - 1)

    def ahead(gs=range(heads)):
        scores(0, first, gs, qt=nxt)

    @pl.when(qi == 0)
    def _():
        scores(0, 1)
        for g in range(heads):
            ahead([g])
            step(0, 1, True, [g])

    @pl.when(qi > 0)
    def _():
        for g in range(heads):
            scores(1, 0, [g])
            step(0, first, False, [g])

        def body(i, carry):
            kt = 1 + 2 * i
            for g in range(heads):
                scores(kt + 1, 1, [g])
                step(kt, 0, False, [g])
            for g in range(heads):
                scores(kt + 2, 0, [g])
                step(kt + 1, 1, False, [g])
            return carry

        lax.fori_loop(0, (qi - 1) // 2, body, 0)

        @pl.when((qi - 1) % 2 == 0)
        def _():
            for g in range(heads):
                ahead([g])
                step(qi, 0, True, [g])

        @pl.when((qi - 1) % 2 == 1)
        def _():
            for g in range(heads):
                scores(qi, 1, [g])
                step(qi - 1, 0, False, [g])
            for g in range(heads):
                ahead([g])
                step(qi, 1, True, [g])

    for g in range(heads):
        o_ref[g * KV_LORA_RANK:(g + 1) * KV_LORA_RANK, :] = (acc_ref[g] / l_ref[g]).astype(o_ref.dtype)


def _prompt_attention(q_cat, k_cat, ckv_t, batch, seq, tile, heads):
    n = batch * seq
    nq = seq // tile
    assert ckv_t.shape == (n // tile, KV_LORA_RANK, tile), ckv_t.shape
    return pl.pallas_call(
        functools.partial(_prompt_attn_kernel, tile=tile, heads=heads),
        grid=(batch, N_HEADS // heads, nq),
        in_specs=[
            pl.BlockSpec((heads, seq, HEAD_PAD), lambda b, h, i: (h, b, 0)),
            pl.BlockSpec((heads, seq, HEAD_PAD), lambda b, h, i: (h, b, 0)),
            pl.BlockSpec((nq, KV_LORA_RANK, tile), lambda b, h, i: (b, 0, 0)),
        ],
        out_specs=pl.BlockSpec((heads * KV_LORA_RANK, tile), lambda b, h, i: (h, b * nq + i)),
        out_shape=jax.ShapeDtypeStruct((N_HEADS * KV_LORA_RANK, n), BF16),
        scratch_shapes=[
            pltpu.VMEM((heads, 1, tile), F32),
            pltpu.VMEM((heads, 1, tile), F32),
            pltpu.VMEM((heads, KV_LORA_RANK, tile), F32),
            pltpu.VMEM((3, heads, tile, tile), F32),
        ],
        compiler_params=_cparams(3),
        name="prompt_attention",
    )(q_cat, k_cat, ckv_t)


def _sample_attn_kernel(q_ref, kn_ref, ckvn_ref, cckv_ref, ckpe_ref, w_uk_ref, g_kn_ref, o_ref,
                        m_ref, l_ref, acc_ref, *, tile, past, n_new):
    m_ref[...] = jnp.full(m_ref.shape, NEG_INF, F32)
    l_ref[...] = jnp.zeros(l_ref.shape, F32)
    acc_ref[...] = jnp.zeros(acc_ref.shape, F32)
    g_kn = g_kn_ref[...]
    qr_all = jnp.concatenate([q_ref[hd][:, QK_NOPE_DIM:] for hd in range(N_HEADS)], axis=0)
    n_kn = N_HEADS * QK_NOPE_DIM
    q_lat = [_dot((q_ref[hd][:, :QK_NOPE_DIM] * g_kn).astype(BF16),
                  w_uk_ref[hd * QK_NOPE_DIM:(hd + 1) * QK_NOPE_DIM, :]).astype(BF16) for hd in range(N_HEADS)]
    w_and_q = jnp.concatenate([w_uk_ref[...]] + q_lat, axis=0)

    def update(s, v):
        m_prev = m_ref[...]
        m_new = jnp.maximum(m_prev, jnp.max(s, axis=-1, keepdims=True))
        alpha = jnp.exp2(m_prev - m_new)
        pr = jnp.exp2(s - m_new)
        l_ref[...] = alpha * l_ref[...] + jnp.sum(pr, axis=-1, keepdims=True)
        acc_ref[...] = alpha * acc_ref[...] + _dot(pr.astype(BF16), v)
        m_ref[...] = m_new

    def up_project(kt):
        v = cckv_ref[kt * tile:(kt + 1) * tile, :].astype(BF16)
        return v, _dot_nt(w_and_q, v)

    def attend(kt, v, kn_s):
        kn_t = kn_s[:n_kn, :]
        s_raw = kn_s[n_kn:, :]
        kpe_t = ckpe_ref[:, kt * tile:(kt + 1) * tile].astype(BF16)
        kpe_t = jnp.concatenate([kpe_t, jnp.zeros_like(kpe_t)], axis=0)
        s_rope = _dot(qr_all, kpe_t)
        inv = []
        for hd in range(N_HEADS):
            blk = kn_t[hd * QK_NOPE_DIM:(hd + 1) * QK_NOPE_DIM, :]
            ms = jnp.sum(blk * blk, axis=0, keepdims=True) / QK_NOPE_DIM
            inv.append(jnp.broadcast_to(lax.rsqrt(ms + EPS), (n_new, tile)))
        s = s_raw * jnp.concatenate(inv, axis=0) + s_rope
        update(s, v)

    n_tiles = past // tile
    nxt = up_project(0)
    for kt in range(n_tiles):
        cur = nxt
        if kt + 1 < n_tiles:
            nxt = up_project(kt + 1)
        attend(kt, *cur)

    s_new = jnp.concatenate([_dot_nt(q_ref[hd], kn_ref[hd]) for hd in range(N_HEADS)], axis=0)
    q_pos = past + lax.broadcasted_iota(jnp.int32, s_new.shape, 0) % n_new
    k_pos = past + lax.broadcasted_iota(jnp.int32, s_new.shape, 1)
    s_new = jnp.where(k_pos // CHUNK <= q_pos // CHUNK, s_new, NEG_INF)
    update(s_new, ckvn_ref[...])

    o = (acc_ref[...] / l_ref[...]).astype(o_ref.dtype)
    for hd in range(N_HEADS):
        o_ref[:, hd * KV_LORA_RANK:(hd + 1) * KV_LORA_RANK] = o[hd * n_new:(hd + 1) * n_new, :]


def _sample_attention(q_cat, k_cat, ckv_b, cache_ckv, cache_kpe, a, w_uk, g_kn, n_new, tile):
    _, dec_batch, past, _ = cache_ckv.shape
    n = dec_batch * n_new
    rows = N_HEADS * n_new
    hrow = pl.BlockSpec((N_HEADS, n_new, HEAD_PAD), lambda b: (0, b, 0))
    cache_kpe_t = jnp.swapaxes(cache_kpe, 2, 3)
    return pl.pallas_call(
        functools.partial(_sample_attn_kernel, tile=tile, past=past, n_new=n_new),
        grid=(dec_batch,),
        in_specs=[
            hrow, hrow,
            pl.BlockSpec((n_new, KV_LORA_RANK), lambda b: (b, 0)),
            pl.BlockSpec((None, None, past, KV_LORA_RANK), lambda b: (a, b, 0, 0)),
            pl.BlockSpec((None, None, QK_ROPE_DIM, past), lambda b: (a, b, 0, 0)),
            _const_spec(w_uk.shape), _const_spec(g_kn.shape),
        ],
        out_specs=pl.BlockSpec((n_new, N_HEADS * KV_LORA_RANK), lambda b: (b, 0)),
        out_shape=jax.ShapeDtypeStruct((n, N_HEADS * KV_LORA_RANK), BF16),
        scratch_shapes=[
            pltpu.VMEM((rows, 1), F32),
            pltpu.VMEM((rows, 1), F32),
            pltpu.VMEM((rows, KV_LORA_RANK), F32),
        ],
        compiler_params=_cparams(1),
        name="sample_attention",
    )(q_cat, k_cat, ckv_b, cache_ckv, cache_kpe_t, w_uk, g_kn)


def _ffn_tail(x1, g2_ref, sh2_ref, sc2_ref, gt2_ref, w1_ref, w2_ref, ff_chunk):
    tm = x1.shape[0]
    h2 = _mod_norm(x1, g2_ref[...], _mod_rows(sh2_ref, tm), _mod_rows(sc2_ref, tm)).astype(BF16)
    return x1 + _mod_rows(gt2_ref, tm) * _sq_relu_ffn(h2, w1_ref, w2_ref, ff_chunk)


def _mla_tail_kernel(x_ref, ol_ref, w_uv_ref, w_o_ref, gt1_ref, g2_ref, sh2_ref, sc2_ref, gt2_ref,
                     w1_ref, w2_ref, o_ref, *, ff_chunk, latent_major):
    heads = []
    for hd in range(N_HEADS):
        lat = slice(hd * KV_LORA_RANK, (hd + 1) * KV_LORA_RANK)
        if latent_major:
            up = lax.dot_general(ol_ref[lat, :], w_uv_ref[hd], (((0,), (0,)), ((), ())),
                                 preferred_element_type=F32)
        else:
            up = _dot(ol_ref[:, lat], w_uv_ref[hd])
        heads.append(up.astype(BF16))
    m = _dot(jnp.concatenate(heads, axis=-1), w_o_ref[...])
    x1 = x_ref[...] + _mod_rows(gt1_ref, m.shape[0]) * m
    o_ref[...] = _ffn_tail(x1, g2_ref, sh2_ref, sc2_ref, gt2_ref, w1_ref, w2_ref, ff_chunk)


def _mla_tail(x, o_lat, mods, mod_spec, p, tm, ff_chunk, latent_major):
    n, d = x.shape
    row = lambda w: pl.BlockSpec((tm, w), lambda i: (i, 0))
    consts_a = [p["w_uv"], p["w_o"]]
    consts_b = [p["w1"], p["w2"]]
    if latent_major:
        ol_spec = pl.BlockSpec((o_lat.shape[0], tm), lambda i: (0, i))
    else:
        ol_spec = row(o_lat.shape[1])
    return pl.pallas_call(
        functools.partial(_mla_tail_kernel, ff_chunk=ff_chunk, latent_major=latent_major),
        grid=(n // tm,),
        in_specs=[row(d), ol_spec] + [_const_spec(c.shape) for c in consts_a]
        + [mod_spec(2), _const_spec(p["g2"].shape), mod_spec(3), mod_spec(4), mod_spec(5)]
        + [_layer_spec(c, p["layer"]) for c in consts_b],
        out_specs=row(d),
        out_shape=jax.ShapeDtypeStruct((n, d), F32),
        compiler_params=_cparams(1),
        name="mla_tail_ffn",
    )(x, o_lat, *consts_a, mods, p["g2"], mods, mods, mods, *consts_b)


def _sgu_layer_kernel(x_ref, g1_ref, sh1_ref, sc1_ref, gt1_ref, w_in_ref, g_v_ref, w_s_ref, b_s_ref,
                      w_o_ref, g2_ref, sh2_ref, sc2_ref, gt2_ref, w1_ref, w2_ref, *out_refs,
                      ff_chunk, period, offset, emit_v):
    x = x_ref[...]
    tm = x.shape[0]
    width = w_o_ref.shape[0]
    gdim = width // SGU_GROUPS
    h = _mod_norm(x, g1_ref[...], _mod_rows(sh1_ref, tm), _mod_rows(sc1_ref, tm)).astype(BF16)
    v = _rms(jax.nn.gelu(_dot(h, w_in_ref[:, width:])), width) * g_v_ref[...]
    u = jax.nn.gelu(_dot(h, w_in_ref[:, :width]))
    if emit_v:
        out_refs[1][...] = v
    vb = v.astype(BF16)

    ri = lax.broadcasted_iota(jnp.int32, (SGU_CHUNK, SGU_CHUNK), 0)
    ci = lax.broadcasted_iota(jnp.int32, (SGU_CHUNK, SGU_CHUNK), 1)
    vis = (ri // period == ci // period) & ((ci % period + offset) // CHUNK <= (ri % period + offset) // CHUNK)
    w_mix = [jnp.where(vis, w_s_ref[g], 0.0).astype(BF16) for g in range(SGU_GROUPS)]

    rows = []
    for c in range(tm // SGU_CHUNK):
        r0 = c * SGU_CHUNK
        cols = []
        for g in range(SGU_GROUPS):
            l0 = g * gdim
            mixed = _dot(w_mix[g], vb[r0:r0 + SGU_CHUNK, l0:l0 + gdim]) + b_s_ref[g]
            cols.append((u[r0:r0 + SGU_CHUNK, l0:l0 + gdim] * mixed).astype(BF16))
        rows.append(jnp.concatenate(cols, axis=-1))
    gated = jnp.concatenate(rows, axis=0)
    x1 = x + _mod_rows(gt1_ref, tm) * _dot(gated, w_o_ref[...])
    out_refs[0][...] = _ffn_tail(x1, g2_ref, sh2_ref, sc2_ref, gt2_ref, w1_ref, w2_ref, ff_chunk)


def _sgu_layer(x, mods, mod_spec, p, tm, ff_chunk, period, offset, emit_v):
    n, d = x.shape
    width = p["w_o"].shape[0]
    row = lambda w: pl.BlockSpec((tm, w), lambda i: (i, 0))
    cs = lambda a: _const_spec(a.shape)
    out_specs = [row(d)]
    out_shape = [jax.ShapeDtypeStruct((n, d), F32)]
    if emit_v:
        out_specs.append(row(width))
        out_shape.append(jax.ShapeDtypeStruct((n, width), F32))
    return pl.pallas_call(
        functools.partial(_sgu_layer_kernel, ff_chunk=ff_chunk, period=period, offset=offset, emit_v=emit_v),
        grid=(n // tm,),
        in_specs=[row(d), cs(p["g1"]), mod_spec(0), mod_spec(1), mod_spec(2), cs(p["w_in"]), cs(p["g_v"]),
                  cs(p["w_s"]), cs(p["b_s"]), cs(p["w_o"]), cs(p["g2"]), mod_spec(3), mod_spec(4), mod_spec(5),
                  _layer_spec(p["w1"], p["layer"]), _layer_spec(p["w2"], p["layer"])],
        out_specs=out_specs,
        out_shape=out_shape,
        compiler_params=_cparams(1),
        name="sgu_layer_ffn",
    )(x, p["g1"], mods, mods, mods, p["w_in"], p["g_v"], p["w_s"], p["b_s"], p["w_o"],
      p["g2"], mods, mods, mods, p["w1"], p["w2"])


def _rope_tables(pos):
    half = QK_ROPE_DIM // 2
    inv = 1.0 / (ROPE_THETA ** (jnp.arange(half, dtype=F32) / half))
    n = pos.shape[0]
    step = CHUNK
    if n % step == 0 and np.array_equal(pos, np.arange(n)):
        ang_a = (step * jnp.arange(n // step, dtype=F32))[:, None] * inv[None, :]
        ang_b = jnp.arange(step, dtype=F32)[:, None] * inv[None, :]
        ca, sa = jnp.cos(ang_a)[:, None, :], jnp.sin(ang_a)[:, None, :]
        cb, sb = jnp.cos(ang_b)[None, :, :], jnp.sin(ang_b)[None, :, :]
        cos = (ca * cb - sa * sb).reshape(n, half)
        sin = (sa * cb + ca * sb).reshape(n, half)
    else:
        ang = jnp.asarray(pos).astype(F32)[:, None] * inv[None, :]
        cos, sin = jnp.cos(ang), jnp.sin(ang)
    z = jnp.zeros((pos.shape[0], LANES - QK_ROPE_DIM), F32)
    return jnp.concatenate([cos, cos, z], axis=-1), jnp.concatenate([-sin, sin, z], axis=-1)


def _pad_lanes(g, n):
    return jnp.pad(g, (0, n - g.shape[0])).reshape(1, n)


def _prep_mla(a, norm1_g, norm2_g, ffn_w1, ffn_w2, layer, mla_w_in, mla_g_qa, mla_g_kva, mla_w_q_up,
              mla_w_uk, mla_w_uv, mla_g_qn, mla_g_qr, mla_g_kn, mla_g_kr, mla_w_o):
    d = mla_w_in.shape[1]
    w_in = mla_w_in[a]
    w_in = jnp.concatenate([w_in[:, Q_LORA_RANK:Q_LORA_RANK + KV_LORA_RANK],
                            jnp.pad(w_in[:, Q_LORA_RANK + KV_LORA_RANK:], ((0, 0), (0, LANES - QK_ROPE_DIM))),
                            w_in[:, :Q_LORA_RANK]], axis=1).astype(BF16)
    half = QK_ROPE_DIM // 2
    w_q_up = mla_w_q_up[a].reshape(Q_LORA_RANK, N_HEADS, QK_NOPE_DIM + QK_ROPE_DIM)
    w_r1 = w_q_up[:, :, QK_NOPE_DIM:QK_NOPE_DIM + half]
    w_r2 = w_q_up[:, :, QK_NOPE_DIM + half:]
    w_z = jnp.zeros((Q_LORA_RANK, N_HEADS, LANES - QK_ROPE_DIM), w_q_up.dtype)
    w_q_up = jnp.concatenate([w_q_up[:, :, :QK_NOPE_DIM], w_r1, w_r2, w_z, w_r2, w_r1, w_z], axis=-1)
    g_qr = mla_g_qr[a]
    g_qr = jnp.concatenate([_pad_lanes(g_qr, LANES),
                            _pad_lanes(jnp.concatenate([g_qr[half:], g_qr[:half]]), LANES)], axis=0)
    return {
        "g1": norm1_g[layer].reshape(1, d), "g2": norm2_g[layer].reshape(1, d),
        "w1": ffn_w1, "w2": ffn_w2, "layer": layer,
        "w_in": w_in,
        "g_qa": mla_g_qa[a].reshape(1, -1), "g_kva": mla_g_kva[a].reshape(1, -1),
        "g_kr": _pad_lanes(mla_g_kr[a], LANES),
        "w_q_up": w_q_up.reshape(Q_LORA_RANK, N_HEADS * Q_HEAD_COLS).astype(BF16),
        "g_qn": mla_g_qn[a].reshape(1, -1), "g_qr": g_qr,
        "w_uk": mla_w_uk[a].reshape(KV_LORA_RANK, N_HEADS * QK_NOPE_DIM).astype(BF16),
        "g_kn": mla_g_kn[a].reshape(1, -1),
        "w_uv": jnp.swapaxes(mla_w_uv[a], 0, 1).astype(BF16),
        "w_o": mla_w_o[a].astype(BF16),
    }


def _prep_sgu(b, norm1_g, norm2_g, ffn_w1, ffn_w2, layer, cm_w_in, cm_g_v, cm_w_s, cm_b_s, cm_w_o, idx):
    d = cm_w_in.shape[1]
    period = len(idx)
    lo = int(idx[0])
    assert SGU_CHUNK % period == 0 and np.array_equal(idx, lo + np.arange(period)), idx
    reps = SGU_CHUNK // period
    w_s = jnp.tile(cm_w_s[b][:, lo:lo + period, lo:lo + period], (1, reps, reps))
    b_s = jnp.tile(cm_b_s[b][:, lo:lo + period], (1, reps))
    gdim = cm_w_o.shape[1] // SGU_GROUPS
    return {
        "g1": norm1_g[layer].reshape(1, d), "g2": norm2_g[layer].reshape(1, d),
        "w1": ffn_w1, "w2": ffn_w2, "layer": layer,
        "w_in": cm_w_in[b].astype(BF16), "g_v": cm_g_v[b].reshape(1, -1),
        "w_s": w_s, "b_s": jnp.broadcast_to(b_s[:, :, None], (SGU_GROUPS, SGU_CHUNK, gdim)),
        "w_o": cm_w_o[b].astype(BF16),
    }, period, lo


def kernel(x_prompt, x_sample, cache_ckv, cache_kpe, c_prompt, c_sample, ada_w, ada_b, norm1_g, norm2_g,
           ffn_w1, ffn_w2, mla_w_in, mla_g_qa, mla_g_kva, mla_w_q_up, mla_w_uk, mla_w_uv, mla_g_qn,
           mla_g_qr, mla_g_kn, mla_g_kr, mla_w_o, cm_w_in, cm_g_v, cm_w_s, cm_b_s, cm_w_o):
    batch, seq, d = x_prompt.shape
    dec_batch, dec_seq, _ = x_sample.shape
    past = cache_ckv.shape[2]
    depth = ada_w.shape[0]
    n_p, n_s = batch * seq, dec_batch * dec_seq
    tm_p, tm_proj, tm_s = TOKEN_TILE, PROJ_TILE, n_s
    attn_tile, attn_heads, ff_chunk = ATTN_TILE, ATTN_HEADS, FF_CHUNK
    assert seq % tm_proj == 0 and seq % attn_tile == 0 and tm_p == attn_tile and n_s % attn_tile == 0
    assert past % attn_tile == 0 and N_HEADS % attn_heads == 0

    pos_p = np.arange(seq)
    pos_s = past + np.arange(dec_seq)
    rc_p, rs_p = _rope_tables(pos_p)
    rc_s, rs_s = _rope_tables(np.tile(pos_s, dec_batch))
    rope_spec_p = pl.BlockSpec((tm_proj, LANES), lambda i: (i % (seq // tm_proj), 0))
    rope_spec_s = pl.BlockSpec((tm_s, LANES), lambda i: (i, 0))

    n_c = batch + dec_batch
    c_all = jnp.concatenate([c_prompt, c_sample], axis=0)
    c_all = jnp.pad(c_all, ((0, -n_c % BF16_ROWS), (0, 0)))
    mod = _ada_modulation(c_all, ada_w, ada_b)
    m_p = mod[:, :batch].reshape(depth, batch, 1, 6 * d)
    m_s = mod[:, batch:n_c]

    def prompt_mods(layer, tm):
        return lambda j: pl.BlockSpec((None, None, 1, d), lambda i: (layer, i // (seq // tm), 0, j))

    def sample_mods(layer):
        return lambda j: pl.BlockSpec((None, tm_s // dec_seq, d), lambda i: (layer, i, j))

    x_p = x_prompt.reshape(n_p, d)
    x_s = x_sample.reshape(n_s, d)
    ckv_p_l, kpe_p_l, ckv_s_l, kpe_s_l, v_s_l = [], [], [], [], []
    for layer in range(depth):
        mod_spec_p, mod_spec_proj, mod_spec_s = prompt_mods(layer, tm_p), prompt_mods(layer, tm_proj), sample_mods(layer)
        if layer % 2 == 0:
            a = layer // 2
            p = _prep_mla(a, norm1_g, norm2_g, ffn_w1, ffn_w2, layer, mla_w_in, mla_g_qa, mla_g_kva,
                          mla_w_q_up, mla_w_uk, mla_w_uv, mla_g_qn, mla_g_qr, mla_g_kn, mla_g_kr, mla_w_o)
            side = (ffn_w1, ffn_w2, mla_w_o, cm_w_in, cm_w_o) if ffn_w1.dtype != BF16 else ()
            q_p, k_p, ckv_p, _, ckvt_p, kr_p, *cast = _mla_project(x_p, m_p, mod_spec_proj, p, rc_p, rs_p,
                                                                   rope_spec_p, tm_proj, attn_tile, side)
            if side:
                ffn_w1, ffn_w2, mla_w_o, cm_w_in, cm_w_o = cast
                p["w1"], p["w2"], p["w_o"] = ffn_w1, ffn_w2, mla_w_o[a]
            q_s, k_s, ckv_s, ckvb_s, _, kr_s = _mla_project(x_s, m_s, mod_spec_s, p, rc_s, rs_s, rope_spec_s,
                                                            tm_s, attn_tile)
            ol_p = _prompt_attention(q_p, k_p, ckvt_p, batch, seq, attn_tile, attn_heads)
            ol_s = _sample_attention(q_s, k_s, ckvb_s, cache_ckv, cache_kpe, a, p["w_uk"].T, p["g_kn"],
                                     dec_seq, attn_tile)
            x_p = _mla_tail(x_p, ol_p, m_p, mod_spec_p, p, tm_p, ff_chunk, True)
            x_s = _mla_tail(x_s, ol_s, m_s, mod_spec_s, p, tm_s, ff_chunk, False)
            ckv_p_l.append(ckv_p.reshape(batch, seq, -1))
            kpe_p_l.append(kr_p.reshape(batch, seq, -1))
            ckv_s_l.append(ckv_s.reshape(dec_batch, dec_seq, -1))
            kpe_s_l.append(kr_s.reshape(dec_batch, dec_seq, -1))
        else:
            b = layer // 2
            args = (b, norm1_g, norm2_g, ffn_w1, ffn_w2, layer, cm_w_in, cm_g_v, cm_w_s, cm_b_s, cm_w_o)
            pp, period_p, off_p = _prep_sgu(*args, np.arange(SGU_CHUNK))
            ps, period_s, off_s = _prep_sgu(*args, pos_s % SGU_CHUNK)
            (x_p,) = _sgu_layer(x_p, m_p, mod_spec_p, pp, tm_p, ff_chunk, period_p, off_p, False)
            x_s, v_s = _sgu_layer(x_s, m_s, mod_spec_s, ps, tm_s, ff_chunk, period_s, off_s, True)
            v_s_l.append(v_s.reshape(dec_batch, dec_seq, -1))

    return (x_p.reshape(batch, seq, d), x_s.reshape(dec_batch, dec_seq, d),
            jnp.stack(ckv_p_l), jnp.stack(kpe_p_l), jnp.stack(ckv_s_l), jnp.stack(kpe_s_l),
            jnp.stack(v_s_l))
```

```python
import functools

import numpy as np
import jax
import jax.numpy as jnp
from jax import lax
from jax.experimental import pallas as pl
from jax.experimental.pallas import tpu as pltpu

F32 = jnp.float32
BF16 = jnp.bfloat16

CHUNK = 64
N_HEADS = 8
QK_NOPE_DIM = 128
QK_ROPE_DIM = 64
Q_LORA_RANK = 384
KV_LORA_RANK = 256
ROPE_THETA = 10000.0
SGU_CHUNK = 128
SGU_GROUPS = 8
EPS = 1e-6

LANES = 128
BF16_ROWS = 16
MXU_DIM = 256
HEAD_PAD = 2 * LANES
Q_HEAD_COLS = 3 * LANES
VMEM_LIMIT = 56 * 1024 * 1024

TOKEN_TILE = 512
PROJ_TILE = 1024
ATTN_TILE = 512
CACHE_TILE = 2048
ATTN_HEADS = 4
FF_CHUNK = 1024
ADA_COLS = 1536

NEG_INF = float(np.finfo(np.float32).min)
Q_SCALE = float((QK_NOPE_DIM + QK_ROPE_DIM) ** -0.5 * np.log2(np.e))


def _cparams(n_axes):
    return pltpu.CompilerParams(
        dimension_semantics=("arbitrary",) * n_axes,
        vmem_limit_bytes=VMEM_LIMIT,
    )


def _const_spec(shape):
    nd = len(shape)
    return pl.BlockSpec(shape, lambda *_: (0,) * nd, pipeline_mode=pl.Buffered(1))


def _layer_spec(stacked, layer):
    return pl.BlockSpec((None,) + stacked.shape[1:], lambda *_: (layer, 0, 0), pipeline_mode=pl.Buffered(1))


def _dot(a, b):
    return jnp.dot(a, b, preferred_element_type=F32)


def _dot_nt(a, b):
    return lax.dot_general(a, b, (((1,), (1,)), ((), ())), preferred_element_type=F32)


def _rms(x, n):
    ms = jnp.sum(x * x, axis=-1, keepdims=True) / n
    return x * lax.rsqrt(ms + EPS)


def _rope128(x, c, s):
    half = QK_ROPE_DIM // 2
    rot = pltpu.roll(x, LANES - half, axis=1) + pltpu.roll(x, half, axis=1)
    return x * c + rot * s


def _mod_rows(ref, rows):
    v = ref[...]
    n = v.shape[0]
    if n in (1, rows):
        return v
    return jnp.broadcast_to(v[:, None, :], (n, rows // n, v.shape[1])).reshape(rows, v.shape[1])


def _mod_norm(x, g, shift, scale):
    return _rms(x, x.shape[-1]) * g * (1.0 + scale) + shift


def _sq_relu_ffn(hb, w1_ref, w2_ref, ff_chunk):
    d_ff = w1_ref.shape[1]
    acc = None
    for c in range(d_ff // ff_chunk):
        a = _dot(hb, w1_ref[:, c * ff_chunk:(c + 1) * ff_chunk])
        a = jnp.maximum(a, 0.0)
        a = (a * a).astype(BF16)
        part = _dot(a, w2_ref[c * ff_chunk:(c + 1) * ff_chunk, :])
        acc = part if acc is None else acc + part
    return acc


def _ada_kernel(c_ref, w_ref, b_ref, o_ref):
    s = jax.nn.silu(c_ref[...]).astype(BF16)
    o_ref[...] = _dot(s, w_ref[...].astype(BF16)) + b_ref[...]


def _ada_modulation(c_all, ada_w, ada_b):
    depth, d, n6 = ada_w.shape
    rows = c_all.shape[0]
    tn = ADA_COLS
    return pl.pallas_call(
        _ada_kernel,
        grid=(depth, n6 // tn),
        in_specs=[
            pl.BlockSpec((rows, d), lambda l, j: (0, 0)),
            pl.BlockSpec((None, d, tn), lambda l, j: (l, 0, j)),
            pl.BlockSpec((None, 1, tn), lambda l, j: (l, 0, j)),
        ],
        out_specs=pl.BlockSpec((None, rows, tn), lambda l, j: (l, 0, j)),
        out_shape=jax.ShapeDtypeStruct((depth, rows, n6), F32),
        compiler_params=_cparams(2),
        name="ada_modulation",
    )(c_all, ada_w, ada_b.reshape(depth, 1, n6))


def _mla_proj_kernel(x_ref, g1_ref, sh_ref, sc_ref, w_in_ref, g_qa_ref, g_kva_ref, g_kr_ref,
                     w_qup_ref, g_qn_ref, g_qr_ref, w_uk_ref, g_kn_ref, rc_ref, rs_ref, *refs):
    n_cast = (len(refs) - 6) // 2
    cast_in, (q_ref, k_ref, ckv_ref, ckvb_ref, ckvt_ref, kr_ref), cast_out = (
        refs[:n_cast], refs[n_cast:n_cast + 6], refs[n_cast + 6:])
    for src, dst in zip(cast_in, cast_out):
        dst[...] = src[...].astype(BF16)
    x = x_ref[...]
    tm = x.shape[0]
    h = _mod_norm(x, g1_ref[...], _mod_rows(sh_ref, tm), _mod_rows(sc_ref, tm)).astype(BF16)
    c0 = KV_LORA_RANK
    ckv = _rms(_dot(h, w_in_ref[:, :c0]), KV_LORA_RANK) * g_kva_ref[...]
    a = _dot(h, w_in_ref[:, c0:])
    cq = _rms(a[:, LANES:], Q_LORA_RANK) * g_qa_ref[...]
    rc = rc_ref[...]
    rs = rs_ref[...]
    kr = _rope128(_rms(a[:, :LANES], QK_ROPE_DIM) * g_kr_ref[...], rc, rs)
    ckv_ref[...] = ckv
    ckv_b = ckv.astype(BF16)
    ckvb_ref[...] = ckv_b
    vt_tile = ckvt_ref.shape[-1]
    for j in range(ckvt_ref.shape[0]):
        ckvt_ref[j] = ckv[j * vt_tile:(j + 1) * vt_tile, :].T.astype(BF16)
    kr_ref[...] = kr[:, :QK_ROPE_DIM]
    kr_b = kr.astype(BF16)

    kn_all = _dot(ckv_b, w_uk_ref[...])
    q = _dot(cq.astype(BF16), w_qup_ref[...])
    g_qn = g_qn_ref[...] * Q_SCALE
    rot_c = rc * (g_qr_ref[0:1, :] * Q_SCALE)
    rot_s = rs * (g_qr_ref[1:2, :] * Q_SCALE)
    g_kn = g_kn_ref[...]
    for hd in range(N_HEADS):
        kn = _rms(kn_all[:, hd * QK_NOPE_DIM:(hd + 1) * QK_NOPE_DIM], QK_NOPE_DIM) * g_kn
        k_ref[hd] = jnp.concatenate([kn.astype(BF16), kr_b], axis=-1)
    for hd in range(N_HEADS):
        o = hd * Q_HEAD_COLS
        qn = _rms(q[:, o:o + QK_NOPE_DIM], QK_NOPE_DIM) * g_qn
        x = q[:, o + QK_NOPE_DIM:o + 2 * LANES]
        x_sw = q[:, o + 2 * LANES:o + 3 * LANES]
        inv = lax.rsqrt(jnp.sum(x * x, axis=-1, keepdims=True) / QK_ROPE_DIM + EPS)
        qr = (x * rot_c + x_sw * rot_s) * inv
        q_ref[hd] = jnp.concatenate([qn.astype(BF16), qr.astype(BF16)], axis=-1)


def _mla_project(x, mods, mod_spec, p, rope_c, rope_s, rope_spec, tm, vt_tile, side_casts=()):
    n, d = x.shape
    steps = n // tm
    row = lambda w: pl.BlockSpec((tm, w), lambda i: (i, 0))
    hrow = pl.BlockSpec((N_HEADS, tm, HEAD_PAD), lambda i: (0, i, 0))
    consts = [p["w_in"], p["g_qa"], p["g_kva"], p["g_kr"], p["w_q_up"], p["g_qn"], p["g_qr"],
              p["w_uk"], p["g_kn"]]
    cast_specs = []
    for w in side_casts:
        layers, rows, cols = w.shape
        slab = layers * rows // steps
        assert slab * steps == layers * rows and rows % slab == 0 and slab % BF16_ROWS == 0, (w.shape, steps)
        per_layer = rows // slab
        cast_specs.append(pl.BlockSpec((None, slab, cols), lambda i, per_layer=per_layer: (i // per_layer, i % per_layer, 0)))
    return pl.pallas_call(
        _mla_proj_kernel,
        grid=(steps,),
        in_specs=[row(d), _const_spec(p["g1"].shape), mod_spec(0), mod_spec(1)]
        + [_const_spec(c.shape) for c in consts] + [rope_spec, rope_spec] + cast_specs,
        out_specs=[hrow, hrow, row(KV_LORA_RANK), row(KV_LORA_RANK),
                   pl.BlockSpec((tm // vt_tile, KV_LORA_RANK, vt_tile), lambda i: (i, 0, 0)), row(QK_ROPE_DIM)]
        + cast_specs,
        out_shape=[
            jax.ShapeDtypeStruct((N_HEADS, n, HEAD_PAD), BF16),
            jax.ShapeDtypeStruct((N_HEADS, n, HEAD_PAD), BF16),
            jax.ShapeDtypeStruct((n, KV_LORA_RANK), F32),
            jax.ShapeDtypeStruct((n, KV_LORA_RANK), BF16),
            jax.ShapeDtypeStruct((n // vt_tile, KV_LORA_RANK, vt_tile), BF16),
            jax.ShapeDtypeStruct((n, QK_ROPE_DIM), F32),
        ] + [jax.ShapeDtypeStruct(w.shape, BF16) for w in side_casts],
        compiler_params=_cparams(1),
        name="mla_project",
    )(x, p["g1"], mods, mods, *consts, rope_c, rope_s, *side_casts)


def _prompt_attn_kernel(q_ref, k_ref, vt_ref, o_ref, m_ref, l_ref, acc_ref, st_ref, *, tile, heads):
    qi = pl.program_id(2)
    m_ref[...] = jnp.full(m_ref.shape, NEG_INF, F32)
    l_ref[...] = jnp.zeros(l_ref.shape, F32)
    acc_ref[...] = jnp.zeros(acc_ref.shape, F32)

    first = 2

    def scores(kt, slot, gs=range(heads), qt=qi):
        k0 = kt * tile if isinstance(kt, int) else pl.multiple_of(kt * tile, tile)
        q0 = pl.multiple_of(qt * tile, tile)
        for g in gs:
            st_ref[slot, g] = _dot_nt(k_ref[g, pl.ds(k0, tile), :], q_ref[g, pl.ds(q0, tile), :])

    def step(kt, slot, masked, gs=range(heads)):
        vt = vt_ref[kt]
        for g in gs:
            for w in range(tile // MXU_DIM):
                nk = (w + 1) * MXU_DIM if masked else tile
                alphas, probs = [], []
                for j in range(w * MXU_DIM // LANES, (w + 1) * MXU_DIM // LANES):
                    cols = slice(j * LANES, (j + 1) * LANES)
                    col = st_ref[slot, g, :nk, cols]
                    if masked:
                        kc = lax.broadcasted_iota(jnp.int32, col.shape, 0) // CHUNK
                        qc = (lax.broadcasted_iota(jnp.int32, col.shape, 1) + j * LANES) // CHUNK
                        col = jnp.where(kc <= qc, col, NEG_INF)
                    m_prev = m_ref[g, :, cols]
                    m_new = jnp.maximum(m_prev, jnp.max(col, axis=0, keepdims=True))
                    alpha = jnp.exp2(m_prev - m_new)
                    pr = jnp.exp2(col - m_new)
                    l_ref[g, :, cols] = alpha * l_ref[g, :, cols] + jnp.sum(pr, axis=0, keepdims=True)
                    m_ref[g, :, cols] = m_new
                    probs.append(pr.astype(BF16))
                    alphas.append(alpha)
                wide = slice(w * MXU_DIM, (w + 1) * MXU_DIM)
                acc_ref[g, :, wide] = (acc_ref[g, :, wide] * jnp.concatenate(alphas, axis=-1)
                                       + _dot(vt[:, :nk], jnp.concatenate(probs, axis=-1)))

    nxt = jnp.minimum(qi + 1, pl.num_programs(2) - 1)

    def ahead(gs=range(heads)):
        scores(0, first, gs, qt=nxt)

    @pl.when(qi == 0)
    def _():
        scores(0, 1)
        for g in range(heads):
            ahead([g])
            step(0, 1, True, [g])

    @pl.when(qi > 0)
    def _():
        for g in range(heads):
            scores(1, 0, [g])
            step(0, first, False, [g])

        def body(i, carry):
            kt = 1 + 2 * i
            for g in range(heads):
                scores(kt + 1, 1, [g])
                step(kt, 0, False, [g])
            for g in range(heads):
                scores(kt + 2, 0, [g])
                step(kt + 1, 1, False, [g])
            return carry

        lax.fori_loop(0, (qi - 1) // 2, body, 0)

        @pl.when((qi - 1) % 2 == 0)
        def _():
            for g in range(heads):
                ahead([g])
                step(qi, 0, True, [g])

        @pl.when((qi - 1) % 2 == 1)
        def _():
            for g in range(heads):
                scores(qi, 1, [g])
                step(qi - 1, 0, False, [g])
            for g in range(heads):
                ahead([g])
                step(qi, 1, True, [g])

    for g in range(heads):
        o_ref[g * KV_LORA_RANK:(g + 1) * KV_LORA_RANK, :] = (acc_ref[g] / l_ref[g]).astype(o_ref.dtype)


def _prompt_attention(q_cat, k_cat, ckv_t, batch, seq, tile, heads):
    n = batch * seq
    nq = seq // tile
    assert ckv_t.shape == (n // tile, KV_LORA_RANK, tile), ckv_t.shape
    return pl.pallas_call(
        functools.partial(_prompt_attn_kernel, tile=tile, heads=heads),
        grid=(batch, N_HEADS // heads, nq),
        in_specs=[
            pl.BlockSpec((heads, seq, HEAD_PAD), lambda b, h, i: (h, b, 0)),
            pl.BlockSpec((heads, seq, HEAD_PAD), lambda b, h, i: (h, b, 0)),
            pl.BlockSpec((nq, KV_LORA_RANK, tile), lambda b, h, i: (b, 0, 0)),
        ],
        out_specs=pl.BlockSpec((heads * KV_LORA_RANK, tile), lambda b, h, i: (h, b * nq + i)),
        out_shape=jax.ShapeDtypeStruct((N_HEADS * KV_LORA_RANK, n), BF16),
        scratch_shapes=[
            pltpu.VMEM((heads, 1, tile), F32),
            pltpu.VMEM((heads, 1, tile), F32),
            pltpu.VMEM((heads, KV_LORA_RANK, tile), F32),
            pltpu.VMEM((3, heads, tile, tile), F32),
        ],
        compiler_params=_cparams(3),
        name="prompt_attention",
    )(q_cat, k_cat, ckv_t)


def _sample_attn_kernel(q_ref, kn_ref, ckvn_ref, cckv_ref, ckpe_ref, w_uk_ref, g_kn_ref, o_ref,
                        m_ref, l_ref, acc_ref, *, tile, past, n_new):
    m_ref[...] = jnp.full(m_ref.shape, NEG_INF, F32)
    l_ref[...] = jnp.zeros(l_ref.shape, F32)
    acc_ref[...] = jnp.zeros(acc_ref.shape, F32)
    g_kn = g_kn_ref[...]
    qr_all = jnp.concatenate([q_ref[hd][:, QK_NOPE_DIM:] for hd in range(N_HEADS)], axis=0)
    n_kn = N_HEADS * QK_NOPE_DIM
    q_lat = [_dot((q_ref[hd][:, :QK_NOPE_DIM] * g_kn).astype(BF16),
                  w_uk_ref[hd * QK_NOPE_DIM:(hd + 1) * QK_NOPE_DIM, :]).astype(BF16) for hd in range(N_HEADS)]
    w_and_q = jnp.concatenate([w_uk_ref[...]] + q_lat, axis=0)

    def update(s, v):
        m_prev = m_ref[...]
        m_new = jnp.maximum(m_prev, jnp.max(s, axis=-1, keepdims=True))
        alpha = jnp.exp2(m_prev - m_new)
        pr = jnp.exp2(s - m_new)
        l_ref[...] = alpha * l_ref[...] + jnp.sum(pr, axis=-1, keepdims=True)
        acc_ref[...] = alpha * acc_ref[...] + _dot(pr.astype(BF16), v)
        m_ref[...] = m_new

    def up_project(kt):
        v = cckv_ref[kt * tile:(kt + 1) * tile, :].astype(BF16)
        return v, _dot_nt(w_and_q, v)

    def attend(kt, v, kn_s):
        kn_t = kn_s[:n_kn, :]
        s_raw = kn_s[n_kn:, :]
        kpe_t = ckpe_ref[:, kt * tile:(kt + 1) * tile].astype(BF16)
        kpe_t = jnp.concatenate([kpe_t, jnp.zeros_like(kpe_t)], axis=0)
        s_rope = _dot(qr_all, kpe_t)
        inv = []
        for hd in range(N_HEADS):
            blk = kn_t[hd * QK_NOPE_DIM:(hd + 1) * QK_NOPE_DIM, :]
            ms = jnp.sum(blk * blk, axis=0, keepdims=True) / QK_NOPE_DIM
            inv.append(jnp.broadcast_to(lax.rsqrt(ms + EPS), (n_new, tile)))
        s = s_raw * jnp.concatenate(inv, axis=0) + s_rope
        update(s, v)

    n_tiles = past // tile
    nxt = up_project(0)
    for kt in range(n_tiles):
        cur = nxt
        if kt + 1 < n_tiles:
            nxt = up_project(kt + 1)
        attend(kt, *cur)

    s_new = jnp.concatenate([_dot_nt(q_ref[hd], kn_ref[hd]) for hd in range(N_HEADS)], axis=0)
    q_pos = past + lax.broadcasted_iota(jnp.int32, s_new.shape, 0) % n_new
    k_pos = past + lax.broadcasted_iota(jnp.int32, s_new.shape, 1)
    s_new = jnp.where(k_pos // CHUNK <= q_pos // CHUNK, s_new, NEG_INF)
    update(s_new, ckvn_ref[...])

    o = (acc_ref[...] / l_ref[...]).astype(o_ref.dtype)
    for hd in range(N_HEADS):
        o_ref[:, hd * KV_LORA_RANK:(hd + 1) * KV_LORA_RANK] = o[hd * n_new:(hd + 1) * n_new, :]


def _sample_attention(q_cat, k_cat, ckv_b, cache_ckv, cache_kpe, a, w_uk, g_kn, n_new, tile):
    _, dec_batch, past, _ = cache_ckv.shape
    n = dec_batch * n_new
    rows = N_HEADS * n_new
    hrow = pl.BlockSpec((N_HEADS, n_new, HEAD_PAD), lambda b: (0, b, 0))
    cache_kpe_t = jnp.swapaxes(cache_kpe, 2, 3)
    return pl.pallas_call(
        functools.partial(_sample_attn_kernel, tile=tile, past=past, n_new=n_new),
        grid=(dec_batch,),
        in_specs=[
            hrow, hrow,
            pl.BlockSpec((n_new, KV_LORA_RANK), lambda b: (b, 0)),
            pl.BlockSpec((None, None, past, KV_LORA_RANK), lambda b: (a, b, 0, 0)),
            pl.BlockSpec((None, None, QK_ROPE_DIM, past), lambda b: (a, b, 0, 0)),
            _const_spec(w_uk.shape), _const_spec(g_kn.shape),
        ],
        out_specs=pl.BlockSpec((n_new, N_HEADS * KV_LORA_RANK), lambda b: (b, 0)),
        out_shape=jax.ShapeDtypeStruct((n, N_HEADS * KV_LORA_RANK), BF16),
        scratch_shapes=[
            pltpu.VMEM((rows, 1), F32),
            pltpu.VMEM((rows, 1), F32),
            pltpu.VMEM((rows, KV_LORA_RANK), F32),
        ],
        compiler_params=_cparams(1),
        name="sample_attention",
    )(q_cat, k_cat, ckv_b, cache_ckv, cache_kpe_t, w_uk, g_kn)


def _ffn_tail(x1, g2_ref, sh2_ref, sc2_ref, gt2_ref, w1_ref, w2_ref, ff_chunk):
    tm = x1.shape[0]
    h2 = _mod_norm(x1, g2_ref[...], _mod_rows(sh2_ref, tm), _mod_rows(sc2_ref, tm)).astype(BF16)
    return x1 + _mod_rows(gt2_ref, tm) * _sq_relu_ffn(h2, w1_ref, w2_ref, ff_chunk)


def _mla_tail_kernel(x_ref, ol_ref, w_uv_ref, w_o_ref, gt1_ref, g2_ref, sh2_ref, sc2_ref, gt2_ref,
                     w1_ref, w2_ref, o_ref, *, ff_chunk, latent_major):
    heads = []
    for hd in range(N_HEADS):
        lat = slice(hd * KV_LORA_RANK, (hd + 1) * KV_LORA_RANK)
        if latent_major:
            up = lax.dot_general(ol_ref[lat, :], w_uv_ref[hd], (((0,), (0,)), ((), ())),
                                 preferred_element_type=F32)
        else:
            up = _dot(ol_ref[:, lat], w_uv_ref[hd])
        heads.append(up.astype(BF16))
    m = _dot(jnp.concatenate(heads, axis=-1), w_o_ref[...])
    x1 = x_ref[...] + _mod_rows(gt1_ref, m.shape[0]) * m
    o_ref[...] = _ffn_tail(x1, g2_ref, sh2_ref, sc2_ref, gt2_ref, w1_ref, w2_ref, ff_chunk)


def _mla_tail(x, o_lat, mods, mod_spec, p, tm, ff_chunk, latent_major):
    n, d = x.shape
    row = lambda w: pl.BlockSpec((tm, w), lambda i: (i, 0))
    consts_a = [p["w_uv"], p["w_o"]]
    consts_b = [p["w1"], p["w2"]]
    if latent_major:
        ol_spec = pl.BlockSpec((o_lat.shape[0], tm), lambda i: (0, i))
    else:
        ol_spec = row(o_lat.shape[1])
    return pl.pallas_call(
        functools.partial(_mla_tail_kernel, ff_chunk=ff_chunk, latent_major=latent_major),
        grid=(n // tm,),
        in_specs=[row(d), ol_spec] + [_const_spec(c.shape) for c in consts_a]
        + [mod_spec(2), _const_spec(p["g2"].shape), mod_spec(3), mod_spec(4), mod_spec(5)]
        + [_layer_spec(c, p["layer"]) for c in consts_b],
        out_specs=row(d),
        out_shape=jax.ShapeDtypeStruct((n, d), F32),
        compiler_params=_cparams(1),
        name="mla_tail_ffn",
    )(x, o_lat, *consts_a, mods, p["g2"], mods, mods, mods, *consts_b)


def _sgu_layer_kernel(x_ref, g1_ref, sh1_ref, sc1_ref, gt1_ref, w_in_ref, g_v_ref, w_s_ref, b_s_ref,
                      w_o_ref, g2_ref, sh2_ref, sc2_ref, gt2_ref, w1_ref, w2_ref, *out_refs,
                      ff_chunk, period, offset, emit_v):
    x = x_ref[...]
    tm = x.shape[0]
    width = w_o_ref.shape[0]
    gdim = width // SGU_GROUPS
    h = _mod_norm(x, g1_ref[...], _mod_rows(sh1_ref, tm), _mod_rows(sc1_ref, tm)).astype(BF16)
    v = _rms(jax.nn.gelu(_dot(h, w_in_ref[:, width:])), width) * g_v_ref[...]
    u = jax.nn.gelu(_dot(h, w_in_ref[:, :width]))
    if emit_v:
        out_refs[1][...] = v
    vb = v.astype(BF16)

    ri = lax.broadcasted_iota(jnp.int32, (SGU_CHUNK, SGU_CHUNK), 0)
    ci = lax.broadcasted_iota(jnp.int32, (SGU_CHUNK, SGU_CHUNK), 1)
    vis = (ri // period == ci // period) & ((ci % period + offset) // CHUNK <= (ri % period + offset) // CHUNK)
    w_mix = [jnp.where(vis, w_s_ref[g], 0.0).astype(BF16) for g in range(SGU_GROUPS)]

    rows = []
    for c in range(tm // SGU_CHUNK):
        r0 = c * SGU_CHUNK
        cols = []
        for g in range(SGU_GROUPS):
            l0 = g * gdim
            mixed = _dot(w_mix[g], vb[r0:r0 + SGU_CHUNK, l0:l0 + gdim]) + b_s_ref[g]
            cols.append((u[r0:r0 + SGU_CHUNK, l0:l0 + gdim] * mixed).astype(BF16))
        rows.append(jnp.concatenate(cols, axis=-1))
    gated = jnp.concatenate(rows, axis=0)
    x1 = x + _mod_rows(gt1_ref, tm) * _dot(gated, w_o_ref[...])
    out_refs[0][...] = _ffn_tail(x1, g2_ref, sh2_ref, sc2_ref, gt2_ref, w1_ref, w2_ref, ff_chunk)


def _sgu_layer(x, mods, mod_spec, p, tm, ff_chunk, period, offset, emit_v):
    n, d = x.shape
    width = p["w_o"].shape[0]
    row = lambda w: pl.BlockSpec((tm, w), lambda i: (i, 0))
    cs = lambda a: _const_spec(a.shape)
    out_specs = [row(d)]
    out_shape = [jax.ShapeDtypeStruct((n, d), F32)]
    if emit_v:
        out_specs.append(row(width))
        out_shape.append(jax.ShapeDtypeStruct((n, width), F32))
    return pl.pallas_call(
        functools.partial(_sgu_layer_kernel, ff_chunk=ff_chunk, period=period, offset=offset, emit_v=emit_v),
        grid=(n // tm,),
        in_specs=[row(d), cs(p["g1"]), mod_spec(0), mod_spec(1), mod_spec(2), cs(p["w_in"]), cs(p["g_v"]),
                  cs(p["w_s"]), cs(p["b_s"]), cs(p["w_o"]), cs(p["g2"]), mod_spec(3), mod_spec(4), mod_spec(5),
                  _layer_spec(p["w1"], p["layer"]), _layer_spec(p["w2"], p["layer"])],
        out_specs=out_specs,
        out_shape=out_shape,
        compiler_params=_cparams(1),
        name="sgu_layer_ffn",
    )(x, p["g1"], mods, mods, mods, p["w_in"], p["g_v"], p["w_s"], p["b_s"], p["w_o"],
      p["g2"], mods, mods, mods, p["w1"], p["w2"])


def _rope_tables(pos):
    half = QK_ROPE_DIM // 2
    inv = 1.0 / (ROPE_THETA ** (jnp.arange(half, dtype=F32) / half))
    n = pos.shape[0]
    step = CHUNK
    if n % step == 0 and np.array_equal(pos, np.arange(n)):
        ang_a = (step * jnp.arange(n // step, dtype=F32))[:, None] * inv[None, :]
        ang_b = jnp.arange(step, dtype=F32)[:, None] * inv[None, :]
        ca, sa = jnp.cos(ang_a)[:, None, :], jnp.sin(ang_a)[:, None, :]
        cb, sb = jnp.cos(ang_b)[None, :, :], jnp.sin(ang_b)[None, :, :]
        cos = (ca * cb - sa * sb).reshape(n, half)
        sin = (sa * cb + ca * sb).reshape(n, half)
    else:
        ang = jnp.asarray(pos).astype(F32)[:, None] * inv[None, :]
        cos, sin = jnp.cos(ang), jnp.sin(ang)
    z = jnp.zeros((pos.shape[0], LANES - QK_ROPE_DIM), F32)
    return jnp.concatenate([cos, cos, z], axis=-1), jnp.concatenate([-sin, sin, z], axis=-1)


def _pad_lanes(g, n):
    return jnp.pad(g, (0, n - g.shape[0])).reshape(1, n)


def _prep_mla(a, norm1_g, norm2_g, ffn_w1, ffn_w2, layer, mla_w_in, mla_g_qa, mla_g_kva, mla_w_q_up,
              mla_w_uk, mla_w_uv, mla_g_qn, mla_g_qr, mla_g_kn, mla_g_kr, mla_w_o):
    d = mla_w_in.shape[1]
    w_in = mla_w_in[a]
    w_in = jnp.concatenate([w_in[:, Q_LORA_RANK:Q_LORA_RANK + KV_LORA_RANK],
                            jnp.pad(w_in[:, Q_LORA_RANK + KV_LORA_RANK:], ((0, 0), (0, LANES - QK_ROPE_DIM))),
                            w_in[:, :Q_LORA_RANK]], axis=1).astype(BF16)
    half = QK_ROPE_DIM // 2
    w_q_up = mla_w_q_up[a].reshape(Q_LORA_RANK, N_HEADS, QK_NOPE_DIM + QK_ROPE_DIM)
    w_r1 = w_q_up[:, :, QK_NOPE_DIM:QK_NOPE_DIM + half]
    w_r2 = w_q_up[:, :, QK_NOPE_DIM + half:]
    w_z = jnp.zeros((Q_LORA_RANK, N_HEADS, LANES - QK_ROPE_DIM), w_q_up.dtype)
    w_q_up = jnp.concatenate([w_q_up[:, :, :QK_NOPE_DIM], w_r1, w_r2, w_z, w_r2, w_r1, w_z], axis=-1)
    g_qr = mla_g_qr[a]
    g_qr = jnp.concatenate([_pad_lanes(g_qr, LANES),
                            _pad_lanes(jnp.concatenate([g_qr[half:], g_qr[:half]]), LANES)], axis=0)
    return {
        "g1": norm1_g[layer].reshape(1, d), "g2": norm2_g[layer].reshape(1, d),
        "w1": ffn_w1, "w2": ffn_w2, "layer": layer,
        "w_in": w_in,
        "g_qa": mla_g_qa[a].reshape(1, -1), "g_kva": mla_g_kva[a].reshape(1, -1),
        "g_kr": _pad_lanes(mla_g_kr[a], LANES),
        "w_q_up": w_q_up.reshape(Q_LORA_RANK, N_HEADS * Q_HEAD_COLS).astype(BF16),
        "g_qn": mla_g_qn[a].reshape(1, -1), "g_qr": g_qr,
        "w_uk": mla_w_uk[a].reshape(KV_LORA_RANK, N_HEADS * QK_NOPE_DIM).astype(BF16),
        "g_kn": mla_g_kn[a].reshape(1, -1),
        "w_uv": jnp.swapaxes(mla_w_uv[a], 0, 1).astype(BF16),
        "w_o": mla_w_o[a].astype(BF16),
    }


def _prep_sgu(b, norm1_g, norm2_g, ffn_w1, ffn_w2, layer, cm_w_in, cm_g_v, cm_w_s, cm_b_s, cm_w_o, idx):
    d = cm_w_in.shape[1]
    period = len(idx)
    lo = int(idx[0])
    assert SGU_CHUNK % period == 0 and np.array_equal(idx, lo + np.arange(period)), idx
    reps = SGU_CHUNK // period
    w_s = jnp.tile(cm_w_s[b][:, lo:lo + period, lo:lo + period], (1, reps, reps))
    b_s = jnp.tile(cm_b_s[b][:, lo:lo + period], (1, reps))
    gdim = cm_w_o.shape[1] // SGU_GROUPS
    return {
        "g1": norm1_g[layer].reshape(1, d), "g2": norm2_g[layer].reshape(1, d),
        "w1": ffn_w1, "w2": ffn_w2, "layer": layer,
        "w_in": cm_w_in[b].astype(BF16), "g_v": cm_g_v[b].reshape(1, -1),
        "w_s": w_s, "b_s": jnp.broadcast_to(b_s[:, :, None], (SGU_GROUPS, SGU_CHUNK, gdim)),
        "w_o": cm_w_o[b].astype(BF16),
    }, period, lo


def kernel(x_prompt, x_sample, cache_ckv, cache_kpe, c_prompt, c_sample, ada_w, ada_b, norm1_g, norm2_g,
           ffn_w1, ffn_w2, mla_w_in, mla_g_qa, mla_g_kva, mla_w_q_up, mla_w_uk, mla_w_uv, mla_g_qn,
           mla_g_qr, mla_g_kn, mla_g_kr, mla_w_o, cm_w_in, cm_g_v, cm_w_s, cm_b_s, cm_w_o):
    batch, seq, d = x_prompt.shape
    dec_batch, dec_seq, _ = x_sample.shape
    past = cache_ckv.shape[2]
    depth = ada_w.shape[0]
    n_p, n_s = batch * seq, dec_batch * dec_seq
    tm_p, tm_proj, tm_s = TOKEN_TILE, PROJ_TILE, n_s
    attn_tile, attn_heads, ff_chunk = ATTN_TILE, ATTN_HEADS, FF_CHUNK
    assert seq % tm_proj == 0 and seq % attn_tile == 0 and tm_p == attn_tile and n_s % attn_tile == 0
    assert past % CACHE_TILE == 0 and N_HEADS % attn_heads == 0

    pos_p = np.arange(seq)
    pos_s = past + np.arange(dec_seq)
    rc_p, rs_p = _rope_tables(pos_p)
    rc_s, rs_s = _rope_tables(np.tile(pos_s, dec_batch))
    rope_spec_p = pl.BlockSpec((tm_proj, LANES), lambda i: (i % (seq // tm_proj), 0))
    rope_spec_s = pl.BlockSpec((tm_s, LANES), lambda i: (i, 0))

    n_c = batch + dec_batch
    c_all = jnp.concatenate([c_prompt, c_sample], axis=0)
    c_all = jnp.pad(c_all, ((0, -n_c % BF16_ROWS), (0, 0)))
    mod = _ada_modulation(c_all, ada_w, ada_b)
    m_p = mod[:, :batch].reshape(depth, batch, 1, 6 * d)
    m_s = mod[:, batch:n_c]

    def prompt_mods(layer, tm):
        return lambda j: pl.BlockSpec((None, None, 1, d), lambda i: (layer, i // (seq // tm), 0, j))

    def sample_mods(layer):
        return lambda j: pl.BlockSpec((None, tm_s // dec_seq, d), lambda i: (layer, i, j))

    x_p = x_prompt.reshape(n_p, d)
    x_s = x_sample.reshape(n_s, d)
    ckv_p_l, kpe_p_l, ckv_s_l, kpe_s_l, v_s_l = [], [], [], [], []
    for layer in range(depth):
        mod_spec_p, mod_spec_proj, mod_spec_s = prompt_mods(layer, tm_p), prompt_mods(layer, tm_proj), sample_mods(layer)
        if layer % 2 == 0:
            a = layer // 2
            p = _prep_mla(a, norm1_g, norm2_g, ffn_w1, ffn_w2, layer, mla_w_in, mla_g_qa, mla_g_kva,
                          mla_w_q_up, mla_w_uk, mla_w_uv, mla_g_qn, mla_g_qr, mla_g_kn, mla_g_kr, mla_w_o)
            side = (ffn_w1, ffn_w2, mla_w_o, cm_w_in, cm_w_o) if ffn_w1.dtype != BF16 else ()
            q_p, k_p, ckv_p, _, ckvt_p, kr_p, *cast = _mla_project(x_p, m_p, mod_spec_proj, p, rc_p, rs_p,
                                                                   rope_spec_p, tm_proj, attn_tile, side)
            if side:
                ffn_w1, ffn_w2, mla_w_o, cm_w_in, cm_w_o = cast
                p["w1"], p["w2"], p["w_o"] = ffn_w1, ffn_w2, mla_w_o[a]
            q_s, k_s, ckv_s, ckvb_s, _, kr_s = _mla_project(x_s, m_s, mod_spec_s, p, rc_s, rs_s, rope_spec_s,
                                                            tm_s, attn_tile)
            ol_p = _prompt_attention(q_p, k_p, ckvt_p, batch, seq, attn_tile, attn_heads)
            ol_s = _sample_attention(q_s, k_s, ckvb_s, cache_ckv, cache_kpe, a, p["w_uk"].T, p["g_kn"],
                                     dec_seq, CACHE_TILE)
            x_p = _mla_tail(x_p, ol_p, m_p, mod_spec_p, p, tm_p, ff_chunk, True)
            x_s = _mla_tail(x_s, ol_s, m_s, mod_spec_s, p, tm_s, ff_chunk, False)
            ckv_p_l.append(ckv_p.reshape(batch, seq, -1))
            kpe_p_l.append(kr_p.reshape(batch, seq, -1))
            ckv_s_l.append(ckv_s.reshape(dec_batch, dec_seq, -1))
            kpe_s_l.append(kr_s.reshape(dec_batch, dec_seq, -1))
        else:
            b = layer // 2
            args = (b, norm1_g, norm2_g, ffn_w1, ffn_w2, layer, cm_w_in, cm_g_v, cm_w_s, cm_b_s, cm_w_o)
            pp, period_p, off_p = _prep_sgu(*args, np.arange(SGU_CHUNK))
            ps, period_s, off_s = _prep_sgu(*args, pos_s % SGU_CHUNK)
            (x_p,) = _sgu_layer(x_p, m_p, mod_spec_p, pp, tm_p, ff_chunk, period_p, off_p, False)
            x_s, v_s = _sgu_layer(x_s, m_s, mod_spec_s, ps, tm_s, ff_chunk, period_s, off_s, True)
            v_s_l.append(v_s.reshape(dec_batch, dec_seq, -1))

    return (x_p.reshape(batch, seq, d), x_s.reshape(dec_batch, dec_seq, d),
            jnp.stack(ckv_p_l), jnp.stack(kpe_p_l), jnp.stack(ckv_s_l), jnp.stack(kpe_s_l),
            jnp.stack(v_s_l))
```

```python
import functools

import numpy as np
import jax
import jax.numpy as jnp
from jax import lax
from jax.experimental import pallas as pl
from jax.experimental.pallas import tpu as pltpu

F32 = jnp.float32
BF16 = jnp.bfloat16

CHUNK = 64
N_HEADS = 8
QK_NOPE_DIM = 128
QK_ROPE_DIM = 64
Q_LORA_RANK = 384
KV_LORA_RANK = 256
ROPE_THETA = 10000.0
SGU_CHUNK = 128
SGU_GROUPS = 8
EPS = 1e-6

LANES = 128
BF16_ROWS = 16
MXU_DIM = 256
HEAD_PAD = 2 * LANES
Q_HEAD_COLS = 3 * LANES
VMEM_LIMIT = 56 * 1024 * 1024

TOKEN_TILE = 512
PROJ_TILE = 1024
ATTN_TILE = 512
CACHE_TILE = 2048
ATTN_HEADS = 4
FF_CHUNK = 1024
ADA_COLS = 1536

NEG_INF = float(np.finfo(np.float32).min)
Q_SCALE = float((QK_NOPE_DIM + QK_ROPE_DIM) ** -0.5 * np.log2(np.e))


def _cparams(n_axes):
    return pltpu.CompilerParams(
        dimension_semantics=("arbitrary",) * n_axes,
        vmem_limit_bytes=VMEM_LIMIT,
    )


def _const_spec(shape):
    nd = len(shape)
    return pl.BlockSpec(shape, lambda *_: (0,) * nd, pipeline_mode=pl.Buffered(1))


def _layer_spec(stacked, layer):
    return pl.BlockSpec((None,) + stacked.shape[1:], lambda *_: (layer, 0, 0), pipeline_mode=pl.Buffered(1))


def _dot(a, b):
    return jnp.dot(a, b, preferred_element_type=F32)


def _dot_nt(a, b):
    return lax.dot_general(a, b, (((1,), (1,)), ((), ())), preferred_element_type=F32)


def _rms(x, n):
    ms = jnp.sum(x * x, axis=-1, keepdims=True) / n
    return x * lax.rsqrt(ms + EPS)


def _rope128(x, c, s):
    half = QK_ROPE_DIM // 2
    rot = pltpu.roll(x, LANES - half, axis=1) + pltpu.roll(x, half, axis=1)
    return x * c + rot * s


def _mod_rows(ref, rows):
    v = ref[...]
    n = v.shape[0]
    if n in (1, rows):
        return v
    return jnp.broadcast_to(v[:, None, :], (n, rows // n, v.shape[1])).reshape(rows, v.shape[1])


def _mod_norm(x, g, shift, scale):
    return _rms(x, x.shape[-1]) * g * (1.0 + scale) + shift


def _sq_relu_ffn(hb, w1_ref, w2_ref, ff_chunk):
    d_ff = w1_ref.shape[1]
    acc = None
    for c in range(d_ff // ff_chunk):
        a = _dot(hb, w1_ref[:, c * ff_chunk:(c + 1) * ff_chunk])
        a = jnp.maximum(a, 0.0)
        a = (a * a).astype(BF16)
        part = _dot(a, w2_ref[c * ff_chunk:(c + 1) * ff_chunk, :])
        acc = part if acc is None else acc + part
    return acc


def _ada_kernel(c_ref, w_ref, b_ref, o_ref):
    s = jax.nn.silu(c_ref[...]).astype(BF16)
    o_ref[...] = _dot(s, w_ref[...].astype(BF16)) + b_ref[...]


def _ada_modulation(c_all, ada_w, ada_b):
    depth, d, n6 = ada_w.shape
    rows = c_all.shape[0]
    tn = ADA_COLS
    return pl.pallas_call(
        _ada_kernel,
        grid=(depth, n6 // tn),
        in_specs=[
            pl.BlockSpec((rows, d), lambda l, j: (0, 0)),
            pl.BlockSpec((None, d, tn), lambda l, j: (l, 0, j)),
            pl.BlockSpec((None, 1, tn), lambda l, j: (l, 0, j)),
        ],
        out_specs=pl.BlockSpec((None, rows, tn), lambda l, j: (l, 0, j)),
        out_shape=jax.ShapeDtypeStruct((depth, rows, n6), F32),
        compiler_params=_cparams(2),
        name="ada_modulation",
    )(c_all, ada_w, ada_b.reshape(depth, 1, n6))


def _mla_proj_kernel(x_ref, g1_ref, sh_ref, sc_ref, w_in_ref, g_qa_ref, g_kva_ref, g_kr_ref,
                     w_qup_ref, g_qn_ref, g_qr_ref, w_uk_ref, g_kn_ref, rc_ref, rs_ref, *refs):
    n_cast = (len(refs) - 6) // 2
    cast_in, (q_ref, k_ref, ckv_ref, ckvb_ref, ckvt_ref, kr_ref), cast_out = (
        refs[:n_cast], refs[n_cast:n_cast + 6], refs[n_cast + 6:])
    for src, dst in zip(cast_in, cast_out):
        dst[...] = src[...].astype(BF16)
    x = x_ref[...]
    tm = x.shape[0]
    h = _mod_norm(x, g1_ref[...], _mod_rows(sh_ref, tm), _mod_rows(sc_ref, tm)).astype(BF16)
    c0 = KV_LORA_RANK
    ckv = _rms(_dot(h, w_in_ref[:, :c0]), KV_LORA_RANK) * g_kva_ref[...]
    a = _dot(h, w_in_ref[:, c0:])
    cq = _rms(a[:, LANES:], Q_LORA_RANK) * g_qa_ref[...]
    rc = rc_ref[...]
    rs = rs_ref[...]
    kr = _rope128(_rms(a[:, :LANES], QK_ROPE_DIM) * g_kr_ref[...], rc, rs)
    ckv_ref[...] = ckv
    ckv_b = ckv.astype(BF16)
    ckvb_ref[...] = ckv_b
    vt_tile = ckvt_ref.shape[-1]
    for j in range(ckvt_ref.shape[0]):
        ckvt_ref[j] = ckv[j * vt_tile:(j + 1) * vt_tile, :].T.astype(BF16)
    kr_ref[...] = kr[:, :QK_ROPE_DIM]
    kr_b = kr.astype(BF16)

    kn_all = _dot(ckv_b, w_uk_ref[...])
    q = _dot(cq.astype(BF16), w_qup_ref[...])
    g_qn = g_qn_ref[...] * Q_SCALE
    rot_c = rc * (g_qr_ref[0:1, :] * Q_SCALE)
    rot_s = rs * (g_qr_ref[1:2, :] * Q_SCALE)
    g_kn = g_kn_ref[...]
    for hd in range(N_HEADS):
        kn = _rms(kn_all[:, hd * QK_NOPE_DIM:(hd + 1) * QK_NOPE_DIM], QK_NOPE_DIM) * g_kn
        k_ref[hd] = jnp.concatenate([kn.astype(BF16), kr_b], axis=-1)
    for hd in range(N_HEADS):
        o = hd * Q_HEAD_COLS
        qn = _rms(q[:, o:o + QK_NOPE_DIM], QK_NOPE_DIM) * g_qn
        x = q[:, o + QK_NOPE_DIM:o + 2 * LANES]
        x_sw = q[:, o + 2 * LANES:o + 3 * LANES]
        inv = lax.rsqrt(jnp.sum(x * x, axis=-1, keepdims=True) / QK_ROPE_DIM + EPS)
        qr = (x * rot_c + x_sw * rot_s) * inv
        q_ref[hd] = jnp.concatenate([qn.astype(BF16), qr.astype(BF16)], axis=-1)


def _mla_project(x, mods, mod_spec, p, rope_c, rope_s, rope_spec, tm, vt_tile, side_casts=()):
    n, d = x.shape
    steps = n // tm
    row = lambda w: pl.BlockSpec((tm, w), lambda i: (i, 0))
    hrow = pl.BlockSpec((N_HEADS, tm, HEAD_PAD), lambda i: (0, i, 0))
    consts = [p["w_in"], p["g_qa"], p["g_kva"], p["g_kr"], p["w_q_up"], p["g_qn"], p["g_qr"],
              p["w_uk"], p["g_kn"]]
    cast_specs = []
    for w in side_casts:
        layers, rows, cols = w.shape
        slab = layers * rows // steps
        assert slab * steps == layers * rows and rows % slab == 0 and slab % BF16_ROWS == 0, (w.shape, steps)
        per_layer = rows // slab
        cast_specs.append(pl.BlockSpec((None, slab, cols), lambda i, per_layer=per_layer: (i // per_layer, i % per_layer, 0)))
    return pl.pallas_call(
        _mla_proj_kernel,
        grid=(steps,),
        in_specs=[row(d), _const_spec(p["g1"].shape), mod_spec(0), mod_spec(1)]
        + [_const_spec(c.shape) for c in consts] + [rope_spec, rope_spec] + cast_specs,
        out_specs=[hrow, hrow, row(KV_LORA_RANK), row(KV_LORA_RANK),
                   pl.BlockSpec((tm // vt_tile, KV_LORA_RANK, vt_tile), lambda i: (i, 0, 0)), row(QK_ROPE_DIM)]
        + cast_specs,
        out_shape=[
            jax.ShapeDtypeStruct((N_HEADS, n, HEAD_PAD), BF16),
            jax.ShapeDtypeStruct((N_HEADS, n, HEAD_PAD), BF16),
            jax.ShapeDtypeStruct((n, KV_LORA_RANK), F32),
            jax.ShapeDtypeStruct((n, KV_LORA_RANK), BF16),
            jax.ShapeDtypeStruct((n // vt_tile, KV_LORA_RANK, vt_tile), BF16),
            jax.ShapeDtypeStruct((n, QK_ROPE_DIM), F32),
        ] + [jax.ShapeDtypeStruct(w.shape, BF16) for w in side_casts],
        compiler_params=_cparams(1),
        name="mla_project",
    )(x, p["g1"], mods, mods, *consts, rope_c, rope_s, *side_casts)


def _prompt_attn_kernel(q_ref, k_ref, vt_ref, o_ref, m_ref, l_ref, acc_ref, st_ref, *, tile, heads):
    qi = pl.program_id(2)
    m_ref[...] = jnp.full(m_ref.shape, NEG_INF, F32)
    l_ref[...] = jnp.zeros(l_ref.shape, F32)
    acc_ref[...] = jnp.zeros(acc_ref.shape, F32)

    first = 2

    def scores(kt, slot, gs=range(heads), qt=qi):
        k0 = kt * tile if isinstance(kt, int) else pl.multiple_of(kt * tile, tile)
        q0 = pl.multiple_of(qt * tile, tile)
        for g in gs:
            st_ref[slot, g] = _dot_nt(k_ref[g, pl.ds(k0, tile), :], q_ref[g, pl.ds(q0, tile), :])

    def step(kt, slot, masked, gs=range(heads)):
        vt = vt_ref[kt]
        for g in gs:
            for w in range(tile // MXU_DIM):
                nk = (w + 1) * MXU_DIM if masked else tile
                alphas, probs = [], []
                for j in range(w * MXU_DIM // LANES, (w + 1) * MXU_DIM // LANES):
                    cols = slice(j * LANES, (j + 1) * LANES)
                    col = st_ref[slot, g, :nk, cols]
                    if masked:
                        kc = lax.broadcasted_iota(jnp.int32, col.shape, 0) // CHUNK
                        qc = (lax.broadcasted_iota(jnp.int32, col.shape, 1) + j * LANES) // CHUNK
                        col = jnp.where(kc <= qc, col, NEG_INF)
                    m_prev = m_ref[g, :, cols]
                    m_new = jnp.maximum(m_prev, jnp.max(col, axis=0, keepdims=True))
                    alpha = jnp.exp2(m_prev - m_new)
                    pr = jnp.exp2(col - m_new)
                    l_ref[g, :, cols] = alpha * l_ref[g, :, cols] + jnp.sum(pr, axis=0, keepdims=True)
                    m_ref[g, :, cols] = m_new
                    probs.append(pr.astype(BF16))
                    alphas.append(alpha)
                wide = slice(w * MXU_DIM, (w + 1) * MXU_DIM)
                acc_ref[g, :, wide] = (acc_ref[g, :, wide] * jnp.concatenate(alphas, axis=-1)
                                       + _dot(vt[:, :nk], jnp.concatenate(probs, axis=-1)))

    nxt = jnp.minimum(qi + 1, pl.num_programs(2) - 1)

    def ahead(gs=range(heads)):
        scores(0, first, gs, qt=nxt)

    @pl.when(qi == 0)
    def _():
        scores(0, 1)
        for g in range(heads):
            ahead([g])
            step(0, 1, True, [g])

    @pl.when(qi > 0)
    def _():
        for g in range(heads):
            scores(1, 0, [g])
            step(0, first, False, [g])

        def body(i, carry):
            kt = 1 + 2 * i
            for g in range(heads):
                scores(kt + 1, 1, [g])
                step(kt, 0, False, [g])
            for g in range(heads):
                scores(kt + 2, 0, [g])
                step(kt + 1, 1, False, [g])
            return carry

        lax.fori_loop(0, (qi - 1) // 2, body, 0)

        @pl.when((qi - 1) % 2 == 0)
        def _():
            for g in range(heads):
                ahead([g])
                step(qi, 0, True, [g])

        @pl.when((qi - 1) % 2 == 1)
        def _():
            for g in range(heads):
                scores(qi, 1, [g])
                step(qi - 1, 0, False, [g])
            for g in range(heads):
                ahead([g])
                step(qi, 1, True, [g])

    for g in range(heads):
        o_ref[g * KV_LORA_RANK:(g + 1) * KV_LORA_RANK, :] = (acc_ref[g] / l_ref[g]).astype(o_ref.dtype)


def _prompt_attention(q_cat, k_cat, ckv_t, batch, seq, tile, heads):
    n = batch * seq
    nq = seq // tile
    assert ckv_t.shape == (n // tile, KV_LORA_RANK, tile), ckv_t.shape
    return pl.pallas_call(
        functools.partial(_prompt_attn_kernel, tile=tile, heads=heads),
        grid=(batch, N_HEADS // heads, nq),
        in_specs=[
            pl.BlockSpec((heads, seq, HEAD_PAD), lambda b, h, i: (h, b, 0)),
            pl.BlockSpec((heads, seq, HEAD_PAD), lambda b, h, i: (h, b, 0)),
            pl.BlockSpec((nq, KV_LORA_RANK, tile), lambda b, h, i: (b, 0, 0)),
        ],
        out_specs=pl.BlockSpec((heads * KV_LORA_RANK, tile), lambda b, h, i: (h, b * nq + i)),
        out_shape=jax.ShapeDtypeStruct((N_HEADS * KV_LORA_RANK, n), BF16),
        scratch_shapes=[
            pltpu.VMEM((heads, 1, tile), F32),
            pltpu.VMEM((heads, 1, tile), F32),
            pltpu.VMEM((heads, KV_LORA_RANK, tile), F32),
            pltpu.VMEM((3, heads, tile, tile), F32),
        ],
        compiler_params=_cparams(3),
        name="prompt_attention",
    )(q_cat, k_cat, ckv_t)


def _sample_attn_kernel(q_ref, kn_ref, ckvn_ref, cckv_ref, ckpe_ref, w_uk_ref, g_kn_ref, o_ref,
                        m_ref, l_ref, acc_ref, *, tile, past, n_new):
    m_ref[...] = jnp.full(m_ref.shape, NEG_INF, F32)
    l_ref[...] = jnp.zeros(l_ref.shape, F32)
    acc_ref[...] = jnp.zeros(acc_ref.shape, F32)
    g_kn = g_kn_ref[...]
    qr_all = jnp.concatenate([q_ref[hd][:, QK_NOPE_DIM:] for hd in range(N_HEADS)], axis=0)
    n_kn = N_HEADS * QK_NOPE_DIM
    q_lat = [_dot((q_ref[hd][:, :QK_NOPE_DIM] * g_kn).astype(BF16),
                  w_uk_ref[hd * QK_NOPE_DIM:(hd + 1) * QK_NOPE_DIM, :]).astype(BF16) for hd in range(N_HEADS)]
    w_and_q = jnp.concatenate([w_uk_ref[...]] + q_lat, axis=0)

    def update(s, v):
        m_prev = m_ref[...]
        m_new = jnp.maximum(m_prev, jnp.max(s, axis=-1, keepdims=True))
        alpha = jnp.exp2(m_prev - m_new)
        pr = jnp.exp2(s - m_new)
        l_ref[...] = alpha * l_ref[...] + jnp.sum(pr, axis=-1, keepdims=True)
        acc_ref[...] = alpha * acc_ref[...] + _dot(pr.astype(BF16), v)
        m_ref[...] = m_new

    def up_project(kt):
        v = cckv_ref[kt * tile:(kt + 1) * tile, :].astype(BF16)
        return v, _dot_nt(w_and_q, v)

    def attend(kt, v, kn_s):
        kn_t = kn_s[:n_kn, :]
        s_raw = kn_s[n_kn:, :]
        kpe_t = ckpe_ref[:, kt * tile:(kt + 1) * tile].astype(BF16)
        kpe_t = jnp.concatenate([kpe_t, jnp.zeros_like(kpe_t)], axis=0)
        s_rope = _dot(qr_all, kpe_t)
        inv = []
        for hd in range(N_HEADS):
            blk = kn_t[hd * QK_NOPE_DIM:(hd + 1) * QK_NOPE_DIM, :]
            ms = jnp.sum(blk * blk, axis=0, keepdims=True) / QK_NOPE_DIM
            inv.append(jnp.broadcast_to(lax.rsqrt(ms + EPS), (n_new, tile)))
        s = s_raw * jnp.concatenate(inv, axis=0) + s_rope
        update(s, v)

    n_tiles = past // tile
    nxt = up_project(0)
    for kt in range(n_tiles):
        cur = nxt
        if kt + 1 < n_tiles:
            nxt = up_project(kt + 1)
        attend(kt, *cur)

    s_new = jnp.concatenate([_dot_nt(q_ref[hd], kn_ref[hd]) for hd in range(N_HEADS)], axis=0)
    q_pos = past + lax.broadcasted_iota(jnp.int32, s_new.shape, 0) % n_new
    k_pos = past + lax.broadcasted_iota(jnp.int32, s_new.shape, 1)
    s_new = jnp.where(k_pos // CHUNK <= q_pos // CHUNK, s_new, NEG_INF)
    update(s_new, ckvn_ref[...])

    o = (acc_ref[...] / l_ref[...]).astype(o_ref.dtype)
    for hd in range(N_HEADS):
        o_ref[:, hd * KV_LORA_RANK:(hd + 1) * KV_LORA_RANK] = o[hd * n_new:(hd + 1) * n_new, :]


def _sample_attention(q_cat, k_cat, ckv_b, cache_ckv, cache_kpe, a, w_uk, g_kn, n_new, tile):
    _, dec_batch, past, _ = cache_ckv.shape
    n = dec_batch * n_new
    rows = N_HEADS * n_new
    hrow = pl.BlockSpec((N_HEADS, n_new, HEAD_PAD), lambda b: (0, b, 0))
    cache_kpe_t = jnp.swapaxes(cache_kpe, 2, 3)
    return pl.pallas_call(
        functools.partial(_sample_attn_kernel, tile=tile, past=past, n_new=n_new),
        grid=(dec_batch,),
        in_specs=[
            hrow, hrow,
            pl.BlockSpec((n_new, KV_LORA_RANK), lambda b: (b, 0)),
            pl.BlockSpec((None, None, past, KV_LORA_RANK), lambda b: (a, b, 0, 0)),
            pl.BlockSpec((None, None, QK_ROPE_DIM, past), lambda b: (a, b, 0, 0)),
            _const_spec(w_uk.shape), _const_spec(g_kn.shape),
        ],
        out_specs=pl.BlockSpec((n_new, N_HEADS * KV_LORA_RANK), lambda b: (b, 0)),
        out_shape=jax.ShapeDtypeStruct((n, N_HEADS * KV_LORA_RANK), BF16),
        scratch_shapes=[
            pltpu.VMEM((rows, 1), F32),
            pltpu.VMEM((rows, 1), F32),
            pltpu.VMEM((rows, KV_LORA_RANK), F32),
        ],
        compiler_params=_cparams(1),
        name="sample_attention",
    )(q_cat, k_cat, ckv_b, cache_ckv, cache_kpe_t, w_uk, g_kn)


def _ffn_tail(x1, g2_ref, sh2_ref, sc2_ref, gt2_ref, w1_ref, w2_ref, ff_chunk):
    tm = x1.shape[0]
    h2 = _mod_norm(x1, g2_ref[...], _mod_rows(sh2_ref, tm), _mod_rows(sc2_ref, tm)).astype(BF16)
    return x1 + _mod_rows(gt2_ref, tm) * _sq_relu_ffn(h2, w1_ref, w2_ref, ff_chunk)


def _mla_tail_kernel(x_ref, ol_ref, w_uv_ref, w_o_ref, gt1_ref, g2_ref, sh2_ref, sc2_ref, gt2_ref,
                     w1_ref, w2_ref, o_ref, *, ff_chunk, latent_major):
    heads = []
    for hd in range(N_HEADS):
        lat = slice(hd * KV_LORA_RANK, (hd + 1) * KV_LORA_RANK)
        if latent_major:
            up = lax.dot_general(ol_ref[lat, :], w_uv_ref[hd], (((0,), (0,)), ((), ())),
                                 preferred_element_type=F32)
        else:
            up = _dot(ol_ref[:, lat], w_uv_ref[hd])
        heads.append(up.astype(BF16))
    m = _dot(jnp.concatenate(heads, axis=-1), w_o_ref[...])
    x1 = x_ref[...] + _mod_rows(gt1_ref, m.shape[0]) * m
    o_ref[...] = _ffn_tail(x1, g2_ref, sh2_ref, sc2_ref, gt2_ref, w1_ref, w2_ref, ff_chunk)


def _mla_tail(x, o_lat, mods, mod_spec, p, tm, ff_chunk, latent_major):
    n, d = x.shape
    row = lambda w: pl.BlockSpec((tm, w), lambda i: (i, 0))
    consts_a = [p["w_uv"], p["w_o"]]
    consts_b = [p["w1"], p["w2"]]
    if latent_major:
        ol_spec = pl.BlockSpec((o_lat.shape[0], tm), lambda i: (0, i))
    else:
        ol_spec = row(o_lat.shape[1])
    return pl.pallas_call(
        functools.partial(_mla_tail_kernel, ff_chunk=ff_chunk, latent_major=latent_major),
        grid=(n // tm,),
        in_specs=[row(d), ol_spec] + [_const_spec(c.shape) for c in consts_a]
        + [mod_spec(2), _const_spec(p["g2"].shape), mod_spec(3), mod_spec(4), mod_spec(5)]
        + [_layer_spec(c, p["layer"]) for c in consts_b],
        out_specs=row(d),
        out_shape=jax.ShapeDtypeStruct((n, d), F32),
        compiler_params=_cparams(1),
        name="mla_tail_ffn",
    )(x, o_lat, *consts_a, mods, p["g2"], mods, mods, mods, *consts_b)


def _sgu_layer_kernel(x_ref, g1_ref, sh1_ref, sc1_ref, gt1_ref, w_in_ref, g_v_ref, w_s_ref, b_s_ref,
                      w_o_ref, g2_ref, sh2_ref, sc2_ref, gt2_ref, w1_ref, w2_ref, *out_refs,
                      ff_chunk, period, offset, emit_v):
    x = x_ref[...]
    tm = x.shape[0]
    width = w_o_ref.shape[0]
    gdim = width // SGU_GROUPS
    h = _mod_norm(x, g1_ref[...], _mod_rows(sh1_ref, tm), _mod_rows(sc1_ref, tm)).astype(BF16)
    v = _rms(jax.nn.gelu(_dot(h, w_in_ref[:, width:])), width) * g_v_ref[...]
    u = jax.nn.gelu(_dot(h, w_in_ref[:, :width]))
    if emit_v:
        out_refs[1][...] = v
    vb = v.astype(BF16)

    ri = lax.broadcasted_iota(jnp.int32, (SGU_CHUNK, SGU_CHUNK), 0)
    ci = lax.broadcasted_iota(jnp.int32, (SGU_CHUNK, SGU_CHUNK), 1)
    vis = (ri // period == ci // period) & ((ci % period + offset) // CHUNK <= (ri % period + offset) // CHUNK)
    w_mix = [jnp.where(vis, w_s_ref[g], 0.0).astype(BF16) for g in range(SGU_GROUPS)]

    rows = []
    for c in range(tm // SGU_CHUNK):
        r0 = c * SGU_CHUNK
        cols = []
        for g in range(SGU_GROUPS):
            l0 = g * gdim
            mixed = _dot(w_mix[g], vb[r0:r0 + SGU_CHUNK, l0:l0 + gdim]) + b_s_ref[g]
            cols.append((u[r0:r0 + SGU_CHUNK, l0:l0 + gdim] * mixed).astype(BF16))
        rows.append(jnp.concatenate(cols, axis=-1))
    gated = jnp.concatenate(rows, axis=0)
    x1 = x + _mod_rows(gt1_ref, tm) * _dot(gated, w_o_ref[...])
    out_refs[0][...] = _ffn_tail(x1, g2_ref, sh2_ref, sc2_ref, gt2_ref, w1_ref, w2_ref, ff_chunk)


def _sgu_layer(x, mods, mod_spec, p, tm, ff_chunk, period, offset, emit_v):
    n, d = x.shape
    width = p["w_o"].shape[0]
    row = lambda w: pl.BlockSpec((tm, w), lambda i: (i, 0))
    cs = lambda a: _const_spec(a.shape)
    out_specs = [row(d)]
    out_shape = [jax.ShapeDtypeStruct((n, d), F32)]
    if emit_v:
        out_specs.append(row(width))
        out_shape.append(jax.ShapeDtypeStruct((n, width), F32))
    return pl.pallas_call(
        functools.partial(_sgu_layer_kernel, ff_chunk=ff_chunk, period=period, offset=offset, emit_v=emit_v),
        grid=(n // tm,),
        in_specs=[row(d), cs(p["g1"]), mod_spec(0), mod_spec(1), mod_spec(2), cs(p["w_in"]), cs(p["g_v"]),
                  cs(p["w_s"]), cs(p["b_s"]), cs(p["w_o"]), cs(p["g2"]), mod_spec(3), mod_spec(4), mod_spec(5),
                  _layer_spec(p["w1"], p["layer"]), _layer_spec(p["w2"], p["layer"])],
        out_specs=out_specs,
        out_shape=out_shape,
        compiler_params=_cparams(1),
        name="sgu_layer_ffn",
    )(x, p["g1"], mods, mods, mods, p["w_in"], p["g_v"], p["w_s"], p["b_s"], p["w_o"],
      p["g2"], mods, mods, mods, p["w1"], p["w2"])


def _rope_tables(pos):
    half = QK_ROPE_DIM // 2
    inv = 1.0 / (ROPE_THETA ** (jnp.arange(half, dtype=F32) / half))
    n = pos.shape[0]
    step = CHUNK
    if n % step == 0 and np.array_equal(pos, np.arange(n)):
        ang_a = (step * jnp.arange(n // step, dtype=F32))[:, None] * inv[None, :]
        ang_b = jnp.arange(step, dtype=F32)[:, None] * inv[None, :]
        ca, sa = jnp.cos(ang_a)[:, None, :], jnp.sin(ang_a)[:, None, :]
        cb, sb = jnp.cos(ang_b)[None, :, :], jnp.sin(ang_b)[None, :, :]
        cos = (ca * cb - sa * sb).reshape(n, half)
        sin = (sa * cb + ca * sb).reshape(n, half)
    else:
        ang = jnp.asarray(pos).astype(F32)[:, None] * inv[None, :]
        cos, sin = jnp.cos(ang), jnp.sin(ang)
    z = jnp.zeros((pos.shape[0], LANES - QK_ROPE_DIM), F32)
    return jnp.concatenate([cos, cos, z], axis=-1), jnp.concatenate([-sin, sin, z], axis=-1)


def _pad_lanes(g, n):
    return jnp.pad(g, (0, n - g.shape[0])).reshape(1, n)


def _prep_mla(a, norm1_g, norm2_g, ffn_w1, ffn_w2, layer, mla_w_in, mla_g_qa, mla_g_kva, mla_w_q_up,
              mla_w_uk, mla_w_uv, mla_g_qn, mla_g_qr, mla_g_kn, mla_g_kr, mla_w_o):
    d = mla_w_in.shape[1]
    w_in = mla_w_in[a]
    w_in = jnp.concatenate([w_in[:, Q_LORA_RANK:Q_LORA_RANK + KV_LORA_RANK],
                            jnp.pad(w_in[:, Q_LORA_RANK + KV_LORA_RANK:], ((0, 0), (0, LANES - QK_ROPE_DIM))),
                            w_in[:, :Q_LORA_RANK]], axis=1).astype(BF16)
    half = QK_ROPE_DIM // 2
    w_q_up = mla_w_q_up[a].reshape(Q_LORA_RANK, N_HEADS, QK_NOPE_DIM + QK_ROPE_DIM)
    w_r1 = w_q_up[:, :, QK_NOPE_DIM:QK_NOPE_DIM + half]
    w_r2 = w_q_up[:, :, QK_NOPE_DIM + half:]
    w_z = jnp.zeros((Q_LORA_RANK, N_HEADS, LANES - QK_ROPE_DIM), w_q_up.dtype)
    w_q_up = jnp.concatenate([w_q_up[:, :, :QK_NOPE_DIM], w_r1, w_r2, w_z, w_r2, w_r1, w_z], axis=-1)
    g_qr = mla_g_qr[a]
    g_qr = jnp.concatenate([_pad_lanes(g_qr, LANES),
                            _pad_lanes(jnp.concatenate([g_qr[half:], g_qr[:half]]), LANES)], axis=0)
    return {
        "g1": norm1_g[layer].reshape(1, d), "g2": norm2_g[layer].reshape(1, d),
        "w1": ffn_w1, "w2": ffn_w2, "layer": layer,
        "w_in": w_in,
        "g_qa": mla_g_qa[a].reshape(1, -1), "g_kva": mla_g_kva[a].reshape(1, -1),
        "g_kr": _pad_lanes(mla_g_kr[a], LANES),
        "w_q_up": w_q_up.reshape(Q_LORA_RANK, N_HEADS * Q_HEAD_COLS).astype(BF16),
        "g_qn": mla_g_qn[a].reshape(1, -1), "g_qr": g_qr,
        "w_uk": mla_w_uk[a].reshape(KV_LORA_RANK, N_HEADS * QK_NOPE_DIM).astype(BF16),
        "g_kn": mla_g_kn[a].reshape(1, -1),
        "w_uv": jnp.swapaxes(mla_w_uv[a], 0, 1).astype(BF16),
        "w_o": mla_w_o[a].astype(BF16),
    }


def _prep_sgu(b, norm1_g, norm2_g, ffn_w1, ffn_w2, layer, cm_w_in, cm_g_v, cm_w_s, cm_b_s, cm_w_o, idx):
    d = cm_w_in.shape[1]
    period = len(idx)
    lo = int(idx[0])
    assert SGU_CHUNK % period == 0 and np.array_equal(idx, lo + np.arange(period)), idx
    reps = SGU_CHUNK // period
    w_s = jnp.tile(cm_w_s[b][:, lo:lo + period, lo:lo + period], (1, reps, reps))
    b_s = jnp.tile(cm_b_s[b][:, lo:lo + period], (1, reps))
    gdim = cm_w_o.shape[1] // SGU_GROUPS
    return {
        "g1": norm1_g[layer].reshape(1, d), "g2": norm2_g[layer].reshape(1, d),
        "w1": ffn_w1, "w2": ffn_w2, "layer": layer,
        "w_in": cm_w_in[b].astype(BF16), "g_v": cm_g_v[b].reshape(1, -1),
        "w_s": w_s, "b_s": jnp.broadcast_to(b_s[:, :, None], (SGU_GROUPS, SGU_CHUNK, gdim)),
        "w_o": cm_w_o[b].astype(BF16),
    }, period, lo


def kernel(x_prompt, x_sample, cache_ckv, cache_kpe, c_prompt, c_sample, ada_w, ada_b, norm1_g, norm2_g,
           ffn_w1, ffn_w2, mla_w_in, mla_g_qa, mla_g_kva, mla_w_q_up, mla_w_uk, mla_w_uv, mla_g_qn,
           mla_g_qr, mla_g_kn, mla_g_kr, mla_w_o, cm_w_in, cm_g_v, cm_w_s, cm_b_s, cm_w_o):
    batch, seq, d = x_prompt.shape
    dec_batch, dec_seq, _ = x_sample.shape
    past = cache_ckv.shape[2]
    depth = ada_w.shape[0]
    n_p, n_s = batch * seq, dec_batch * dec_seq
    tm_p, tm_proj, tm_s = TOKEN_TILE, PROJ_TILE, n_s
    attn_tile, attn_heads, ff_chunk = ATTN_TILE, ATTN_HEADS, FF_CHUNK
    assert seq % tm_proj == 0 and seq % attn_tile == 0 and tm_p == attn_tile and n_s % attn_tile == 0
    assert past % CACHE_TILE == 0 and N_HEADS % attn_heads == 0

    pos_p = np.arange(seq)
    pos_s = past + np.arange(dec_seq)
    rc_p, rs_p = _rope_tables(pos_p)
    rc_s, rs_s = _rope_tables(np.tile(pos_s, dec_batch))
    rope_spec_p = pl.BlockSpec((tm_proj, LANES), lambda i: (i % (seq // tm_proj), 0))
    rope_spec_s = pl.BlockSpec((tm_s, LANES), lambda i: (i, 0))

    n_c = batch + dec_batch
    c_all = jnp.concatenate([c_prompt, c_sample], axis=0)
    c_all = jnp.pad(c_all, ((0, -n_c % BF16_ROWS), (0, 0)))
    mod = _ada_modulation(c_all, ada_w, ada_b)
    m_p = mod[:, :batch].reshape(depth, batch, 1, 6 * d)
    m_s = mod[:, batch:n_c]

    def prompt_mods(layer, tm):
        return lambda j: pl.BlockSpec((None, None, 1, d), lambda i: (layer, i // (seq // tm), 0, j))

    def sample_mods(layer):
        return lambda j: pl.BlockSpec((None, tm_s // dec_seq, d), lambda i: (layer, i, j))

    x_p = x_prompt.reshape(n_p, d)
    x_s = x_sample.reshape(n_s, d)
    ckv_p_l, kpe_p_l, ckv_s_l, kpe_s_l, v_s_l = [], [], [], [], []
    for layer in range(depth):
        mod_spec_p, mod_spec_proj, mod_spec_s = prompt_mods(layer, tm_p), prompt_mods(layer, tm_proj), sample_mods(layer)
        if layer % 2 == 0:
            a = layer // 2
            p = _prep_mla(a, norm1_g, norm2_g, ffn_w1, ffn_w2, layer, mla_w_in, mla_g_qa, mla_g_kva,
                          mla_w_q_up, mla_w_uk, mla_w_uv, mla_g_qn, mla_g_qr, mla_g_kn, mla_g_kr, mla_w_o)
            side = (ffn_w1, ffn_w2, mla_w_o, cm_w_in, cm_w_o) if ffn_w1.dtype != BF16 else ()
            q_p, k_p, ckv_p, _, ckvt_p, kr_p, *cast = _mla_project(x_p, m_p, mod_spec_proj, p, rc_p, rs_p,
                                                                   rope_spec_p, tm_proj, attn_tile, side)
            if side:
                ffn_w1, ffn_w2, mla_w_o, cm_w_in, cm_w_o = cast
                p["w1"], p["w2"], p["w_o"] = ffn_w1, ffn_w2, mla_w_o[a]
            q_s, k_s, ckv_s, ckvb_s, _, kr_s = _mla_project(x_s, m_s, mod_spec_s, p, rc_s, rs_s, rope_spec_s,
                                                            tm_s, attn_tile)
            ol_p = _prompt_attention(q_p, k_p, ckvt_p, batch, seq, attn_tile, attn_heads)
            ol_s = _sample_attention(q_s, k_s, ckvb_s, cache_ckv, cache_kpe, a, p["w_uk"].T, p["g_kn"],
                                     dec_seq, CACHE_TILE)
            x_p = _mla_tail(x_p, ol_p, m_p, mod_spec_proj, p, tm_proj, ff_chunk, True)
            x_s = _mla_tail(x_s, ol_s, m_s, mod_spec_s, p, tm_s, ff_chunk, False)
            ckv_p_l.append(ckv_p.reshape(batch, seq, -1))
            kpe_p_l.append(kr_p.reshape(batch, seq, -1))
            ckv_s_l.append(ckv_s.reshape(dec_batch, dec_seq, -1))
            kpe_s_l.append(kr_s.reshape(dec_batch, dec_seq, -1))
        else:
            b = layer // 2
            args = (b, norm1_g, norm2_g, ffn_w1, ffn_w2, layer, cm_w_in, cm_g_v, cm_w_s, cm_b_s, cm_w_o)
            pp, period_p, off_p = _prep_sgu(*args, np.arange(SGU_CHUNK))
            ps, period_s, off_s = _prep_sgu(*args, pos_s % SGU_CHUNK)
            (x_p,) = _sgu_layer(x_p, m_p, mod_spec_proj, pp, tm_proj, ff_chunk, period_p, off_p, False)
            x_s, v_s = _sgu_layer(x_s, m_s, mod_spec_s, ps, tm_s, ff_chunk, period_s, off_s, True)
            v_s_l.append(v_s.reshape(dec_batch, dec_seq, -1))

    return (x_p.reshape(batch, seq, d), x_s.reshape(dec_batch, dec_seq, d),
            jnp.stack(ckv_p_l), jnp.stack(kpe_p_l), jnp.stack(ckv_s_l), jnp.stack(kpe_s_l),
            jnp.stack(v_s_l))
```
